```python
import math
import jax, jax.numpy as jnp
from jax import lax
import numpy as np

D_MODEL = 2048
BATCH = 2
SEQ = 4096
DEPTH = 4
DEC_BATCH = 128
DEC_SEQ = 4
PAST_LEN = 8192
PAGE_SIZE = 128

N_MIXERS = 3
N_RWKV = (DEPTH + 2) // 3
N_MLA = (DEPTH + 1) // 3
N_GLA = DEPTH // 3
NORM_EPS = 1e-6

RW_HEAD_DIM = 64
RW_HEADS = D_MODEL // RW_HEAD_DIM
RW_DECAY_LORA = 96
RW_A_LORA = 96
RW_V_LORA = 64
RW_GN_EPS = 64e-5

MLA_HEADS = 16
MLA_NOPE = 128
MLA_ROPE = 64
MLA_V = 128
MLA_QK = MLA_NOPE + MLA_ROPE
MLA_Q_LORA = 512
MLA_KV_LORA = 512
MLA_WIDTH = MLA_HEADS * MLA_V
MLA_SCALE = MLA_QK ** -0.5
ROPE_THETA = 10000.0
Q_BLOCK = 128

GLA_HEADS = 4
GLA_DK = D_MODEL // 2 // GLA_HEADS
GLA_DV = D_MODEL // GLA_HEADS
GLA_QK = GLA_HEADS * GLA_DK
GLA_GATE_LORA = 16
GLA_TAU = 16.0
GLA_CHUNK = 64

kernel_name = 'hybrid_rwkv7_mla_gla_adaln_decode_step'


def rms_norm(x, g):
    xf = x.astype(jnp.float32)
    y = xf * lax.rsqrt(jnp.mean(xf * xf, axis=-1, keepdims=True) + NORM_EPS)
    return (y * g.astype(jnp.float32)).astype(x.dtype)


def modulate(x, c, g_pre, w_ada, b_ada):
    m = jax.nn.silu(c) @ w_ada + b_ada
    shift, scale, gate = jnp.split(m[:, None, :], 3, axis=-1)
    return rms_norm(x, g_pre) * (1 + scale) + shift, gate


def residual(x, out, gate, g_post):
    return x + gate * rms_norm(out, g_post)


def apply_rope(x, pos):
    half = x.shape[-1] // 2
    inv_freq = ROPE_THETA ** (-jnp.arange(half, dtype=jnp.float32) / half)
    ang = pos.astype(jnp.float32)[:, None] * inv_freq[None, :]
    ang = ang.reshape((ang.shape[0],) + (1,) * (x.ndim - 3) + (half,))
    cos, sin = jnp.cos(ang), jnp.sin(ang)
    xf = x.astype(jnp.float32)
    x1, x2 = xf[..., :half], xf[..., half:]
    return jnp.concatenate([x1 * cos - x2 * sin, x2 * cos + x1 * sin], axis=-1).astype(x.dtype)


def rwkv7_mixer(h, shift_prev, S0, v_first, vres, mu, w_in, w0, w1, w2, a0, a1, a2,
                k_k, k_a, r_k, ln_w, ln_b, w_out):
    B, T, D = h.shape
    f32 = jnp.float32
    h_prev = jnp.concatenate([shift_prev[:, None, :].astype(h.dtype), h[:, :-1]], axis=1)
    dx = h_prev - h
    x4 = h[:, :, None, :] + dx[:, :, None, :] * mu[:4]
    r, k, v, z = jnp.moveaxis(jnp.einsum('btcd,cde->btce', x4, w_in), 2, 0)
    xw = h + dx * mu[4]
    xa = h + dx * mu[5]
    w_raw = (w0 + jnp.tanh(xw @ w1) @ w2).astype(f32)
    decay = jnp.exp(-jnp.exp(-jax.nn.softplus(-w_raw) - 0.5))
    a = jax.nn.sigmoid((a0 + (xa @ a1) @ a2).astype(f32))
    r, k, v = r.astype(f32), k.astype(f32), v.astype(f32)
    if vres is None:
        v_first = v
    else:
        v0, v1, v2 = vres
        v = v + (v_first - v) * jax.nn.sigmoid((v0 + (x4[:, :, 2] @ v1) @ v2).astype(f32))
    heads = lambda t: t.reshape(B, T, RW_HEADS, RW_HEAD_DIM)
    kk = heads(k * k_k.astype(f32))
    kk = kk / jnp.maximum(jnp.linalg.norm(kk, axis=-1, keepdims=True), 1e-12)
    k = k * (1 + (a - 1) * k_a.astype(f32))
    r_h, w_h, k_h, v_h, a_h = heads(r), heads(decay), heads(k), heads(v), heads(a)

    def step(S, inp):
        r_t, w_t, k_t, v_t, kk_t, a_t = inp
        sa = jnp.einsum('bhvk,bhk->bhv', S, -kk_t)
        S = S * w_t[:, :, None, :] + sa[..., None] * (kk_t * a_t)[:, :, None, :] \
            + v_t[..., None] * k_t[:, :, None, :]
        return S, jnp.einsum('bhvk,bhk->bhv', S, r_t)

    xs = tuple(jnp.moveaxis(t, 1, 0) for t in (r_h, w_h, k_h, v_h, kk, a_h))
    S_T, o = lax.scan(step, S0.astype(f32), xs)
    o = jnp.moveaxis(o, 0, 1)
    mean = jnp.mean(o, axis=-1, keepdims=True)
    var = jnp.mean(jnp.square(o - mean), axis=-1, keepdims=True)
    o = ((o - mean) * lax.rsqrt(var + RW_GN_EPS)).reshape(B, T, D) * ln_w.astype(f32) + ln_b.astype(f32)
    bonus = jnp.sum(r_h * k_h * r_k.astype(f32), axis=-1, keepdims=True) * v_h
    o = (o + bonus.reshape(B, T, D)).astype(h.dtype) * jax.nn.silu(z)
    return o @ w_out, h[:, -1], S_T.astype(S0.dtype), v_first


def mla_project(h, pos, w_in, g_q, g_kv, w_uq):
    cq, ckv, kr, z = jnp.split(h @ w_in, [MLA_Q_LORA, MLA_Q_LORA + MLA_KV_LORA,
                                          MLA_Q_LORA + MLA_KV_LORA + MLA_ROPE], axis=-1)
    q = jnp.einsum('btc,chd->bthd', rms_norm(cq, g_q), w_uq)
    q_nope = q[..., :MLA_NOPE]
    q_rope = apply_rope(q[..., MLA_NOPE:], pos)
    return q_nope, q_rope, rms_norm(ckv, g_kv), apply_rope(kr, pos), z


def mla_attend_prompt(q_nope, q_rope, ckv, k_rope, w_uk, w_uv):
    B, S = q_nope.shape[:2]
    k_nope = jnp.einsum('bsc,chd->bshd', ckv, w_uk)
    v = jnp.einsum('bsc,chd->bshd', ckv, w_uv)
    nb = S // Q_BLOCK
    to_blocks = lambda t: jnp.swapaxes(t.reshape((B, nb, Q_BLOCK) + t.shape[2:]), 0, 1)
    kpos = jnp.arange(S)

    def block(args):
        qn, qr, start = args
        s = (jnp.einsum('bqhd,bshd->bhqs', qn, k_nope)
             + jnp.einsum('bqhr,bsr->bhqs', qr, k_rope)).astype(jnp.float32) * MLA_SCALE
        qpos = start + jnp.arange(Q_BLOCK)
        s = jnp.where(kpos[None, :] <= qpos[:, None], s, -jnp.inf)
        p = jax.nn.softmax(s, axis=-1).astype(v.dtype)
        return jnp.einsum('bhqs,bshd->bqhd', p, v)

    o = lax.map(block, (to_blocks(q_nope), to_blocks(q_rope), jnp.arange(nb) * Q_BLOCK))
    return jnp.swapaxes(o, 0, 1).reshape(B, S, MLA_HEADS, MLA_V)


def mla_attend_sample(q_nope, q_rope, ckv_new, kr_new, cache_ckv, cache_krope, layer, page_table, w_uk, w_uv):
    f32 = jnp.float32
    DB, T = q_nope.shape[:2]
    q_lat = jnp.einsum('bthd,chd->bthc', q_nope, w_uk)

    def page_step(carry, phys):
        m, l, acc = carry
        ck = cache_ckv[layer, phys]
        kr = cache_krope[layer, phys]
        s = (jnp.einsum('bthc,bpc->bhtp', q_lat, ck)
             + jnp.einsum('bthr,bpr->bhtp', q_rope, kr)).astype(f32) * MLA_SCALE
        m_new = jnp.maximum(m, jnp.max(s, axis=-1))
        p = jnp.exp(s - m_new[..., None])
        corr = jnp.exp(m - m_new)
        l = l * corr + jnp.sum(p, axis=-1)
        acc = acc * corr[..., None] + jnp.einsum('bhtp,bpc->bhtc', p, ck.astype(f32))
        return (m_new, l, acc), None

    init = (jnp.full((DB, MLA_HEADS, T), -1e30, f32), jnp.zeros((DB, MLA_HEADS, T), f32),
            jnp.zeros((DB, MLA_HEADS, T, MLA_KV_LORA), f32))
    (m, l, acc), _ = lax.scan(page_step, init, page_table.T)
    s = (jnp.einsum('bthc,bsc->bhts', q_lat, ckv_new)
         + jnp.einsum('bthr,bsr->bhts', q_rope, kr_new)).astype(f32) * MLA_SCALE
    causal = jnp.tril(jnp.ones((T, T), dtype=bool))
    s = jnp.where(causal, s, -jnp.inf)
    m_new = jnp.maximum(m, jnp.max(s, axis=-1))
    p = jnp.exp(s - m_new[..., None])
    corr = jnp.exp(m - m_new)
    l = l * corr + jnp.sum(p, axis=-1)
    acc = acc * corr[..., None] + jnp.einsum('bhts,bsc->bhtc', p, ckv_new.astype(f32))
    o_lat = (acc / l[..., None]).astype(w_uv.dtype)
    return jnp.einsum('bhtc,chd->bthd', o_lat, w_uv)


def mla_out(o, z, w_out):
    B, T = o.shape[:2]
    return (o.reshape(B, T, MLA_WIDTH) * jax.nn.silu(z)) @ w_out


def gla_recurrence(q, k, v, log_a, S0):
    B, T, H, DK = q.shape
    L = math.gcd(T, GLA_CHUNK)
    n = T // L
    chunks = lambda t: t.astype(jnp.float32).reshape(B, n, L, H, t.shape[-1]).transpose(1, 0, 3, 2, 4)
    causal = jnp.tril(jnp.ones((L, L), dtype=bool))[None, None, :, :, None]

    def step(S, inp):
        qc, kc, vc, gc = inp
        b = jnp.cumsum(gc, axis=2)
        rel = jnp.exp(jnp.where(causal, b[:, :, :, None, :] - b[:, :, None, :, :], -jnp.inf))
        att = jnp.einsum('bhtd,bhsd,bhtsd->bhts', qc, kc, rel)
        o = jnp.einsum('bhtd,bhdv->bhtv', qc * jnp.exp(b), S) + jnp.einsum('bhts,bhsv->bhtv', att, vc)
        b_end = b[:, :, -1:, :]
        S = S * jnp.exp(b_end[:, :, 0, :, None]) + jnp.einsum('bhsd,bhsv->bhdv', kc * jnp.exp(b_end - b), vc)
        return S, o

    S_T, o = lax.scan(step, S0.astype(jnp.float32), (chunks(q), chunks(k), chunks(v), chunks(log_a)))
    return o.transpose(1, 0, 3, 2, 4).reshape(B, T, H, -1), S_T


def gla_mixer(h, S0, w_in, w_g2, b_g, g_norm, w_out):
    B, T, _ = h.shape
    q, k, v, z, gl = jnp.split(h @ w_in, [GLA_QK, 2 * GLA_QK, 2 * GLA_QK + D_MODEL,
                                          2 * GLA_QK + 2 * D_MODEL], axis=-1)
    log_a = jax.nn.log_sigmoid((gl @ w_g2 + b_g).astype(jnp.float32)) / GLA_TAU
    hd = lambda t, d: t.reshape(B, T, GLA_HEADS, d)
    o, S_T = gla_recurrence(hd(q * GLA_DK ** -0.5, GLA_DK), hd(k, GLA_DK), hd(v, GLA_DV),
                            hd(log_a, GLA_DK), S0)
    o = rms_norm(o, g_norm).reshape(B, T, D_MODEL).astype(h.dtype) * jax.nn.silu(z)
    return o @ w_out, S_T.astype(S0.dtype)


def setup_inputs(seed: int = 0) -> dict:
    key = jax.random.key(seed)
    keys = jax.random.split(key, 43)
    idx = iter(range(43))
    D = D_MODEL
    n_pages = PAST_LEN // PAGE_SIZE
    n_phys = (5 * DEC_BATCH * n_pages + 3) // 4

    def nrm(shape, scale=1.0):
        return jax.random.normal(keys[next(idx)], shape, jnp.float32) * scale

    def uni(shape, lo, hi):
        return jax.random.uniform(keys[next(idx)], shape, jnp.float32, lo, hi)

    return {
        'x_prompt': nrm((BATCH, SEQ, D)),
        'x_sample': nrm((DEC_BATCH, DEC_SEQ, D)),
        'c_prompt': nrm((BATCH, D)),
        'c_sample': nrm((DEC_BATCH, D)),
        'state_rwkv_wkv': nrm((N_RWKV, DEC_BATCH, RW_HEADS, RW_HEAD_DIM, RW_HEAD_DIM), 0.3),
        'state_rwkv_shift': nrm((N_RWKV, DEC_BATCH, D)),
        'cache_mla_ckv': nrm((N_MLA, n_phys, PAGE_SIZE, MLA_KV_LORA)),
        'cache_mla_krope': nrm((N_MLA, n_phys, PAGE_SIZE, MLA_ROPE)),
        'page_table': jax.random.permutation(keys[next(idx)], n_phys)[:DEC_BATCH * n_pages]
                      .reshape(DEC_BATCH, n_pages).astype(jnp.int32),
        'state_gla': nrm((N_GLA, DEC_BATCH, GLA_HEADS, GLA_DK, GLA_DV)),
        'norm_pre': 1.0 + nrm((DEPTH, D), 0.02),
        'norm_post': 1.0 + nrm((DEPTH, D), 0.02),
        'ada_w': nrm((DEPTH, D, 3 * D), 0.5 * D ** -0.5),
        'ada_b': nrm((DEPTH, 3 * D), 0.02),
        'rw_mu': uni((N_RWKV, 6, D), 0.0, 1.0),
        'rw_w_in': nrm((N_RWKV, 4, D, D), D ** -0.5),
        'rw_w0': uni((N_RWKV, D), -6.0, -1.0),
        'rw_w1': nrm((N_RWKV, D, RW_DECAY_LORA), D ** -0.5),
        'rw_w2': nrm((N_RWKV, RW_DECAY_LORA, D), 0.1 * RW_DECAY_LORA ** -0.5),
        'rw_a0': nrm((N_RWKV, D), 0.1),
        'rw_a1': nrm((N_RWKV, D, RW_A_LORA), D ** -0.5),
        'rw_a2': nrm((N_RWKV, RW_A_LORA, D), 0.1 * RW_A_LORA ** -0.5),
        'rw_v0': nrm((N_RWKV - 1, D), 0.1),
        'rw_v1': nrm((N_RWKV - 1, D, RW_V_LORA), D ** -0.5),
        'rw_v2': nrm((N_RWKV - 1, RW_V_LORA, D), 0.1 * RW_V_LORA ** -0.5),
        'rw_k_k': 0.85 + nrm((N_RWKV, D), 0.02),
        'rw_k_a': 1.0 + nrm((N_RWKV, D), 0.02),
        'rw_r_k': nrm((N_RWKV, RW_HEADS, RW_HEAD_DIM), 0.1),
        'rw_ln_w': 1.0 + nrm((N_RWKV, D), 0.02),
        'rw_ln_b': nrm((N_RWKV, D), 0.02),
        'rw_w_out': nrm((N_RWKV, D, D), D ** -0.5),
        'mla_w_in': nrm((N_MLA, D, MLA_Q_LORA + MLA_KV_LORA + MLA_ROPE + MLA_WIDTH), D ** -0.5),
        'mla_q_norm': 1.0 + nrm((N_MLA, MLA_Q_LORA), 0.02),
        'mla_kv_norm': 1.0 + nrm((N_MLA, MLA_KV_LORA), 0.02),
        'mla_w_uq': nrm((N_MLA, MLA_Q_LORA, MLA_HEADS, MLA_QK), MLA_Q_LORA ** -0.5),
        'mla_w_uk': nrm((N_MLA, MLA_KV_LORA, MLA_HEADS, MLA_NOPE), MLA_KV_LORA ** -0.5),
        'mla_w_uv': nrm((N_MLA, MLA_KV_LORA, MLA_HEADS, MLA_V), MLA_KV_LORA ** -0.5),
        'mla_w_out': nrm((N_MLA, MLA_WIDTH, D), MLA_WIDTH ** -0.5),
        'gla_w_in': nrm((N_GLA, D, 2 * GLA_QK + 2 * D + GLA_GATE_LORA), D ** -0.5),
        'gla_w_g2': nrm((N_GLA, GLA_GATE_LORA, GLA_QK), GLA_GATE_LORA ** -0.5),
        'gla_b_g': nrm((N_GLA, GLA_QK), 0.1),
        'gla_norm': 1.0 + nrm((N_GLA, GLA_DV), 0.02),
        'gla_w_out': nrm((N_GLA, D, D), D ** -0.5),
    }


def reference(x_prompt, x_sample, c_prompt, c_sample, state_rwkv_wkv, state_rwkv_shift,
              cache_mla_ckv, cache_mla_krope, page_table, state_gla,
              norm_pre, norm_post, ada_w, ada_b,
              rw_mu, rw_w_in, rw_w0, rw_w1, rw_w2, rw_a0, rw_a1, rw_a2, rw_v0, rw_v1, rw_v2,
              rw_k_k, rw_k_a, rw_r_k, rw_ln_w, rw_ln_b, rw_w_out,
              mla_w_in, mla_q_norm, mla_kv_norm, mla_w_uq, mla_w_uk, mla_w_uv, mla_w_out,
              gla_w_in, gla_w_g2, gla_b_g, gla_norm, gla_w_out):
    B, T_p = x_prompt.shape[:2]
    T_s = x_sample.shape[1]
    pos_p = jnp.arange(T_p, dtype=jnp.int32)
    pos_s = PAST_LEN + jnp.arange(T_s, dtype=jnp.int32)
    yp, ys = x_prompt, x_sample
    vf_p = vf_s = None
    p_wkv, p_shift, p_ckv, p_kr, p_gla = [], [], [], [], []
    s_wkv, s_shift, s_ckv, s_kr, s_gla = [], [], [], [], []
    for i in range(DEPTH):
        kind, j = i % N_MIXERS, i // N_MIXERS
        hp, gp = modulate(yp, c_prompt, norm_pre[i], ada_w[i], ada_b[i])
        hs, gs = modulate(ys, c_sample, norm_pre[i], ada_w[i], ada_b[i])
        if kind == 0:
            vres = None if j == 0 else (rw_v0[j - 1], rw_v1[j - 1], rw_v2[j - 1])
            rw = (rw_mu[j], rw_w_in[j], rw_w0[j], rw_w1[j], rw_w2[j], rw_a0[j], rw_a1[j], rw_a2[j],
                  rw_k_k[j], rw_k_a[j], rw_r_k[j], rw_ln_w[j], rw_ln_b[j], rw_w_out[j])
            zero_shift = jnp.zeros((B, D_MODEL), hp.dtype)
            zero_wkv = jnp.zeros((B, RW_HEADS, RW_HEAD_DIM, RW_HEAD_DIM), state_rwkv_wkv.dtype)
            op, shp, Sp, vf_p = rwkv7_mixer(hp, zero_shift, zero_wkv, vf_p, vres, *rw)
            os_, shs, Ss, vf_s = rwkv7_mixer(hs, state_rwkv_shift[j], state_rwkv_wkv[j], vf_s, vres, *rw)
            p_wkv.append(Sp)
            p_shift.append(shp)
            s_wkv.append(Ss)
            s_shift.append(shs)
        elif kind == 1:
            qn, qr, ckv, kr, z = mla_project(hp, pos_p, mla_w_in[j], mla_q_norm[j], mla_kv_norm[j], mla_w_uq[j])
            op = mla_out(mla_attend_prompt(qn, qr, ckv, kr, mla_w_uk[j], mla_w_uv[j]), z, mla_w_out[j])
            p_ckv.append(ckv)
            p_kr.append(kr)
            qn, qr, ckv, kr, z = mla_project(hs, pos_s, mla_w_in[j], mla_q_norm[j], mla_kv_norm[j], mla_w_uq[j])
            o_s = mla_attend_sample(qn, qr, ckv, kr, cache_mla_ckv, cache_mla_krope, j, page_table,
                                    mla_w_uk[j], mla_w_uv[j])
            os_ = mla_out(o_s, z, mla_w_out[j])
            s_ckv.append(ckv)
            s_kr.append(kr)
        else:
            zero_gla = jnp.zeros((B, GLA_HEADS, GLA_DK, GLA_DV), state_gla.dtype)
            op, Sp = gla_mixer(hp, zero_gla, gla_w_in[j], gla_w_g2[j], gla_b_g[j], gla_norm[j], gla_w_out[j])
            os_, Ss = gla_mixer(hs, state_gla[j], gla_w_in[j], gla_w_g2[j], gla_b_g[j], gla_norm[j], gla_w_out[j])
            p_gla.append(Sp)
            s_gla.append(Ss)
        yp = residual(yp, op, gp, norm_post[i])
        ys = residual(ys, os_, gs, norm_post[i])
    return (yp, ys,
            jnp.stack(p_wkv), jnp.stack(p_shift), jnp.stack(p_ckv), jnp.stack(p_kr), jnp.stack(p_gla),
            jnp.stack(s_wkv), jnp.stack(s_shift), jnp.stack(s_ckv), jnp.stack(s_kr), jnp.stack(s_gla))
```

```python
import functools
import math

import jax
import jax.numpy as jnp
import numpy as np
from jax import lax
from jax.experimental import pallas as pl
from jax.experimental.pallas import tpu as pltpu

F32 = jnp.float32
BF16 = jnp.bfloat16

D = 2048
NORM_EPS = 1e-6
RW_N = 64
RW_H = D // RW_N
RW_LORA_PAD = 128
RW_GN_EPS = 64e-5
RW_TILES = D // 128
RW_FOLD = 128 // RW_H
EXP_M05 = math.exp(-0.5)
MLA_H = 16
MLA_NOPE = 128
MLA_ROPE = 64
MLA_V = 128
MLA_QL = 512
MLA_KVL = 512
MLA_SCALE = (MLA_NOPE + MLA_ROPE) ** -0.5
ROPE_THETA = 10000.0
PAGE = 128
GLA_H = 4
GLA_DK = 256
GLA_DV = 512
GLA_QK = GLA_H * GLA_DK
GLA_LORA_PAD = 256
GLA_TAU = 16.0
GLA_CHUNK = 64

LANE = 128
ROW_TILE = 128
VMEM_LIMIT = 48 * 1024 * 1024


def _cparams(sem):
    return pltpu.CompilerParams(dimension_semantics=sem, vmem_limit_bytes=VMEM_LIMIT)


def _tile(n, pref):
    for t in (1024, 640, 512, 384, 256, 128, 64, 32, 16, 8):
        if t <= pref and n % t == 0:
            return t
    return n


def _sigmoid(x):
    return 1.0 / (1.0 + jnp.exp(-x))


def _silu(x):
    return x * _sigmoid(x)


def _mm_kernel(*refs, has_bias, act_in, act_out, tanh_group):
    if has_bias:
        x_ref, w_ref, b_ref, o_ref = refs
    else:
        x_ref, w_ref, o_ref = refs
    x = x_ref[...]
    if act_in == "silu":
        x = _silu(x.astype(F32))
    acc = jnp.dot(x.astype(BF16), w_ref[...], preferred_element_type=F32)
    if has_bias:
        acc = acc + b_ref[...]
    if act_out == "tanh_group":
        acc = jnp.where(pl.program_id(0) == tanh_group, jnp.tanh(acc), acc)
    elif act_out == "logsig_tau":
        acc = (jnp.minimum(acc, 0.0) - jnp.log(1.0 + jnp.exp(-jnp.abs(acc)))) * (1.0 / GLA_TAU)
    o_ref[...] = acc.astype(o_ref.dtype)


def matmul(x, w, bias=None, *, x_off=0, act_in=None, act_out=None, tanh_group=0,
           out_dtype=F32, tm=512, tn=512, name="mm"):
    G, K, N = w.shape
    M = x.shape[1]
    tm = _tile(M, tm)
    tn = _tile(N, tn)
    in_specs = [
        pl.BlockSpec((None, tm, K), lambda g, i, j: (g + x_off, i, 0)),
        pl.BlockSpec((None, K, tn), lambda g, i, j: (g, 0, j)),
    ]
    args = [x, w]
    if bias is not None:
        in_specs.append(pl.BlockSpec((None, 1, tn), lambda g, i, j: (g, 0, j)))
        args.append(bias)
    return pl.pallas_call(
        functools.partial(_mm_kernel, has_bias=bias is not None, act_in=act_in,
                          act_out=act_out, tanh_group=tanh_group),
        out_shape=jax.ShapeDtypeStruct((G, M, N), out_dtype),
        grid=(G, M // tm, N // tn),
        in_specs=in_specs,
        out_specs=pl.BlockSpec((None, tm, tn), lambda g, i, j: (g, i, j)),
        compiler_params=_cparams(("parallel", "parallel", "arbitrary")),
        name=name,
    )(*args)


def matmul2d(x, w, bias=None, **kw):
    b3 = None if bias is None else bias[None, None, :]
    return matmul(x[None], w[None], b3, **kw)[0]


def _hmm_kernel(x_ref, w_ref, o_ref):
    o_ref[...] = jnp.dot(x_ref[...].astype(BF16), w_ref[...],
                         preferred_element_type=F32).astype(o_ref.dtype)


def head_matmul(x, w, out_dtype, name):
    H, Kh, Nh = w.shape
    M = x.shape[0]
    tm = _tile(M, 512)
    return pl.pallas_call(
        _hmm_kernel,
        out_shape=jax.ShapeDtypeStruct((M, H * Nh), out_dtype),
        grid=(H, M // tm),
        in_specs=[pl.BlockSpec((tm, Kh), lambda h, i: (i, h)),
                  pl.BlockSpec((None, Kh, Nh), lambda h, i: (h, 0, 0))],
        out_specs=pl.BlockSpec((tm, Nh), lambda h, i: (i, h)),
        compiler_params=_cparams(("parallel", "arbitrary")),
        name=name,
    )(x, w)


class RowLayout:
    def __init__(self, B, T, DB, TS):
        self.B, self.T, self.DB, self.TS = B, T, DB, TS
        self.Mp = B * T
        self.Ms = DB * TS
        self.M = self.Mp + self.Ms
        self.te = math.gcd(math.gcd(ROW_TILE, T), self.Ms)
        self.n_p_blocks = self.Mp // self.te
        self.blocks_per_seq = T // self.te
        self.nblocks = self.M // self.te

    def mod_index(self, i):
        return jnp.where(i < self.n_p_blocks, i // self.blocks_per_seq,
                         self.B + i - self.n_p_blocks)

    def expand_mod(self, m):
        mp = jnp.broadcast_to(m[:self.B, None, :], (self.B, self.te, m.shape[-1]))
        ms = jnp.repeat(m[self.B:], self.TS, axis=0).reshape(self.Ms // self.te, self.te, -1)
        return jnp.concatenate([mp, ms], axis=0)


def _rms(x, g):
    ms = jnp.mean(x * x, axis=-1, keepdims=True)
    return x * lax.rsqrt(ms + NORM_EPS) * g


def _prenorm_kernel(x_ref, g_ref, shift_ref, scale_ref, h_ref):
    h_ref[...] = _rms(x_ref[...], g_ref[...]) * (1.0 + scale_ref[...]) + shift_ref[...]


def prenorm(lay, x, g_pre, mod):
    te = lay.te
    row = pl.BlockSpec((te, D), lambda i: (i, 0))
    return pl.pallas_call(
        _prenorm_kernel,
        out_shape=jax.ShapeDtypeStruct((lay.M, D), F32),
        grid=(lay.nblocks,),
        in_specs=[row,
                  pl.BlockSpec((1, D), lambda i: (0, 0)),
                  pl.BlockSpec((None, te, D), lambda i: (lay.mod_index(i), 0, 0)),
                  pl.BlockSpec((None, te, D), lambda i: (lay.mod_index(i), 0, 1))],
        out_specs=row,
        compiler_params=_cparams(("parallel",)),
        name="prenorm",
    )(x, g_pre[None], mod, mod)


def _residual_kernel(x_ref, o_ref, g_ref, gate_ref, y_ref):
    y_ref[...] = x_ref[...] + gate_ref[...] * _rms(o_ref[...], g_ref[...])


def residual(lay, x, o, g_post, mod):
    te = lay.te
    row = pl.BlockSpec((te, D), lambda i: (i, 0))
    return pl.pallas_call(
        _residual_kernel,
        out_shape=jax.ShapeDtypeStruct((lay.M, D), F32),
        grid=(lay.nblocks,),
        in_specs=[row, row,
                  pl.BlockSpec((1, D), lambda i: (0, 0)),
                  pl.BlockSpec((None, te, D), lambda i: (lay.mod_index(i), 0, 2))],
        out_specs=row,
        compiler_params=_cparams(("parallel",)),
        name="residual",
    )(x, o, g_post[None], mod)


RW_MIX_ORDER = (0, 1, 3, 2, 4, 5)


def _rw_mix_kernel(h_ref, hp_ref, mu_ref, x_ref):
    h = h_ref[...]
    dx = hp_ref[...] - h
    for o, c in enumerate(RW_MIX_ORDER):
        x_ref[o] = (h + dx * mu_ref[c:c + 1, :]).astype(BF16)


def rw_mix(lay, h, h_prev, mu):
    te = lay.te
    row = pl.BlockSpec((te, D), lambda i: (i, 0))
    return pl.pallas_call(
        _rw_mix_kernel,
        out_shape=jax.ShapeDtypeStruct((6, lay.M, D), BF16),
        grid=(lay.nblocks,),
        in_specs=[row, row, pl.BlockSpec((6, D), lambda i: (0, 0))],
        out_specs=pl.BlockSpec((6, te, D), lambda i: (0, i, 0)),
        compiler_params=_cparams(("parallel",)),
        name="rw_mix",
    )(h, h_prev, mu)


def _head_fold(s):
    s = s + pltpu.roll(s, RW_H, 1)
    return s + pltpu.roll(s, 2 * RW_H, 1)


def _rw_prep_kernel(*refs, vres):
    if vres:
        p_ref, l_ref, par_ref, vf_ref, out_ref, rk_ref = refs
        iw, ia = 1, 2
    else:
        p_ref, l_ref, par_ref, out_ref, rk_ref = refs
        iw, ia = 0, 1
    rows = out_ref.shape[1]
    n2 = jnp.zeros((rows, LANE), F32)
    rk = jnp.zeros((rows, LANE), F32)
    for j in range(RW_TILES):
        js = slice(j * LANE, (j + 1) * LANE)
        par = lambda i: par_ref[i:i + 1, js]
        r = p_ref[0, :, js]
        k = p_ref[1, :, js]
        v = p_ref[3, :, js]
        decay = jnp.exp(-EXP_M05 * _sigmoid(par(0) + l_ref[iw, :, js]))
        a = _sigmoid(par(1) + l_ref[ia, :, js])
        kk = k * par(2)
        n2 = n2 + kk * kk
        kmod = k * (1.0 + (a - 1.0) * par(3))
        rk = rk + r * kmod * par(4)
        if vres:
            v = v + (vf_ref[:, js] - v) * _sigmoid(par(5) + l_ref[0, :, js])
        out_ref[0, :, js] = decay
        out_ref[1, :, js] = kmod
        out_ref[2, :, js] = kk
        out_ref[3, :, js] = a
        out_ref[4, :, js] = v
    inv = 1.0 / jnp.maximum(jnp.sqrt(_head_fold(n2)), 1e-12)
    rk_ref[...] = _head_fold(rk)
    for j in range(RW_TILES):
        js = slice(j * LANE, (j + 1) * LANE)
        kkn = out_ref[2, :, js] * inv
        out_ref[2, :, js] = kkn
        out_ref[3, :, js] = kkn * out_ref[3, :, js]


def rw_prep(lay, proj, lora, params, v_first):
    te = lay.te
    vres = v_first is not None
    nl = lora.shape[0]
    in_specs = [pl.BlockSpec((4, te, D), lambda i: (0, i, 0)),
                pl.BlockSpec((nl, te, D), lambda i: (0, i, 0)),
                pl.BlockSpec(params.shape, lambda i: (0, 0))]
    args = [proj, lora, params]
    if vres:
        in_specs.append(pl.BlockSpec((te, D), lambda i: (i, 0)))
        args.append(v_first)
    return pl.pallas_call(
        functools.partial(_rw_prep_kernel, vres=vres),
        out_shape=(jax.ShapeDtypeStruct((5, lay.M, D), F32),
                   jax.ShapeDtypeStruct((lay.M, LANE), F32)),
        grid=(lay.nblocks,),
        in_specs=in_specs,
        out_specs=(pl.BlockSpec((5, te, D), lambda i: (0, i, 0)),
                   pl.BlockSpec((te, LANE), lambda i: (i, 0))),
        compiler_params=_cparams(("parallel",)),
        name="rw_prep",
    )(*args)


def _wkv_kernel(r_a, r_b, s_a, s_b, v_a, v_b, s0_ref, rep_ref, o_a, o_b, st_ref,
                vrep_a, vrep_b, *, steps, offs):
    @pl.when(pl.program_id(1) == 0)
    def _():
        st_ref[...] = s0_ref[...]

    if offs != (0, 0):
        o_a[...] = jnp.zeros(o_a.shape, F32)
        o_b[...] = jnp.zeros(o_b.shape, F32)

    seqs = ((r_a, s_a, v_a, o_a, vrep_a, offs[0]), (r_b, s_b, v_b, o_b, vrep_b, offs[1]))
    rep = rep_ref[...]
    for (_, _, v_ref, _, vrep, off) in seqs:
        y = v_ref[off:off + steps].reshape(steps * RW_N, RW_H)
        hi = y.astype(BF16)
        r1 = y - hi.astype(F32)
        mid = r1.astype(BF16)
        lo = (r1 - mid.astype(F32)).astype(BF16)
        yr = (jnp.dot(hi, rep, preferred_element_type=F32)
              + jnp.dot(mid, rep, preferred_element_type=F32)
              + jnp.dot(lo, rep, preferred_element_type=F32))
        vrep[...] = yr.reshape(steps, RW_N, LANE)

    def step(t, carry):
        sas = []
        for q, (_, s_ref, _, _, _, off) in enumerate(seqs):
            p = jnp.zeros((RW_N, LANE), F32)
            for j in range(RW_TILES):
                p = p + st_ref[q, j] * s_ref[2, off + t, j:j + 1, :]
            sas.append(-_head_fold(p))
        for q, (r_ref, s_ref, _, o_ref, vrep, off) in enumerate(seqs):
            sa = sas[q]
            vr = vrep[t]
            o = jnp.zeros((RW_N, LANE), F32)
            for j in range(RW_TILES):
                row = lambda c: s_ref[c, off + t, j:j + 1, :]
                sn = st_ref[q, j] * row(0) + sa * row(3) + vr * row(1)
                st_ref[q, j] = sn
                o = o + sn * r_ref[off + t, j:j + 1, :]
            o_ref[off + t] = _head_fold(o)[:, :RW_H]
        return carry

    lax.fori_loop(0, steps, step, 0)


def wkv(proj4, stack5, s0, *, n_pairs, n_chunks, tb, steps, offs, row_a, row_b, M):
    p4 = proj4.reshape(4, M, RW_TILES, LANE)
    s5 = stack5.reshape(5, M, RW_TILES, LANE)
    v5 = stack5.reshape(5, M, RW_N, RW_H)
    rep = jnp.tile(jnp.eye(RW_H, dtype=BF16), (1, RW_FOLD))
    rspec = lambda f: pl.BlockSpec((None, tb, RW_TILES, LANE), lambda g, c: (0, f(g, c), 0, 0))
    sspec = lambda f: pl.BlockSpec((5, tb, RW_TILES, LANE), lambda g, c: (0, f(g, c), 0, 0))
    vspec = lambda f: pl.BlockSpec((None, tb, RW_N, RW_H), lambda g, c: (4, f(g, c), 0, 0))
    ospec = lambda f: pl.BlockSpec((tb, RW_N, RW_H), lambda g, c: (f(g, c), 0, 0))
    stspec = pl.BlockSpec((2, RW_TILES, RW_N, LANE), lambda g, c: (g, 0, 0, 0))
    o_a, o_b, st = pl.pallas_call(
        functools.partial(_wkv_kernel, steps=steps, offs=offs),
        out_shape=(jax.ShapeDtypeStruct((M, RW_N, RW_H), F32),
                   jax.ShapeDtypeStruct((M, RW_N, RW_H), F32),
                   jax.ShapeDtypeStruct(s0.shape, F32)),
        grid=(n_pairs, n_chunks),
        in_specs=[rspec(row_a), rspec(row_b), sspec(row_a), sspec(row_b),
                  vspec(row_a), vspec(row_b), stspec,
                  pl.BlockSpec((RW_H, LANE), lambda g, c: (0, 0))],
        out_specs=(ospec(row_a), ospec(row_b), stspec),
        scratch_shapes=[pltpu.VMEM((steps, RW_N, LANE), F32),
                        pltpu.VMEM((steps, RW_N, LANE), F32)],
        compiler_params=_cparams(("arbitrary", "arbitrary")),
        name="wkv",
    )(p4, p4, s5, s5, v5, v5, s0, rep)
    return o_a, o_b, st


def _rw_post_kernel(o_ref, v_ref, rk_ref, z_ref, ln_ref, y_ref):
    rows = o_ref.shape[0]
    s = jnp.zeros((rows, LANE), F32)
    for j in range(RW_TILES):
        s = s + o_ref[:, j * LANE:(j + 1) * LANE]
    mean = _head_fold(s) * (1.0 / RW_N)
    s2 = jnp.zeros((rows, LANE), F32)
    for j in range(RW_TILES):
        d = o_ref[:, j * LANE:(j + 1) * LANE] - mean
        s2 = s2 + d * d
    rstd = lax.rsqrt(_head_fold(s2) * (1.0 / RW_N) + RW_GN_EPS)
    rk = rk_ref[...]
    for j in range(RW_TILES):
        js = slice(j * LANE, (j + 1) * LANE)
        o = (o_ref[:, js] - mean) * rstd * ln_ref[0:1, js] + ln_ref[1:2, js]
        o = o + rk * v_ref[:, js]
        y_ref[:, js] = (o * _silu(z_ref[:, js])).astype(BF16)


def rw_post(lay, o, stack5, rk, proj4, ln):
    te = lay.te
    row = pl.BlockSpec((te, D), lambda i: (i, 0))
    return pl.pallas_call(
        _rw_post_kernel,
        out_shape=jax.ShapeDtypeStruct((lay.M, D), BF16),
        grid=(lay.nblocks,),
        in_specs=[row,
                  pl.BlockSpec((None, te, D), lambda i: (4, i, 0)),
                  pl.BlockSpec((te, LANE), lambda i: (i, 0)),
                  pl.BlockSpec((None, te, D), lambda i: (2, i, 0)),
                  pl.BlockSpec((2, D), lambda i: (0, 0))],
        out_specs=row,
        compiler_params=_cparams(("parallel",)),
        name="rw_post",
    )(o, stack5, rk, proj4, ln)


def _rw_perm():
    return np.arange(D).reshape(RW_H, RW_N).T.reshape(-1)


def _state_to_tiles(s):
    n = s.shape[0]
    s = s.reshape(n, RW_H, RW_N, RW_TILES, RW_FOLD)
    return s.transpose(0, 3, 2, 4, 1).reshape(n, RW_TILES, RW_N, LANE)


def _tiles_to_state(s):
    n = s.shape[0]
    s = s.reshape(n, RW_TILES, RW_N, RW_FOLD, RW_H)
    return s.transpose(0, 4, 2, 1, 3).reshape(n, RW_H, RW_N, RW_N)


def _pad_cols(w, n):
    return jnp.pad(w, ((0, 0), (0, n - w.shape[1])))


def _pad_rows(w, n):
    return jnp.pad(w, ((0, n - w.shape[0]), (0, 0)))


def rwkv_layer(lay, h, shift_state, wkv_state, v_first, w):
    B, T, DB, TS, Mp, M = lay.B, lay.T, lay.DB, lay.TS, lay.Mp, lay.M
    perm = _rw_perm()
    hp = h[:Mp].reshape(B, T, D)
    hs = h[Mp:].reshape(DB, TS, D)
    h_prev = jnp.concatenate([
        jnp.concatenate([jnp.zeros((B, 1, D), F32), hp[:, :-1]], axis=1).reshape(Mp, D),
        jnp.concatenate([shift_state[:, None, :], hs[:, :-1]], axis=1).reshape(lay.Ms, D)], axis=0)
    x6 = rw_mix(lay, h, h_prev, w["mu"])
    w_in = w["w_in"][jnp.array([0, 1, 3, 2])][:, :, perm].astype(BF16)
    proj4 = matmul(x6, w_in, name="rw_proj")
    vres = v_first is not None
    l1 = [w["v1"]] if vres else []
    l1 += [w["w1"], w["a1"]]
    l2 = [w["v2"]] if vres else []
    l2 += [w["w2"], w["a2"]]
    l1 = jnp.stack([_pad_cols(a, RW_LORA_PAD) for a in l1]).astype(BF16)
    l2 = jnp.stack([_pad_rows(a, RW_LORA_PAD)[:, perm] for a in l2]).astype(BF16)
    lo1 = matmul(x6, l1, x_off=3 if vres else 4, act_out="tanh_group",
                 tanh_group=1 if vres else 0, out_dtype=BF16, name="rw_lora1")
    lora = matmul(lo1, l2, name="rw_lora2")
    plist = [w["w0"], w["a0"], w["k_k"], w["k_a"], w["r_k"].reshape(-1)]
    if vres:
        plist.append(w["v0"])
    plist += [jnp.zeros((D,), F32)] * (8 - len(plist))
    params = jnp.stack(plist)[:, perm]
    stack5, rk = rw_prep(lay, proj4, lora, params, v_first)
    v_first_out = v_first if vres else proj4[3]

    tb = _tile(T, 64)
    nc = T // tb
    zero_state = jnp.zeros((B, RW_TILES, RW_N, LANE), F32)
    op_a, op_b, p_st = wkv(proj4, stack5, zero_state, n_pairs=1, n_chunks=nc, tb=tb, steps=tb,
                           offs=(0, 0), row_a=lambda g, c: c, row_b=lambda g, c: nc + c, M=M)
    base = Mp // (2 * TS)
    rows_s = lambda g, c: base + g
    os_a, os_b, s_st = wkv(proj4, stack5, _state_to_tiles(wkv_state), n_pairs=DB // 2, n_chunks=1,
                           tb=2 * TS, steps=TS, offs=(0, TS), row_a=rows_s, row_b=rows_s, M=M)
    o = jnp.concatenate([op_a[:T], op_b[T:Mp], os_a[Mp:] + os_b[Mp:]], axis=0).reshape(M, D)

    ln = jnp.stack([w["ln_w"], w["ln_b"]])[:, perm]
    y = rw_post(lay, o, stack5, rk, proj4, ln)
    o_proj = matmul2d(y, w["w_out"][perm, :].astype(BF16), name="rw_out")
    p_shift = hp[:, -1]
    s_shift = hs[:, -1]
    return o_proj, _tiles_to_state(p_st), p_shift, _tiles_to_state(s_st), s_shift, v_first_out


def _rot_cols(w):
    half = MLA_ROPE // 2
    return jnp.concatenate([-w[..., half:], w[..., :half]], axis=-1)


def _mla_prep_kernel(pq_ref, pkv_ref, pkr_ref, gq_ref, gkv_ref, cs_ref,
                     cq_ref, ckv_ref, kr_ref, krb_ref):
    cq_ref[...] = _rms(pq_ref[...], gq_ref[...]).astype(BF16)
    ckv_ref[...] = _rms(pkv_ref[...], gkv_ref[...])
    t2 = pkr_ref[...]
    kr = t2 * cs_ref[0] + pltpu.roll(t2, MLA_ROPE, 1) * cs_ref[1]
    kr_ref[...] = kr[:, :MLA_ROPE]
    krb_ref[...] = kr.astype(BF16)


def mla_prep(lay, proj, g_q, g_kv, cs):
    te = lay.te
    M = lay.M
    return pl.pallas_call(
        _mla_prep_kernel,
        out_shape=(jax.ShapeDtypeStruct((M, MLA_QL), BF16),
                   jax.ShapeDtypeStruct((M, MLA_KVL), F32),
                   jax.ShapeDtypeStruct((M, MLA_ROPE), F32),
                   jax.ShapeDtypeStruct((M, LANE), BF16)),
        grid=(lay.nblocks,),
        in_specs=[pl.BlockSpec((te, MLA_QL), lambda i: (i, D // MLA_QL)),
                  pl.BlockSpec((te, MLA_KVL), lambda i: (i, D // MLA_KVL + 1)),
                  pl.BlockSpec((te, LANE), lambda i: (i, (D + MLA_QL + MLA_KVL) // LANE)),
                  pl.BlockSpec((1, MLA_QL), lambda i: (0, 0)),
                  pl.BlockSpec((1, MLA_KVL), lambda i: (0, 0)),
                  pl.BlockSpec((2, te, LANE), lambda i: (0, i, 0))],
        out_specs=(pl.BlockSpec((te, MLA_QL), lambda i: (i, 0)),
                   pl.BlockSpec((te, MLA_KVL), lambda i: (i, 0)),
                   pl.BlockSpec((te, MLA_ROPE), lambda i: (i, 0)),
                   pl.BlockSpec((te, LANE), lambda i: (i, 0))),
        compiler_params=_cparams(("parallel",)),
        name="mla_prep",
    )(proj, proj, proj, g_q[None], g_kv[None], cs)


def _mla_q_kernel(q_ref, cs_ref, qn_ref, qr_ref):
    for h in range(MLA_H):
        qn_ref[:, h * LANE:(h + 1) * LANE] = q_ref[:, 2 * h * LANE:(2 * h + 1) * LANE].astype(BF16)
        t2 = q_ref[:, (2 * h + 1) * LANE:(2 * h + 2) * LANE]
        qr = t2 * cs_ref[0] + pltpu.roll(t2, MLA_ROPE, 1) * cs_ref[1]
        qr_ref[:, h * LANE:(h + 1) * LANE] = qr.astype(BF16)


def mla_q(lay, q, cs):
    te = lay.te
    M = lay.M
    return pl.pallas_call(
        _mla_q_kernel,
        out_shape=(jax.ShapeDtypeStruct((M, MLA_H * LANE), BF16),
                   jax.ShapeDtypeStruct((M, MLA_H * LANE), BF16)),
        grid=(lay.nblocks,),
        in_specs=[pl.BlockSpec((te, MLA_H * 2 * LANE), lambda i: (i, 0)),
                  pl.BlockSpec((2, te, LANE), lambda i: (0, i, 0))],
        out_specs=(pl.BlockSpec((te, MLA_H * LANE), lambda i: (i, 0)),
                   pl.BlockSpec((te, MLA_H * LANE), lambda i: (i, 0))),
        compiler_params=_cparams(("parallel",)),
        name="mla_q",
    )(q, cs)


def _flash_kernel(qn_ref, qr_ref, kn_ref, kr_ref, v_ref, o_ref, *, tq):
    qi = pl.program_id(2)
    q = jnp.concatenate([qn_ref[...], qr_ref[...]], axis=1)
    nt = (((1,), (1,)), ((), ()))

    def scores(start):
        k = jnp.concatenate([kn_ref[pl.ds(start, tq), :], kr_ref[pl.ds(start, tq), :]], axis=1)
        return lax.dot_general(q, k, nt, preferred_element_type=F32) * MLA_SCALE

    def update(s, start, carry):
        m, l, acc = carry
        m_new = jnp.maximum(m, jnp.max(s, axis=1, keepdims=True))
        p = jnp.exp(s - m_new)
        alpha = jnp.exp(m - m_new)
        l = alpha * l + jnp.sum(p, axis=1, keepdims=True)
        acc = alpha * acc + jnp.dot(p.astype(BF16), v_ref[pl.ds(start, tq), :],
                                    preferred_element_type=F32)
        return m_new, l, acc

    def body(ki, carry):
        start = pl.multiple_of(ki * tq, tq)
        return update(scores(start), start, carry)

    init = (jnp.full((tq, 1), -1e30, F32), jnp.zeros((tq, 1), F32), jnp.zeros((tq, MLA_V), F32))
    carry = lax.fori_loop(0, qi, body, init)
    start = pl.multiple_of(qi * tq, tq)
    s = scores(start)
    causal = lax.broadcasted_iota(jnp.int32, (tq, tq), 1) <= lax.broadcasted_iota(jnp.int32, (tq, tq), 0)
    m, l, acc = update(jnp.where(causal, s, -1e30), start, carry)
    o_ref[...] = acc / l


def mla_flash(qn, qr, knv, krb, B, T):
    tq = _tile(T, 512)
    nq = T // tq
    return pl.pallas_call(
        functools.partial(_flash_kernel, tq=tq),
        out_shape=jax.ShapeDtypeStruct((B * T, MLA_H * MLA_V), F32),
        grid=(B, MLA_H, nq),
        in_specs=[pl.BlockSpec((tq, LANE), lambda b, h, i: (b * nq + i, h)),
                  pl.BlockSpec((tq, LANE), lambda b, h, i: (b * nq + i, h)),
                  pl.BlockSpec((T, LANE), lambda b, h, i: (b, h)),
                  pl.BlockSpec((T, LANE), lambda b, h, i: (b, 0)),
                  pl.BlockSpec((T, LANE), lambda b, h, i: (b, MLA_H + h))],
        out_specs=pl.BlockSpec((tq, MLA_V), lambda b, h, i: (b * nq + i, h)),
        compiler_params=_cparams(("parallel", "parallel", "arbitrary")),
        name="mla_flash",
    )(qn, qr, knv, krb, knv)


def _paged_kernel(pt_ref, ql_ref, qr_ref, cn_ref, kn_ref, *rest, pg, ts):
    ck_refs = rest[:pg]
    kr_refs = rest[pg:2 * pg]
    o_ref, m_ref, l_ref, acc_ref = rest[2 * pg:]
    p = pl.program_id(1)
    nt = (((1,), (1,)), ((), ()))

    @pl.when(p == 0)
    def _():
        m_ref[...] = jnp.full(m_ref.shape, -1e30, F32)
        l_ref[...] = jnp.zeros(l_ref.shape, F32)
        acc_ref[...] = jnp.zeros(acc_ref.shape, F32)

    ql = ql_ref[...]
    qr = qr_ref[...][:, :MLA_ROPE]
    m = m_ref[...]
    l = l_ref[...]
    acc = acc_ref[...]
    for i in range(pg):
        ck = ck_refs[i][...].astype(BF16)
        kr = kr_refs[i][...].astype(BF16)
        s = (lax.dot_general(ql, ck, nt, preferred_element_type=F32)
             + lax.dot_general(qr, kr, nt, preferred_element_type=F32)) * MLA_SCALE
        m_new = jnp.maximum(m, jnp.max(s, axis=1, keepdims=True))
        pr = jnp.exp(s - m_new)
        alpha = jnp.exp(m - m_new)
        l = alpha * l + jnp.sum(pr, axis=1, keepdims=True)
        acc = alpha * acc + jnp.dot(pr.astype(BF16), ck, preferred_element_type=F32)
        m = m_new
    m_ref[...] = m
    l_ref[...] = l
    acc_ref[...] = acc

    @pl.when(p == pl.num_programs(1) - 1)
    def _():
        qlf = ql.astype(F32)
        qrf = qr.astype(F32)
        tok = lax.broadcasted_iota(jnp.int32, (ts * MLA_H, 1), 0) // MLA_H
        cols = []
        for j in range(ts):
            cn = cn_ref[j:j + 1, :].astype(BF16).astype(F32)
            kn = kn_ref[j:j + 1, :MLA_ROPE].astype(F32)
            sj = (jnp.sum(qlf * cn, axis=1, keepdims=True)
                  + jnp.sum(qrf * kn, axis=1, keepdims=True)) * MLA_SCALE
            cols.append(jnp.where(tok >= j, sj, -1e30))
        m2 = m
        for sj in cols:
            m2 = jnp.maximum(m2, sj)
        alpha = jnp.exp(m - m2)
        l2 = alpha * l
        acc2 = alpha * acc
        for j, sj in enumerate(cols):
            pj = jnp.exp(sj - m2)
            l2 = l2 + pj
            acc2 = acc2 + pj.astype(BF16).astype(F32) * cn_ref[j:j + 1, :].astype(BF16).astype(F32)
        o_ref[...] = (acc2 / l2).astype(o_ref.dtype)


def mla_paged(page_table, ql, qr, ckv_new, krb_new, cache_ckv, cache_kr, layer, DB, TS):
    n_pages = page_table.shape[1]
    pg = _tile(n_pages, 8) if n_pages >= 8 else n_pages
    rows = TS * MLA_H
    pt = page_table.reshape(-1)

    def page_spec(i, width):
        return pl.BlockSpec((None, None, PAGE, width),
                            lambda b, p, pt_ref: (layer, pt_ref[b * n_pages + p * pg + i], 0, 0))

    per_b = lambda width: pl.BlockSpec((None, rows, width), lambda b, p, pt_ref: (b, 0, 0))
    new_b = lambda width: pl.BlockSpec((None, TS, width), lambda b, p, pt_ref: (b, 0, 0))
    grid_spec = pltpu.PrefetchScalarGridSpec(
        num_scalar_prefetch=1,
        grid=(DB, n_pages // pg),
        in_specs=[per_b(MLA_KVL), per_b(LANE), new_b(MLA_KVL), new_b(LANE)]
                 + [page_spec(i, MLA_KVL) for i in range(pg)]
                 + [page_spec(i, MLA_ROPE) for i in range(pg)],
        out_specs=per_b(MLA_KVL),
        scratch_shapes=[pltpu.VMEM((rows, 1), F32), pltpu.VMEM((rows, 1), F32),
                        pltpu.VMEM((rows, MLA_KVL), F32)],
    )
    return pl.pallas_call(
        functools.partial(_paged_kernel, pg=pg, ts=TS),
        out_shape=jax.ShapeDtypeStruct((DB, rows, MLA_KVL), BF16),
        grid_spec=grid_spec,
        compiler_params=_cparams(("parallel", "arbitrary")),
        name="mla_paged",
    )(pt, ql, qr, ckv_new, krb_new, *([cache_ckv] * pg), *([cache_kr] * pg))


def _gate_kernel(o_ref, z_ref, y_ref):
    y_ref[...] = (o_ref[...] * _silu(z_ref[...])).astype(BF16)


def gate_mul(lay, o, zsrc, zcol):
    te = lay.te
    return pl.pallas_call(
        _gate_kernel,
        out_shape=jax.ShapeDtypeStruct((lay.M, D), BF16),
        grid=(lay.nblocks,),
        in_specs=[pl.BlockSpec((te, D), lambda i: (i, 0)),
                  pl.BlockSpec((te, D), lambda i: (i, zcol))],
        out_specs=pl.BlockSpec((te, D), lambda i: (i, 0)),
        compiler_params=_cparams(("parallel",)),
        name="gate_mul",
    )(o, zsrc)


def mla_layer(lay, h, cache_ckv, cache_kr, layer, page_table, w):
    B, T, DB, TS, Mp, M = lay.B, lay.T, lay.DB, lay.TS, lay.Mp, lay.M
    w_in = w["w_in"]
    w_kr = w_in[:, 2 * MLA_QL:2 * MLA_QL + MLA_ROPE]
    w1 = jnp.concatenate([w_in[:, 2 * MLA_QL + MLA_ROPE:], w_in[:, :2 * MLA_QL],
                          w_kr, _rot_cols(w_kr)], axis=1)
    proj = matmul2d(h, w1.astype(BF16), tn=640, name="mla_proj")
    pos = jnp.concatenate([jnp.tile(jnp.arange(T, dtype=F32), B),
                           jnp.tile(page_table.shape[1] * PAGE + jnp.arange(TS, dtype=F32), DB)])
    half = MLA_ROPE // 2
    inv_freq = ROPE_THETA ** (-jnp.arange(half, dtype=F32) / half)
    ang = pos[:, None] * inv_freq[None, :]
    zeros = jnp.zeros((M, MLA_ROPE), F32)
    cs = jnp.stack([jnp.concatenate([jnp.cos(ang), jnp.cos(ang), zeros], axis=1),
                    jnp.concatenate([jnp.sin(ang), jnp.sin(ang), zeros], axis=1)])
    cq, ckv, kr, krb = mla_prep(lay, proj, w["q_norm"], w["kv_norm"], cs)
    w_uq = w["w_uq"]
    wq = jnp.concatenate([w_uq, _rot_cols(w_uq[..., MLA_NOPE:])], axis=-1)
    q = matmul2d(cq, wq.reshape(MLA_QL, MLA_H * 2 * LANE).astype(BF16), name="mla_qproj")
    qn, qr = mla_q(lay, q, cs)
    w_kv = jnp.concatenate([w["w_uk"].reshape(MLA_KVL, -1), w["w_uv"].reshape(MLA_KVL, -1)], axis=1)
    knv = matmul2d(ckv[:Mp], w_kv.astype(BF16), out_dtype=BF16, name="mla_kv")
    o_p = mla_flash(qn, qr, knv, krb, B, T)
    w_ukT = w["w_uk"].transpose(1, 2, 0).astype(BF16)
    ql = head_matmul(qn[Mp:], w_ukT, BF16, "mla_qlat")
    o_lat = mla_paged(page_table, ql.reshape(DB, TS * MLA_H, MLA_KVL),
                      qr[Mp:].reshape(DB, TS * MLA_H, LANE),
                      ckv[Mp:].reshape(DB, TS, MLA_KVL), krb[Mp:].reshape(DB, TS, LANE),
                      cache_ckv, cache_kr, layer, DB, TS)
    o_s = head_matmul(o_lat.reshape(lay.Ms, MLA_H * MLA_KVL),
                      w["w_uv"].transpose(1, 0, 2).astype(BF16), F32, "mla_ouv")
    o = jnp.concatenate([o_p, o_s], axis=0)
    y = gate_mul(lay, o, proj, 0)
    o_proj = matmul2d(y, w["w_out"].astype(BF16), name="mla_out")
    return (o_proj, ckv[:Mp].reshape(B, T, MLA_KVL), kr[:Mp].reshape(B, T, MLA_ROPE),
            ckv[Mp:].reshape(DB, TS, MLA_KVL), kr[Mp:].reshape(DB, TS, MLA_ROPE))


def _gla_kernel(q_ref, k_ref, v_ref, g_ref, s0_ref, o_ref, sT_ref, st_ref, *, L):
    c = pl.program_id(2)
    small = L < 16

    @pl.when(c == 0)
    def _():
        st_ref[...] = s0_ref[...].T

    q = q_ref[...] * (GLA_DK ** -0.5)
    k = k_ref[...]
    v = v_ref[...]
    trow = lax.broadcasted_iota(jnp.int32, (L, 1), 0)
    b = g_ref[...]
    sh = 1
    while sh < L:
        b = b + jnp.where(trow >= sh, pltpu.roll(b, sh, 0), 0.0)
        sh *= 2
    col = lax.broadcasted_iota(jnp.int32, (L, L), 1)
    st = st_ref[...]
    nt = (((1,), (1,)), ((), ()))
    tn = (((0,), (0,)), ((), ()))
    o = lax.dot_general((q * jnp.exp(b)).astype(BF16), st.astype(BF16), nt, preferred_element_type=F32)
    att = jnp.zeros((L, L), F32)
    for s in range(L):
        e = jnp.where(trow >= s, b - b[s:s + 1, :], -jnp.inf)
        a_s = jnp.sum(q * k[s:s + 1, :] * jnp.exp(e), axis=1, keepdims=True)
        if small:
            o = o + a_s.astype(BF16).astype(F32) * v[s:s + 1, :].astype(BF16).astype(F32)
        else:
            att = jnp.where(col == s, a_s, att)
    if not small:
        o = o + jnp.dot(att.astype(BF16), v.astype(BF16), preferred_element_type=F32)
    o_ref[...] = o
    b_end = b[L - 1:L, :]
    kd = k * jnp.exp(b_end - b)
    if small:
        upd = lax.dot_general(v.astype(BF16).astype(F32), kd.astype(BF16).astype(F32), tn,
                              preferred_element_type=F32, precision=lax.Precision.HIGHEST)
    else:
        upd = lax.dot_general(v.astype(BF16), kd.astype(BF16), tn, preferred_element_type=F32)
    st = st * jnp.exp(b_end) + upd
    st_ref[...] = st

    @pl.when(c == pl.num_programs(2) - 1)
    def _():
        sT_ref[...] = st.T


def gla_scan(q, k, v, g, s0, *, nseq, nchunk, L, row0, qcol, kcol, vcol):
    rows = lambda n, h, c: row0 + n * nchunk + c
    return pl.pallas_call(
        functools.partial(_gla_kernel, L=L),
        out_shape=(jax.ShapeDtypeStruct((q.shape[0], GLA_H * GLA_DV), F32),
                   jax.ShapeDtypeStruct((nseq, GLA_H, GLA_DK, GLA_DV), F32)),
        grid=(nseq, GLA_H, nchunk),
        in_specs=[pl.BlockSpec((L, GLA_DK), lambda n, h, c: (rows(n, h, c), qcol + h)),
                  pl.BlockSpec((L, GLA_DK), lambda n, h, c: (rows(n, h, c), kcol + h)),
                  pl.BlockSpec((L, GLA_DV), lambda n, h, c: (rows(n, h, c), vcol + h)),
                  pl.BlockSpec((L, GLA_DK), lambda n, h, c: (rows(n, h, c), h)),
                  pl.BlockSpec((None, None, GLA_DK, GLA_DV), lambda n, h, c: (n, h, 0, 0))],
        out_specs=(pl.BlockSpec((L, GLA_DV), lambda n, h, c: (rows(n, h, c), h)),
                   pl.BlockSpec((None, None, GLA_DK, GLA_DV), lambda n, h, c: (n, h, 0, 0))),
        scratch_shapes=[pltpu.VMEM((GLA_DV, GLA_DK), F32)],
        compiler_params=_cparams(("parallel", "parallel", "arbitrary")),
        name="gla_scan",
    )(q, k, v, g, s0)


def _gla_post_kernel(o_ref, z_ref, g_ref, y_ref):
    for h in range(GLA_H):
        hs = slice(h * GLA_DV, (h + 1) * GLA_DV)
        y_ref[:, hs] = (_rms(o_ref[:, hs], g_ref[...]) * _silu(z_ref[:, hs])).astype(BF16)


def gla_post(lay, o, proj, g_norm):
    te = lay.te
    return pl.pallas_call(
        _gla_post_kernel,
        out_shape=jax.ShapeDtypeStruct((lay.M, D), BF16),
        grid=(lay.nblocks,),
        in_specs=[pl.BlockSpec((te, D), lambda i: (i, 0)),
                  pl.BlockSpec((te, D), lambda i: (i, 2)),
                  pl.BlockSpec((1, GLA_DV), lambda i: (0, 0))],
        out_specs=pl.BlockSpec((te, D), lambda i: (i, 0)),
        compiler_params=_cparams(("parallel",)),
        name="gla_post",
    )(o, proj, g_norm[None])


def gla_layer(lay, h, state, w):
    B, T, DB, TS, Mp, M = lay.B, lay.T, lay.DB, lay.TS, lay.Mp, lay.M
    w_in = _pad_cols(w["w_in"], 2 * GLA_QK + 2 * D + GLA_LORA_PAD)
    proj = matmul2d(h, w_in.astype(BF16), tn=640, name="gla_proj")
    gl = proj[:, 2 * GLA_QK + 2 * D:]
    g = matmul2d(gl, _pad_rows(w["w_g2"], GLA_LORA_PAD).astype(BF16), w["b_g"],
                 act_out="logsig_tau", name="gla_gate")
    L = math.gcd(T, GLA_CHUNK)
    nk = GLA_QK // GLA_DK
    o_p, p_state = gla_scan(proj, proj, proj, g, jnp.zeros((B, GLA_H, GLA_DK, GLA_DV), F32),
                            nseq=B, nchunk=T // L, L=L, row0=0,
                            qcol=0, kcol=nk, vcol=2 * GLA_QK // GLA_DV)
    LS = 8
    pad = lambda a: jnp.pad(a.reshape(DB, TS, -1), ((0, 0), (0, LS - TS), (0, 0))).reshape(DB * LS, -1)
    ps = pad(proj[Mp:, :2 * GLA_QK + D])
    o_s, s_state = gla_scan(ps, ps, ps, pad(g[Mp:]), state, nseq=DB, nchunk=1, L=LS, row0=0,
                            qcol=0, kcol=nk, vcol=2 * GLA_QK // GLA_DV)
    o_s = o_s.reshape(DB, LS, D)[:, :TS].reshape(lay.Ms, D)
    o = jnp.concatenate([o_p[:Mp], o_s], axis=0)
    y = gla_post(lay, o, proj, w["norm"])
    o_proj = matmul2d(y, w["w_out"].astype(BF16), name="gla_out")
    return o_proj, p_state, s_state


def kernel(x_prompt, x_sample, c_prompt, c_sample, state_rwkv_wkv, state_rwkv_shift, cache_mla_ckv, cache_mla_krope, page_table, state_gla, norm_pre, norm_post, ada_w, ada_b, rw_mu, rw_w_in, rw_w0, rw_w1, rw_w2, rw_a0, rw_a1, rw_a2, rw_v0, rw_v1, rw_v2, rw_k_k, rw_k_a, rw_r_k, rw_ln_w, rw_ln_b, rw_w_out, mla_w_in, mla_q_norm, mla_kv_norm, mla_w_uq, mla_w_uk, mla_w_uv, mla_w_out, gla_w_in, gla_w_g2, gla_b_g, gla_norm, gla_w_out):
    B, T, _ = x_prompt.shape
    DB, TS, _ = x_sample.shape
    depth = norm_pre.shape[0]
    lay = RowLayout(B, T, DB, TS)
    Mp = lay.Mp
    x = jnp.concatenate([x_prompt.reshape(Mp, D), x_sample.reshape(lay.Ms, D)], axis=0)
    c = jnp.concatenate([c_prompt, c_sample], axis=0)[None]
    c = jnp.broadcast_to(c, (depth,) + c.shape[1:])
    mods = matmul(c, ada_w.astype(BF16), ada_b[:, None, :], act_in="silu", name="ada")

    p_wkv, p_shift, p_ckv, p_kr, p_gla = [], [], [], [], []
    s_wkv, s_shift, s_ckv, s_kr, s_gla = [], [], [], [], []
    v_first = None
    for i in range(depth):
        kind, j = i % 3, i // 3
        mod = lay.expand_mod(mods[i])
        h = prenorm(lay, x, norm_pre[i], mod)
        if kind == 0:
            w = dict(mu=rw_mu[j], w_in=rw_w_in[j], w0=rw_w0[j], w1=rw_w1[j], w2=rw_w2[j],
                     a0=rw_a0[j], a1=rw_a1[j], a2=rw_a2[j], k_k=rw_k_k[j], k_a=rw_k_a[j],
                     r_k=rw_r_k[j], ln_w=rw_ln_w[j], ln_b=rw_ln_b[j], w_out=rw_w_out[j])
            if j > 0:
                w.update(v0=rw_v0[j - 1], v1=rw_v1[j - 1], v2=rw_v2[j - 1])
            o, pst, psh, sst, ssh, v_first = rwkv_layer(
                lay, h, state_rwkv_shift[j], state_rwkv_wkv[j], v_first if j > 0 else None, w)
            p_wkv.append(pst)
            p_shift.append(psh)
            s_wkv.append(sst)
            s_shift.append(ssh)
        elif kind == 1:
            w = dict(w_in=mla_w_in[j], q_norm=mla_q_norm[j], kv_norm=mla_kv_norm[j],
                     w_uq=mla_w_uq[j], w_uk=mla_w_uk[j], w_uv=mla_w_uv[j], w_out=mla_w_out[j])
            o, pc, pk, sc, sk = mla_layer(lay, h, cache_mla_ckv, cache_mla_krope, j, page_table, w)
            p_ckv.append(pc)
            p_kr.append(pk)
            s_ckv.append(sc)
            s_kr.append(sk)
        else:
            w = dict(w_in=gla_w_in[j], w_g2=gla_w_g2[j], b_g=gla_b_g[j], norm=gla_norm[j],
                     w_out=gla_w_out[j])
            o, pg, sg = gla_layer(lay, h, state_gla[j], w)
            p_gla.append(pg)
            s_gla.append(sg)
        x = residual(lay, x, o, norm_post[i], mod)
    yp = x[:Mp].reshape(B, T, D)
    ys = x[Mp:].reshape(DB, TS, D)
    return (yp, ys,
            jnp.stack(p_wkv), jnp.stack(p_shift), jnp.stack(p_ckv), jnp.stack(p_kr), jnp.stack(p_gla),
            jnp.stack(s_wkv), jnp.stack(s_shift), jnp.stack(s_ckv), jnp.stack(s_kr), jnp.stack(s_gla))
```

```python
import functools
import math

import jax
import jax.numpy as jnp
import numpy as np
from jax import lax
from jax.experimental import pallas as pl
from jax.experimental.pallas import tpu as pltpu

F32 = jnp.float32
BF16 = jnp.bfloat16

D = 2048
NORM_EPS = 1e-6
RW_N = 64
RW_H = D // RW_N
RW_LORA_PAD = 128
RW_GN_EPS = 64e-5
RW_TILES = D // 128
RW_FOLD = 128 // RW_H
EXP_M05 = math.exp(-0.5)
MLA_H = 16
MLA_NOPE = 128
MLA_ROPE = 64
MLA_V = 128
MLA_QL = 512
MLA_KVL = 512
MLA_SCALE = (MLA_NOPE + MLA_ROPE) ** -0.5
ROPE_THETA = 10000.0
PAGE = 128
GLA_H = 4
GLA_DK = 256
GLA_DV = 512
GLA_QK = GLA_H * GLA_DK
GLA_LORA_PAD = 256
GLA_TAU = 16.0
GLA_CHUNK = 64

LANE = 128
ROW_TILE = 128
VMEM_LIMIT = 48 * 1024 * 1024


def _cparams(sem):
    return pltpu.CompilerParams(dimension_semantics=sem, vmem_limit_bytes=VMEM_LIMIT)


def _tile(n, pref):
    for t in (1024, 640, 512, 384, 256, 128, 64, 32, 16, 8):
        if t <= pref and n % t == 0:
            return t
    return n


def _sigmoid(x):
    return 1.0 / (1.0 + jnp.exp(-x))


def _silu(x):
    return x * _sigmoid(x)


def _mm_kernel(*refs, has_bias, act_in, act_out, tanh_group):
    if has_bias:
        x_ref, w_ref, b_ref, o_ref = refs
    else:
        x_ref, w_ref, o_ref = refs
    x = x_ref[...]
    if act_in == "silu":
        x = _silu(x.astype(F32))
    acc = jnp.dot(x.astype(BF16), w_ref[...], preferred_element_type=F32)
    if has_bias:
        acc = acc + b_ref[...]
    if act_out == "tanh_group":
        acc = jnp.where(pl.program_id(0) == tanh_group, jnp.tanh(acc), acc)
    elif act_out == "logsig_tau":
        acc = (jnp.minimum(acc, 0.0) - jnp.log(1.0 + jnp.exp(-jnp.abs(acc)))) * (1.0 / GLA_TAU)
    o_ref[...] = acc.astype(o_ref.dtype)


def matmul(x, w, bias=None, *, x_off=0, act_in=None, act_out=None, tanh_group=0,
           out_dtype=F32, tm=512, tn=512, name="mm"):
    G, K, N = w.shape
    M = x.shape[1]
    tm = _tile(M, tm)
    tn = _tile(N, tn)
    in_specs = [
        pl.BlockSpec((None, tm, K), lambda g, i, j: (g + x_off, i, 0)),
        pl.BlockSpec((None, K, tn), lambda g, i, j: (g, 0, j)),
    ]
    args = [x, w]
    if bias is not None:
        in_specs.append(pl.BlockSpec((None, 1, tn), lambda g, i, j: (g, 0, j)))
        args.append(bias)
    return pl.pallas_call(
        functools.partial(_mm_kernel, has_bias=bias is not None, act_in=act_in,
                          act_out=act_out, tanh_group=tanh_group),
        out_shape=jax.ShapeDtypeStruct((G, M, N), out_dtype),
        grid=(G, M // tm, N // tn),
        in_specs=in_specs,
        out_specs=pl.BlockSpec((None, tm, tn), lambda g, i, j: (g, i, j)),
        compiler_params=_cparams(("parallel", "parallel", "arbitrary")),
        name=name,
    )(*args)


def matmul2d(x, w, bias=None, **kw):
    b3 = None if bias is None else bias[None, None, :]
    return matmul(x[None], w[None], b3, **kw)[0]


def _hmm_kernel(x_ref, w_ref, o_ref):
    o_ref[...] = jnp.dot(x_ref[...].astype(BF16), w_ref[...],
                         preferred_element_type=F32).astype(o_ref.dtype)


def head_matmul(x, w, out_dtype, name):
    H, Kh, Nh = w.shape
    M = x.shape[0]
    tm = _tile(M, 512)
    return pl.pallas_call(
        _hmm_kernel,
        out_shape=jax.ShapeDtypeStruct((M, H * Nh), out_dtype),
        grid=(H, M // tm),
        in_specs=[pl.BlockSpec((tm, Kh), lambda h, i: (i, h)),
                  pl.BlockSpec((None, Kh, Nh), lambda h, i: (h, 0, 0))],
        out_specs=pl.BlockSpec((tm, Nh), lambda h, i: (i, h)),
        compiler_params=_cparams(("parallel", "arbitrary")),
        name=name,
    )(x, w)


class RowLayout:
    def __init__(self, B, T, DB, TS):
        self.B, self.T, self.DB, self.TS = B, T, DB, TS
        self.Mp = B * T
        self.Ms = DB * TS
        self.M = self.Mp + self.Ms
        self.te = math.gcd(math.gcd(ROW_TILE, T), self.Ms)
        self.n_p_blocks = self.Mp // self.te
        self.blocks_per_seq = T // self.te
        self.nblocks = self.M // self.te

    def mod_index(self, i):
        return jnp.where(i < self.n_p_blocks, i // self.blocks_per_seq,
                         self.B + i - self.n_p_blocks)

    def expand_mod(self, m):
        mp = jnp.broadcast_to(m[:self.B, None, :], (self.B, self.te, m.shape[-1]))
        ms = jnp.repeat(m[self.B:], self.TS, axis=0).reshape(self.Ms // self.te, self.te, -1)
        return jnp.concatenate([mp, ms], axis=0)


def _rms(x, g):
    ms = jnp.mean(x * x, axis=-1, keepdims=True)
    return x * lax.rsqrt(ms + NORM_EPS) * g


def _prenorm_kernel(x_ref, g_ref, shift_ref, scale_ref, h_ref):
    h_ref[...] = _rms(x_ref[...], g_ref[...]) * (1.0 + scale_ref[...]) + shift_ref[...]


def prenorm(lay, x, g_pre, mod):
    te = lay.te
    row = pl.BlockSpec((te, D), lambda i: (i, 0))
    return pl.pallas_call(
        _prenorm_kernel,
        out_shape=jax.ShapeDtypeStruct((lay.M, D), F32),
        grid=(lay.nblocks,),
        in_specs=[row,
                  pl.BlockSpec((1, D), lambda i: (0, 0)),
                  pl.BlockSpec((None, te, D), lambda i: (lay.mod_index(i), 0, 0)),
                  pl.BlockSpec((None, te, D), lambda i: (lay.mod_index(i), 0, 1))],
        out_specs=row,
        compiler_params=_cparams(("parallel",)),
        name="prenorm",
    )(x, g_pre[None], mod, mod)


def _residual_kernel(x_ref, o_ref, g_ref, gate_ref, y_ref):
    y_ref[...] = x_ref[...] + gate_ref[...] * _rms(o_ref[...], g_ref[...])


def residual(lay, x, o, g_post, mod):
    te = lay.te
    row = pl.BlockSpec((te, D), lambda i: (i, 0))
    return pl.pallas_call(
        _residual_kernel,
        out_shape=jax.ShapeDtypeStruct((lay.M, D), F32),
        grid=(lay.nblocks,),
        in_specs=[row, row,
                  pl.BlockSpec((1, D), lambda i: (0, 0)),
                  pl.BlockSpec((None, te, D), lambda i: (lay.mod_index(i), 0, 2))],
        out_specs=row,
        compiler_params=_cparams(("parallel",)),
        name="residual",
    )(x, o, g_post[None], mod)


RW_MIX_ORDER = (0, 1, 3, 2, 4, 5)


def _rw_mix_kernel(h_ref, hp_ref, mu_ref, x_ref):
    h = h_ref[...]
    dx = hp_ref[...] - h
    for o, c in enumerate(RW_MIX_ORDER):
        x_ref[o] = (h + dx * mu_ref[c:c + 1, :]).astype(BF16)


def rw_mix(lay, h, h_prev, mu):
    te = lay.te
    row = pl.BlockSpec((te, D), lambda i: (i, 0))
    return pl.pallas_call(
        _rw_mix_kernel,
        out_shape=jax.ShapeDtypeStruct((6, lay.M, D), BF16),
        grid=(lay.nblocks,),
        in_specs=[row, row, pl.BlockSpec((6, D), lambda i: (0, 0))],
        out_specs=pl.BlockSpec((6, te, D), lambda i: (0, i, 0)),
        compiler_params=_cparams(("parallel",)),
        name="rw_mix",
    )(h, h_prev, mu)


def _head_fold(s):
    s = s + pltpu.roll(s, RW_H, 1)
    return s + pltpu.roll(s, 2 * RW_H, 1)


def _rw_prep_kernel(*refs, vres):
    if vres:
        p_ref, l_ref, par_ref, vf_ref, out_ref, v_ref, rk_ref = refs
        iw, ia = 1, 2
    else:
        p_ref, l_ref, par_ref, out_ref, v_ref, rk_ref = refs
        iw, ia = 0, 1
    rows = out_ref.shape[1]
    n2 = jnp.zeros((rows, LANE), F32)
    rk = jnp.zeros((rows, LANE), F32)
    for j in range(RW_TILES):
        js = slice(j * LANE, (j + 1) * LANE)
        par = lambda i: par_ref[i:i + 1, js]
        r = p_ref[0, :, js]
        k = p_ref[1, :, js]
        v = p_ref[3, :, js]
        decay = jnp.exp(-EXP_M05 * _sigmoid(par(0) + l_ref[iw, :, js]))
        a = _sigmoid(par(1) + l_ref[ia, :, js])
        kk = k * par(2)
        n2 = n2 + kk * kk
        kmod = k * (1.0 + (a - 1.0) * par(3))
        rk = rk + r * kmod * par(4)
        if vres:
            v = v + (vf_ref[:, js] - v) * _sigmoid(par(5) + l_ref[0, :, js])
        out_ref[0, :, js] = decay
        out_ref[1, :, js] = kmod
        out_ref[2, :, js] = kk
        out_ref[3, :, js] = a
        v_ref[:, js] = v
    inv = 1.0 / jnp.maximum(jnp.sqrt(_head_fold(n2)), 1e-12)
    rk_ref[...] = _head_fold(rk)
    for j in range(RW_TILES):
        js = slice(j * LANE, (j + 1) * LANE)
        kkn = out_ref[2, :, js] * inv
        out_ref[2, :, js] = kkn
        out_ref[3, :, js] = kkn * out_ref[3, :, js]


def rw_prep(lay, proj, lora, params, v_first):
    te = lay.te
    vres = v_first is not None
    nl = lora.shape[0]
    in_specs = [pl.BlockSpec((4, te, D), lambda i: (0, i, 0)),
                pl.BlockSpec((nl, te, D), lambda i: (0, i, 0)),
                pl.BlockSpec(params.shape, lambda i: (0, 0))]
    args = [proj, lora, params]
    if vres:
        in_specs.append(pl.BlockSpec((te, D), lambda i: (i, 0)))
        args.append(v_first)
    return pl.pallas_call(
        functools.partial(_rw_prep_kernel, vres=vres),
        out_shape=(jax.ShapeDtypeStruct((4, lay.M, D), F32),
                   jax.ShapeDtypeStruct((lay.M, D), F32),
                   jax.ShapeDtypeStruct((lay.M, LANE), F32)),
        grid=(lay.nblocks,),
        in_specs=in_specs,
        out_specs=(pl.BlockSpec((4, te, D), lambda i: (0, i, 0)),
                   pl.BlockSpec((te, D), lambda i: (i, 0)),
                   pl.BlockSpec((te, LANE), lambda i: (i, 0))),
        compiler_params=_cparams(("parallel",)),
        name="rw_prep",
    )(*args)


def _wkv_kernel(r_a, r_b, s_a, s_b, v_a, v_b, s0_ref, rep_ref, o_a, o_b, st_ref,
                vrep_a, vrep_b, *, steps, offs):
    @pl.when(pl.program_id(1) == 0)
    def _():
        st_ref[...] = s0_ref[...]

    if offs != (0, 0):
        o_a[...] = jnp.zeros(o_a.shape, F32)
        o_b[...] = jnp.zeros(o_b.shape, F32)

    seqs = ((r_a, s_a, v_a, o_a, vrep_a, offs[0]), (r_b, s_b, v_b, o_b, vrep_b, offs[1]))
    for (_, _, v_ref, _, vrep, off) in seqs:
        y = v_ref[off:off + steps].reshape(steps * RW_TILES, LANE)
        hi = y.astype(BF16)
        lo = (y - hi.astype(F32)).astype(BF16)
        for g in range(RW_FOLD):
            yr = (jnp.dot(hi, rep_ref[g], preferred_element_type=F32)
                  + jnp.dot(lo, rep_ref[g], preferred_element_type=F32))
            vrep[:, g * RW_TILES:(g + 1) * RW_TILES, :] = yr.reshape(steps, RW_TILES, LANE)

    lane_group = lax.broadcasted_iota(jnp.int32, (RW_TILES, LANE), 1) // RW_H

    def kk_dot(q, s_ref, row):
        p = jnp.zeros((RW_N, LANE), F32)
        for j in range(RW_TILES):
            p = p + st_ref[q, j] * s_ref[2, row:row + 1, j * LANE:(j + 1) * LANE]
        return p

    def step(t, carry):
        t_next = jnp.minimum(t + 1, steps - 1)
        out, raw = [], []
        for q, (r_ref, s_ref, _, o_ref, vrep, off) in enumerate(seqs):
            sa = -(carry[q] if q == 0 else _head_fold(carry[q]))
            vr = vrep[t]
            w_row = s_ref[0, pl.ds(off + t, 1), :]
            k_row = s_ref[1, pl.ds(off + t, 1), :]
            b_row = s_ref[3, pl.ds(off + t, 1), :]
            r_row = r_ref[pl.ds(off + t, 1), :]
            kk_row = s_ref[2, pl.ds(off + t_next, 1), :]
            o = jnp.zeros((RW_N, LANE), F32)
            p = jnp.zeros((RW_N, LANE), F32)
            for j in range(RW_TILES):
                js = slice(j * LANE, (j + 1) * LANE)
                sn = st_ref[q, j] * w_row[:, js] + sa * b_row[:, js] + vr * k_row[:, js]
                st_ref[q, j] = sn
                o = o + sn * r_row[:, js]
                p = p + sn * kk_row[:, js]
            out.append(_head_fold(p) if q == 0 else p)
            emit(q, jnp.maximum(t - 1, 0), carry[2 + q])
            raw.append(o)
        return tuple(out + raw)

    def emit(q, t, o):
        o_ref, off = seqs[q][3], seqs[q][5]
        o = _head_fold(o)
        res = o[0:RW_TILES]
        for g in range(1, RW_FOLD):
            res = jnp.where(lane_group == g, o[g * RW_TILES:(g + 1) * RW_TILES], res)
        o_ref[off + t] = res

    init = tuple(kk_dot(q, s[1], s[5]) for q, s in enumerate(seqs))
    zero = jnp.zeros((RW_N, LANE), F32)
    last = lax.fori_loop(0, steps, step, (_head_fold(init[0]), init[1], zero, zero))
    for q in range(2):
        emit(q, steps - 1, last[2 + q])


def wkv(proj4, stack4, v, s0, *, n_pairs, n_chunks, tb, steps, offs, row_a, row_b, M):
    v3 = v.reshape(M, RW_TILES, LANE)
    lane = np.arange(LANE)
    rep = jnp.asarray(np.stack([(lane[:, None] // RW_H == g) & (lane[:, None] % RW_H == lane[None, :] % RW_H)
                                for g in range(RW_FOLD)]), BF16)
    rspec = lambda f: pl.BlockSpec((None, tb, D), lambda g, c: (0, f(g, c), 0))
    sspec = lambda f: pl.BlockSpec((4, tb, D), lambda g, c: (0, f(g, c), 0))
    vspec = lambda f: pl.BlockSpec((tb, RW_TILES, LANE), lambda g, c: (f(g, c), 0, 0))
    stspec = pl.BlockSpec((2, RW_TILES, RW_N, LANE), lambda g, c: (g, 0, 0, 0))
    o_a, o_b, st = pl.pallas_call(
        functools.partial(_wkv_kernel, steps=steps, offs=offs),
        out_shape=(jax.ShapeDtypeStruct((M, RW_TILES, LANE), F32),
                   jax.ShapeDtypeStruct((M, RW_TILES, LANE), F32),
                   jax.ShapeDtypeStruct(s0.shape, F32)),
        grid=(n_pairs, n_chunks),
        in_specs=[rspec(row_a), rspec(row_b), sspec(row_a), sspec(row_b),
                  vspec(row_a), vspec(row_b), stspec,
                  pl.BlockSpec((RW_FOLD, LANE, LANE), lambda g, c: (0, 0, 0))],
        out_specs=(vspec(row_a), vspec(row_b), stspec),
        scratch_shapes=[pltpu.VMEM((steps, RW_N, LANE), F32),
                        pltpu.VMEM((steps, RW_N, LANE), F32)],
        compiler_params=_cparams(("arbitrary", "arbitrary")),
        name="wkv",
    )(proj4, proj4, stack4, stack4, v3, v3, s0, rep)
    return o_a, o_b, st


def _rw_post_kernel(o_ref, v_ref, rk_ref, z_ref, ln_ref, y_ref):
    rows = o_ref.shape[0]
    s = jnp.zeros((rows, LANE), F32)
    for j in range(RW_TILES):
        s = s + o_ref[:, j * LANE:(j + 1) * LANE]
    mean = _head_fold(s) * (1.0 / RW_N)
    s2 = jnp.zeros((rows, LANE), F32)
    for j in range(RW_TILES):
        d = o_ref[:, j * LANE:(j + 1) * LANE] - mean
        s2 = s2 + d * d
    rstd = lax.rsqrt(_head_fold(s2) * (1.0 / RW_N) + RW_GN_EPS)
    rk = rk_ref[...]
    for j in range(RW_TILES):
        js = slice(j * LANE, (j + 1) * LANE)
        o = (o_ref[:, js] - mean) * rstd * ln_ref[0:1, js] + ln_ref[1:2, js]
        o = o + rk * v_ref[:, js]
        y_ref[:, js] = (o * _silu(z_ref[:, js])).astype(BF16)


def rw_post(lay, o, v, rk, proj4, ln):
    te = lay.te
    row = pl.BlockSpec((te, D), lambda i: (i, 0))
    return pl.pallas_call(
        _rw_post_kernel,
        out_shape=jax.ShapeDtypeStruct((lay.M, D), BF16),
        grid=(lay.nblocks,),
        in_specs=[row,
                  row,
                  pl.BlockSpec((te, LANE), lambda i: (i, 0)),
                  pl.BlockSpec((None, te, D), lambda i: (2, i, 0)),
                  pl.BlockSpec((2, D), lambda i: (0, 0))],
        out_specs=row,
        compiler_params=_cparams(("parallel",)),
        name="rw_post",
    )(o, v, rk, proj4, ln)


def _head_minor(a):
    lead = a.shape[:-1]
    return a.reshape(lead + (RW_H, RW_N)).swapaxes(-1, -2).reshape(lead + (D,))


def _state_to_tiles(s):
    n = s.shape[0]
    s = s.reshape(n, RW_H, RW_TILES, RW_FOLD, RW_TILES, RW_FOLD)
    return s.transpose(0, 4, 3, 2, 5, 1).reshape(n, RW_TILES, RW_N, LANE)


def _tiles_to_state(s):
    n = s.shape[0]
    s = s.reshape(n, RW_TILES, RW_FOLD, RW_TILES, RW_FOLD, RW_H)
    return s.transpose(0, 5, 3, 2, 1, 4).reshape(n, RW_H, RW_N, RW_N)


def _pad_cols(w, n):
    return jnp.pad(w, ((0, 0), (0, n - w.shape[1])))


def _pad_rows(w, n):
    return jnp.pad(w, ((0, n - w.shape[0]), (0, 0)))


def rwkv_layer(lay, h, shift_state, wkv_state, v_first, w):
    B, T, DB, TS, Mp, M = lay.B, lay.T, lay.DB, lay.TS, lay.Mp, lay.M
    hp = h[:Mp].reshape(B, T, D)
    hs = h[Mp:].reshape(DB, TS, D)
    h_prev = jnp.concatenate([
        jnp.concatenate([jnp.zeros((B, 1, D), F32), hp[:, :-1]], axis=1).reshape(Mp, D),
        jnp.concatenate([shift_state[:, None, :], hs[:, :-1]], axis=1).reshape(lay.Ms, D)], axis=0)
    x6 = rw_mix(lay, h, h_prev, w["mu"])
    w_in = jnp.stack([_head_minor(w["w_in"][c]).astype(BF16) for c in (0, 1, 3, 2)])
    proj4 = matmul(x6, w_in, name="rw_proj")
    vres = v_first is not None
    l1 = [w["v1"]] if vres else []
    l1 += [w["w1"], w["a1"]]
    l2 = [w["v2"]] if vres else []
    l2 += [w["w2"], w["a2"]]
    l1 = jnp.stack([_pad_cols(a, RW_LORA_PAD) for a in l1]).astype(BF16)
    l2 = jnp.stack([_head_minor(_pad_rows(a, RW_LORA_PAD)) for a in l2]).astype(BF16)
    lo1 = matmul(x6, l1, x_off=3 if vres else 4, act_out="tanh_group",
                 tanh_group=1 if vres else 0, out_dtype=BF16, name="rw_lora1")
    lora = matmul(lo1, l2, name="rw_lora2")
    plist = [w["w0"], w["a0"], w["k_k"], w["k_a"], w["r_k"].reshape(-1)]
    if vres:
        plist.append(w["v0"])
    plist += [jnp.zeros((D,), F32)] * (8 - len(plist))
    params = _head_minor(jnp.stack(plist))
    stack4, v, rk = rw_prep(lay, proj4, lora, params, v_first)
    v_first_out = v_first if vres else v

    tb = _tile(T, 128)
    nc = T // tb
    zero_state = jnp.zeros((B, RW_TILES, RW_N, LANE), F32)
    op_a, op_b, p_st = wkv(proj4, stack4, v, zero_state, n_pairs=1, n_chunks=nc, tb=tb, steps=tb,
                           offs=(0, 0), row_a=lambda g, c: c, row_b=lambda g, c: nc + c, M=M)
    base = Mp // (2 * TS)
    rows_s = lambda g, c: base + g
    os_a, os_b, s_st = wkv(proj4, stack4, v, _state_to_tiles(wkv_state), n_pairs=DB // 2, n_chunks=1,
                           tb=2 * TS, steps=TS, offs=(0, TS), row_a=rows_s, row_b=rows_s, M=M)
    o = jnp.concatenate([op_a[:T], op_b[T:Mp], os_a[Mp:] + os_b[Mp:]], axis=0).reshape(M, D)

    ln = _head_minor(jnp.stack([w["ln_w"], w["ln_b"]]))
    y = rw_post(lay, o, v, rk, proj4, ln)
    w_out = w["w_out"].reshape(RW_H, RW_N, D).swapaxes(0, 1).reshape(D, D)
    o_proj = matmul2d(y, w_out.astype(BF16), name="rw_out")
    p_shift = hp[:, -1]
    s_shift = hs[:, -1]
    return o_proj, _tiles_to_state(p_st), p_shift, _tiles_to_state(s_st), s_shift, v_first_out


def _rot_cols(w):
    half = MLA_ROPE // 2
    return jnp.concatenate([-w[..., half:], w[..., :half]], axis=-1)


def _mla_prep_kernel(pq_ref, pkv_ref, pkr_ref, gq_ref, gkv_ref, cs_ref,
                     cq_ref, ckv_ref, kr_ref, krb_ref):
    cq_ref[...] = _rms(pq_ref[...], gq_ref[...]).astype(BF16)
    ckv_ref[...] = _rms(pkv_ref[...], gkv_ref[...])
    t2 = pkr_ref[...]
    kr = t2 * cs_ref[0] + pltpu.roll(t2, MLA_ROPE, 1) * cs_ref[1]
    kr_ref[...] = kr[:, :MLA_ROPE]
    krb_ref[...] = kr.astype(BF16)


def mla_prep(lay, proj, g_q, g_kv, cs):
    te = lay.te
    M = lay.M
    return pl.pallas_call(
        _mla_prep_kernel,
        out_shape=(jax.ShapeDtypeStruct((M, MLA_QL), BF16),
                   jax.ShapeDtypeStruct((M, MLA_KVL), F32),
                   jax.ShapeDtypeStruct((M, MLA_ROPE), F32),
                   jax.ShapeDtypeStruct((M, LANE), BF16)),
        grid=(lay.nblocks,),
        in_specs=[pl.BlockSpec((te, MLA_QL), lambda i: (i, D // MLA_QL)),
                  pl.BlockSpec((te, MLA_KVL), lambda i: (i, D // MLA_KVL + 1)),
                  pl.BlockSpec((te, LANE), lambda i: (i, (D + MLA_QL + MLA_KVL) // LANE)),
                  pl.BlockSpec((1, MLA_QL), lambda i: (0, 0)),
                  pl.BlockSpec((1, MLA_KVL), lambda i: (0, 0)),
                  pl.BlockSpec((2, te, LANE), lambda i: (0, i, 0))],
        out_specs=(pl.BlockSpec((te, MLA_QL), lambda i: (i, 0)),
                   pl.BlockSpec((te, MLA_KVL), lambda i: (i, 0)),
                   pl.BlockSpec((te, MLA_ROPE), lambda i: (i, 0)),
                   pl.BlockSpec((te, LANE), lambda i: (i, 0))),
        compiler_params=_cparams(("parallel",)),
        name="mla_prep",
    )(proj, proj, proj, g_q[None], g_kv[None], cs)


def _mla_q_kernel(q_ref, cs_ref, qn_ref, qr_ref):
    for h in range(MLA_H):
        qn_ref[:, h * LANE:(h + 1) * LANE] = q_ref[:, 2 * h * LANE:(2 * h + 1) * LANE].astype(BF16)
        t2 = q_ref[:, (2 * h + 1) * LANE:(2 * h + 2) * LANE]
        qr = t2 * cs_ref[0] + pltpu.roll(t2, MLA_ROPE, 1) * cs_ref[1]
        qr_ref[:, h * LANE:(h + 1) * LANE] = qr.astype(BF16)


def mla_q(lay, q, cs):
    te = lay.te
    M = lay.M
    return pl.pallas_call(
        _mla_q_kernel,
        out_shape=(jax.ShapeDtypeStruct((M, MLA_H * LANE), BF16),
                   jax.ShapeDtypeStruct((M, MLA_H * LANE), BF16)),
        grid=(lay.nblocks,),
        in_specs=[pl.BlockSpec((te, MLA_H * 2 * LANE), lambda i: (i, 0)),
                  pl.BlockSpec((2, te, LANE), lambda i: (0, i, 0))],
        out_specs=(pl.BlockSpec((te, MLA_H * LANE), lambda i: (i, 0)),
                   pl.BlockSpec((te, MLA_H * LANE), lambda i: (i, 0))),
        compiler_params=_cparams(("parallel",)),
        name="mla_q",
    )(q, cs)


def _flash_kernel(qn_ref, qr_ref, kn_ref, kr_ref, v_ref, o_ref, *, tq):
    qi = pl.program_id(2)
    q = jnp.concatenate([qn_ref[...], qr_ref[...]], axis=1)
    nt = (((1,), (1,)), ((), ()))

    def scores(start):
        k = jnp.concatenate([kn_ref[pl.ds(start, tq), :], kr_ref[pl.ds(start, tq), :]], axis=1)
        return lax.dot_general(q, k, nt, preferred_element_type=F32) * MLA_SCALE

    def update(s, start, carry):
        m, l, acc = carry
        m_new = jnp.maximum(m, jnp.max(s, axis=1, keepdims=True))
        p = jnp.exp(s - m_new)
        alpha = jnp.exp(m - m_new)
        l = alpha * l + jnp.sum(p, axis=1, keepdims=True)
        acc = alpha * acc + jnp.dot(p.astype(BF16), v_ref[pl.ds(start, tq), :],
                                    preferred_element_type=F32)
        return m_new, l, acc

    def body(ki, carry):
        start = pl.multiple_of(ki * tq, tq)
        return update(scores(start), start, carry)

    init = (jnp.full((tq, 1), -1e30, F32), jnp.zeros((tq, 1), F32), jnp.zeros((tq, MLA_V), F32))
    carry = lax.fori_loop(0, qi, body, init)
    start = pl.multiple_of(qi * tq, tq)
    s = scores(start)
    causal = lax.broadcasted_iota(jnp.int32, (tq, tq), 1) <= lax.broadcasted_iota(jnp.int32, (tq, tq), 0)
    m, l, acc = update(jnp.where(causal, s, -1e30), start, carry)
    o_ref[...] = acc / l


def mla_flash(qn, qr, knv, krb, B, T):
    tq = _tile(T, 512)
    nq = T // tq
    return pl.pallas_call(
        functools.partial(_flash_kernel, tq=tq),
        out_shape=jax.ShapeDtypeStruct((B * T, MLA_H * MLA_V), F32),
        grid=(B, MLA_H, nq),
        in_specs=[pl.BlockSpec((tq, LANE), lambda b, h, i: (b * nq + i, h)),
                  pl.BlockSpec((tq, LANE), lambda b, h, i: (b * nq + i, h)),
                  pl.BlockSpec((T, LANE), lambda b, h, i: (b, h)),
                  pl.BlockSpec((T, LANE), lambda b, h, i: (b, 0)),
                  pl.BlockSpec((T, LANE), lambda b, h, i: (b, MLA_H + h))],
        out_specs=pl.BlockSpec((tq, MLA_V), lambda b, h, i: (b * nq + i, h)),
        compiler_params=_cparams(("parallel", "parallel", "arbitrary")),
        name="mla_flash",
    )(qn, qr, knv, krb, knv)


def _paged_kernel(pt_ref, ql_ref, qr_ref, cn_ref, kn_ref, *rest, pg, ts):
    ck_refs = rest[:pg]
    kr_refs = rest[pg:2 * pg]
    o_ref, m_ref, l_ref, acc_ref = rest[2 * pg:]
    p = pl.program_id(1)
    nt = (((1,), (1,)), ((), ()))

    @pl.when(p == 0)
    def _():
        m_ref[...] = jnp.full(m_ref.shape, -1e30, F32)
        l_ref[...] = jnp.zeros(l_ref.shape, F32)
        acc_ref[...] = jnp.zeros(acc_ref.shape, F32)

    rows = ts * MLA_H
    ql = ql_ref[...]
    qr = qr_ref[...][:, :MLA_ROPE]
    qlp = jnp.concatenate([ql, jnp.zeros((LANE - rows, MLA_KVL), BF16)], axis=0)
    ck = jnp.concatenate([r[...].astype(BF16) for r in ck_refs], axis=0)
    kr_t = jnp.concatenate([r[...].astype(BF16) for r in kr_refs], axis=1)
    s_t = lax.dot_general(ck, qlp, nt, preferred_element_type=F32)
    s = (s_t.T[:rows] + jnp.dot(qr, kr_t, preferred_element_type=F32)) * MLA_SCALE
    m = m_ref[...]
    m_new = jnp.maximum(m, jnp.max(s, axis=1, keepdims=True))
    pr = jnp.exp(s - m_new)
    alpha = jnp.exp(m - m_new)
    l = alpha * l_ref[...] + jnp.sum(pr, axis=1, keepdims=True)
    acc = alpha * acc_ref[...] + jnp.dot(pr.astype(BF16), ck, preferred_element_type=F32)
    m = m_new
    m_ref[...] = m
    l_ref[...] = l
    acc_ref[...] = acc

    @pl.when(p == pl.num_programs(1) - 1)
    def _():
        qlf = ql.astype(F32)
        qrf = qr.astype(F32)
        tok = lax.broadcasted_iota(jnp.int32, (ts * MLA_H, 1), 0) // MLA_H
        cols = []
        for j in range(ts):
            cn = cn_ref[j:j + 1, :].astype(BF16).astype(F32)
            kn = kn_ref[j:j + 1, :MLA_ROPE].astype(F32)
            sj = (jnp.sum(qlf * cn, axis=1, keepdims=True)
                  + jnp.sum(qrf * kn, axis=1, keepdims=True)) * MLA_SCALE
            cols.append(jnp.where(tok >= j, sj, -1e30))
        m2 = m
        for sj in cols:
            m2 = jnp.maximum(m2, sj)
        alpha = jnp.exp(m - m2)
        l2 = alpha * l
        acc2 = alpha * acc
        for j, sj in enumerate(cols):
            pj = jnp.exp(sj - m2)
            l2 = l2 + pj
            acc2 = acc2 + pj.astype(BF16).astype(F32) * cn_ref[j:j + 1, :].astype(BF16).astype(F32)
        o_ref[...] = (acc2 / l2).astype(o_ref.dtype)


def mla_paged(page_table, ql, qr, ckv_new, krb_new, cache_ckv, cache_kr, layer, DB, TS):
    n_pages = page_table.shape[1]
    pg = _tile(n_pages, 16) if n_pages >= 8 else n_pages
    rows = TS * MLA_H
    pt = page_table.reshape(-1)

    def page_spec(i, shape):
        return pl.BlockSpec((None, None) + shape,
                            lambda b, p, pt_ref: (layer, pt_ref[b * n_pages + p * pg + i], 0, 0))

    cache_kr_t = jnp.swapaxes(cache_kr, 2, 3)
    per_b = lambda width: pl.BlockSpec((None, rows, width), lambda b, p, pt_ref: (b, 0, 0))
    new_b = lambda width: pl.BlockSpec((None, TS, width), lambda b, p, pt_ref: (b, 0, 0))
    grid_spec = pltpu.PrefetchScalarGridSpec(
        num_scalar_prefetch=1,
        grid=(DB, n_pages // pg),
        in_specs=[per_b(MLA_KVL), per_b(LANE), new_b(MLA_KVL), new_b(LANE)]
                 + [page_spec(i, (PAGE, MLA_KVL)) for i in range(pg)]
                 + [page_spec(i, (MLA_ROPE, PAGE)) for i in range(pg)],
        out_specs=per_b(MLA_KVL),
        scratch_shapes=[pltpu.VMEM((rows, 1), F32), pltpu.VMEM((rows, 1), F32),
                        pltpu.VMEM((rows, MLA_KVL), F32)],
    )
    return pl.pallas_call(
        functools.partial(_paged_kernel, pg=pg, ts=TS),
        out_shape=jax.ShapeDtypeStruct((DB, rows, MLA_KVL), BF16),
        grid_spec=grid_spec,
        compiler_params=_cparams(("parallel", "arbitrary")),
        name="mla_paged",
    )(pt, ql, qr, ckv_new, krb_new, *([cache_ckv] * pg), *([cache_kr_t] * pg))


def _gate_kernel(o_ref, z_ref, y_ref):
    y_ref[...] = (o_ref[...] * _silu(z_ref[...])).astype(BF16)


def gate_mul(lay, o, zsrc, zcol):
    te = lay.te
    return pl.pallas_call(
        _gate_kernel,
        out_shape=jax.ShapeDtypeStruct((lay.M, D), BF16),
        grid=(lay.nblocks,),
        in_specs=[pl.BlockSpec((te, D), lambda i: (i, 0)),
                  pl.BlockSpec((te, D), lambda i: (i, zcol))],
        out_specs=pl.BlockSpec((te, D), lambda i: (i, 0)),
        compiler_params=_cparams(("parallel",)),
        name="gate_mul",
    )(o, zsrc)


def mla_layer(lay, h, cache_ckv, cache_kr, layer, page_table, w):
    B, T, DB, TS, Mp, M = lay.B, lay.T, lay.DB, lay.TS, lay.Mp, lay.M
    w_in = w["w_in"]
    w_kr = w_in[:, 2 * MLA_QL:2 * MLA_QL + MLA_ROPE]
    w1 = jnp.concatenate([w_in[:, 2 * MLA_QL + MLA_ROPE:], w_in[:, :2 * MLA_QL],
                          w_kr, _rot_cols(w_kr)], axis=1)
    proj = matmul2d(h, w1.astype(BF16), tn=640, name="mla_proj")
    pos = jnp.concatenate([jnp.tile(jnp.arange(T, dtype=F32), B),
                           jnp.tile(page_table.shape[1] * PAGE + jnp.arange(TS, dtype=F32), DB)])
    half = MLA_ROPE // 2
    inv_freq = ROPE_THETA ** (-jnp.arange(half, dtype=F32) / half)
    ang = pos[:, None] * inv_freq[None, :]
    zeros = jnp.zeros((M, MLA_ROPE), F32)
    cs = jnp.stack([jnp.concatenate([jnp.cos(ang), jnp.cos(ang), zeros], axis=1),
                    jnp.concatenate([jnp.sin(ang), jnp.sin(ang), zeros], axis=1)])
    cq, ckv, kr, krb = mla_prep(lay, proj, w["q_norm"], w["kv_norm"], cs)
    w_uq = w["w_uq"]
    wq = jnp.concatenate([w_uq, _rot_cols(w_uq[..., MLA_NOPE:])], axis=-1)
    q = matmul2d(cq, wq.reshape(MLA_QL, MLA_H * 2 * LANE).astype(BF16), name="mla_qproj")
    qn, qr = mla_q(lay, q, cs)
    w_kv = jnp.concatenate([w["w_uk"].reshape(MLA_KVL, -1), w["w_uv"].reshape(MLA_KVL, -1)], axis=1)
    knv = matmul2d(ckv[:Mp], w_kv.astype(BF16), out_dtype=BF16, name="mla_kv")
    o_p = mla_flash(qn, qr, knv, krb, B, T)
    w_ukT = w["w_uk"].transpose(1, 2, 0).astype(BF16)
    ql = head_matmul(qn[Mp:], w_ukT, BF16, "mla_qlat")
    o_lat = mla_paged(page_table, ql.reshape(DB, TS * MLA_H, MLA_KVL),
                      qr[Mp:].reshape(DB, TS * MLA_H, LANE),
                      ckv[Mp:].reshape(DB, TS, MLA_KVL), krb[Mp:].reshape(DB, TS, LANE),
                      cache_ckv, cache_kr, layer, DB, TS)
    o_s = head_matmul(o_lat.reshape(lay.Ms, MLA_H * MLA_KVL),
                      w["w_uv"].transpose(1, 0, 2).astype(BF16), F32, "mla_ouv")
    o = jnp.concatenate([o_p, o_s], axis=0)
    y = gate_mul(lay, o, proj, 0)
    o_proj = matmul2d(y, w["w_out"].astype(BF16), name="mla_out")
    return (o_proj, ckv[:Mp].reshape(B, T, MLA_KVL), kr[:Mp].reshape(B, T, MLA_ROPE),
            ckv[Mp:].reshape(DB, TS, MLA_KVL), kr[Mp:].reshape(DB, TS, MLA_ROPE))


def _gla_kernel(q_ref, k_ref, v_ref, g_ref, s0_ref, o_ref, sT_ref, st_ref, *, L):
    c = pl.program_id(2)
    small = L < 16

    @pl.when(c == 0)
    def _():
        st_ref[...] = s0_ref[...].T

    q = q_ref[...] * (GLA_DK ** -0.5)
    k = k_ref[...]
    v = v_ref[...]
    trow = lax.broadcasted_iota(jnp.int32, (L, 1), 0)
    b = g_ref[...]
    sh = 1
    while sh < L:
        b = b + jnp.where(trow >= sh, pltpu.roll(b, sh, 0), 0.0)
        sh *= 2
    col = lax.broadcasted_iota(jnp.int32, (L, L), 1)
    st = st_ref[...]
    nt = (((1,), (1,)), ((), ()))
    tn = (((0,), (0,)), ((), ()))
    o = lax.dot_general((q * jnp.exp(b)).astype(BF16), st.astype(BF16), nt, preferred_element_type=F32)
    att = jnp.zeros((L, L), F32)
    for s in range(L):
        e = jnp.where(trow >= s, b - b[s:s + 1, :], -jnp.inf)
        a_s = jnp.sum(q * k[s:s + 1, :] * jnp.exp(e), axis=1, keepdims=True)
        if small:
            o = o + a_s.astype(BF16).astype(F32) * v[s:s + 1, :].astype(BF16).astype(F32)
        else:
            att = jnp.where(col == s, a_s, att)
    if not small:
        o = o + jnp.dot(att.astype(BF16), v.astype(BF16), preferred_element_type=F32)
    o_ref[...] = o
    b_end = b[L - 1:L, :]
    kd = k * jnp.exp(b_end - b)
    if small:
        upd = lax.dot_general(v.astype(BF16).astype(F32), kd.astype(BF16).astype(F32), tn,
                              preferred_element_type=F32, precision=lax.Precision.HIGHEST)
    else:
        upd = lax.dot_general(v.astype(BF16), kd.astype(BF16), tn, preferred_element_type=F32)
    st = st * jnp.exp(b_end) + upd
    st_ref[...] = st

    @pl.when(c == pl.num_programs(2) - 1)
    def _():
        sT_ref[...] = st.T


def gla_scan(q, k, v, g, s0, *, nseq, nchunk, L, row0, qcol, kcol, vcol):
    rows = lambda n, h, c: row0 + n * nchunk + c
    return pl.pallas_call(
        functools.partial(_gla_kernel, L=L),
        out_shape=(jax.ShapeDtypeStruct((q.shape[0], GLA_H * GLA_DV), F32),
                   jax.ShapeDtypeStruct((nseq, GLA_H, GLA_DK, GLA_DV), F32)),
        grid=(nseq, GLA_H, nchunk),
        in_specs=[pl.BlockSpec((L, GLA_DK), lambda n, h, c: (rows(n, h, c), qcol + h)),
                  pl.BlockSpec((L, GLA_DK), lambda n, h, c: (rows(n, h, c), kcol + h)),
                  pl.BlockSpec((L, GLA_DV), lambda n, h, c: (rows(n, h, c), vcol + h)),
                  pl.BlockSpec((L, GLA_DK), lambda n, h, c: (rows(n, h, c), h)),
                  pl.BlockSpec((None, None, GLA_DK, GLA_DV), lambda n, h, c: (n, h, 0, 0))],
        out_specs=(pl.BlockSpec((L, GLA_DV), lambda n, h, c: (rows(n, h, c), h)),
                   pl.BlockSpec((None, None, GLA_DK, GLA_DV), lambda n, h, c: (n, h, 0, 0))),
        scratch_shapes=[pltpu.VMEM((GLA_DV, GLA_DK), F32)],
        compiler_params=_cparams(("parallel", "parallel", "arbitrary")),
        name="gla_scan",
    )(q, k, v, g, s0)


def _gla_post_kernel(o_ref, z_ref, g_ref, y_ref):
    for h in range(GLA_H):
        hs = slice(h * GLA_DV, (h + 1) * GLA_DV)
        y_ref[:, hs] = (_rms(o_ref[:, hs], g_ref[...]) * _silu(z_ref[:, hs])).astype(BF16)


def gla_post(lay, o, proj, g_norm):
    te = lay.te
    return pl.pallas_call(
        _gla_post_kernel,
        out_shape=jax.ShapeDtypeStruct((lay.M, D), BF16),
        grid=(lay.nblocks,),
        in_specs=[pl.BlockSpec((te, D), lambda i: (i, 0)),
                  pl.BlockSpec((te, D), lambda i: (i, 2)),
                  pl.BlockSpec((1, GLA_DV), lambda i: (0, 0))],
        out_specs=pl.BlockSpec((te, D), lambda i: (i, 0)),
        compiler_params=_cparams(("parallel",)),
        name="gla_post",
    )(o, proj, g_norm[None])


def gla_layer(lay, h, state, w):
    B, T, DB, TS, Mp, M = lay.B, lay.T, lay.DB, lay.TS, lay.Mp, lay.M
    w_in = _pad_cols(w["w_in"], 2 * GLA_QK + 2 * D + GLA_LORA_PAD)
    proj = matmul2d(h, w_in.astype(BF16), tn=640, name="gla_proj")
    gl = proj[:, 2 * GLA_QK + 2 * D:]
    g = matmul2d(gl, _pad_rows(w["w_g2"], GLA_LORA_PAD).astype(BF16), w["b_g"],
                 act_out="logsig_tau", name="gla_gate")
    L = math.gcd(T, GLA_CHUNK)
    nk = GLA_QK // GLA_DK
    o_p, p_state = gla_scan(proj, proj, proj, g, jnp.zeros((B, GLA_H, GLA_DK, GLA_DV), F32),
                            nseq=B, nchunk=T // L, L=L, row0=0,
                            qcol=0, kcol=nk, vcol=2 * GLA_QK // GLA_DV)
    LS = 8
    pad = lambda a: jnp.pad(a.reshape(DB, TS, -1), ((0, 0), (0, LS - TS), (0, 0))).reshape(DB * LS, -1)
    ps = pad(proj[Mp:, :2 * GLA_QK + D])
    o_s, s_state = gla_scan(ps, ps, ps, pad(g[Mp:]), state, nseq=DB, nchunk=1, L=LS, row0=0,
                            qcol=0, kcol=nk, vcol=2 * GLA_QK // GLA_DV)
    o_s = o_s.reshape(DB, LS, D)[:, :TS].reshape(lay.Ms, D)
    o = jnp.concatenate([o_p[:Mp], o_s], axis=0)
    y = gla_post(lay, o, proj, w["norm"])
    o_proj = matmul2d(y, w["w_out"].astype(BF16), name="gla_out")
    return o_proj, p_state, s_state


def kernel(x_prompt, x_sample, c_prompt, c_sample, state_rwkv_wkv, state_rwkv_shift, cache_mla_ckv, cache_mla_krope, page_table, state_gla, norm_pre, norm_post, ada_w, ada_b, rw_mu, rw_w_in, rw_w0, rw_w1, rw_w2, rw_a0, rw_a1, rw_a2, rw_v0, rw_v1, rw_v2, rw_k_k, rw_k_a, rw_r_k, rw_ln_w, rw_ln_b, rw_w_out, mla_w_in, mla_q_norm, mla_kv_norm, mla_w_uq, mla_w_uk, mla_w_uv, mla_w_out, gla_w_in, gla_w_g2, gla_b_g, gla_norm, gla_w_out):
    B, T, _ = x_prompt.shape
    DB, TS, _ = x_sample.shape
    depth = norm_pre.shape[0]
    lay = RowLayout(B, T, DB, TS)
    Mp = lay.Mp
    x = jnp.concatenate([x_prompt.reshape(Mp, D), x_sample.reshape(lay.Ms, D)], axis=0)
    c = jnp.concatenate([c_prompt, c_sample], axis=0)[None]
    c = jnp.broadcast_to(c, (depth,) + c.shape[1:])
    mods = matmul(c, ada_w.astype(BF16), ada_b[:, None, :], act_in="silu", name="ada")

    p_wkv, p_shift, p_ckv, p_kr, p_gla = [], [], [], [], []
    s_wkv, s_shift, s_ckv, s_kr, s_gla = [], [], [], [], []
    v_first = None
    for i in range(depth):
        kind, j = i % 3, i // 3
        mod = lay.expand_mod(mods[i])
        h = prenorm(lay, x, norm_pre[i], mod)
        if kind == 0:
            w = dict(mu=rw_mu[j], w_in=rw_w_in[j], w0=rw_w0[j], w1=rw_w1[j], w2=rw_w2[j],
                     a0=rw_a0[j], a1=rw_a1[j], a2=rw_a2[j], k_k=rw_k_k[j], k_a=rw_k_a[j],
                     r_k=rw_r_k[j], ln_w=rw_ln_w[j], ln_b=rw_ln_b[j], w_out=rw_w_out[j])
            if j > 0:
                w.update(v0=rw_v0[j - 1], v1=rw_v1[j - 1], v2=rw_v2[j - 1])
            o, pst, psh, sst, ssh, v_first = rwkv_layer(
                lay, h, state_rwkv_shift[j], state_rwkv_wkv[j], v_first if j > 0 else None, w)
            p_wkv.append(pst)
            p_shift.append(psh)
            s_wkv.append(sst)
            s_shift.append(ssh)
        elif kind == 1:
            w = dict(w_in=mla_w_in[j], q_norm=mla_q_norm[j], kv_norm=mla_kv_norm[j],
                     w_uq=mla_w_uq[j], w_uk=mla_w_uk[j], w_uv=mla_w_uv[j], w_out=mla_w_out[j])
            o, pc, pk, sc, sk = mla_layer(lay, h, cache_mla_ckv, cache_mla_krope, j, page_table, w)
            p_ckv.append(pc)
            p_kr.append(pk)
            s_ckv.append(sc)
            s_kr.append(sk)
        else:
            w = dict(w_in=gla_w_in[j], w_g2=gla_w_g2[j], b_g=gla_b_g[j], norm=gla_norm[j],
                     w_out=gla_w_out[j])
            o, pg, sg = gla_layer(lay, h, state_gla[j], w)
            p_gla.append(pg)
            s_gla.append(sg)
        x = residual(lay, x, o, norm_post[i], mod)
    yp = x[:Mp].reshape(B, T, D)
    ys = x[Mp:].reshape(DB, TS, D)
    return (yp, ys,
            jnp.stack(p_wkv), jnp.stack(p_shift), jnp.stack(p_ckv), jnp.stack(p_kr), jnp.stack(p_gla),
            jnp.stack(s_wkv), jnp.stack(s_shift), jnp.stack(s_ckv), jnp.stack(s_kr), jnp.stack(s_gla))
```

```python
import functools
import math

import jax
import jax.numpy as jnp
import numpy as np
from jax import lax
from jax.experimental import pallas as pl
from jax.experimental.pallas import tpu as pltpu

F32 = jnp.float32
BF16 = jnp.bfloat16

D = 2048
NORM_EPS = 1e-6
RW_N = 64
RW_H = D // RW_N
RW_LORA_PAD = 128
RW_GN_EPS = 64e-5
RW_TILES = D // 128
RW_FOLD = 128 // RW_H
EXP_M05 = math.exp(-0.5)
MLA_H = 16
MLA_NOPE = 128
MLA_ROPE = 64
MLA_V = 128
MLA_QL = 512
MLA_KVL = 512
MLA_SCALE = (MLA_NOPE + MLA_ROPE) ** -0.5
ROPE_THETA = 10000.0
PAGE = 128
GLA_H = 4
GLA_DK = 256
GLA_DV = 512
GLA_QK = GLA_H * GLA_DK
GLA_LORA_PAD = 256
GLA_TAU = 16.0
GLA_CHUNK = 64
GLA_SUB = 16

LANE = 128
ROW_TILE = 128
VMEM_LIMIT = 48 * 1024 * 1024


def _cparams(sem):
    return pltpu.CompilerParams(dimension_semantics=sem, vmem_limit_bytes=VMEM_LIMIT)


def _tile(n, pref):
    for t in (1024, 640, 512, 384, 256, 128, 64, 32, 16, 8):
        if t <= pref and n % t == 0:
            return t
    return n


def _sigmoid(x):
    return 1.0 / (1.0 + jnp.exp(-x))


def _silu(x):
    return x * _sigmoid(x)


def _mm_kernel(*refs, has_bias, act_in, act_out, tanh_group):
    if has_bias:
        x_ref, w_ref, b_ref, o_ref = refs
    else:
        x_ref, w_ref, o_ref = refs
    x = x_ref[...]
    if act_in == "silu":
        x = _silu(x.astype(F32))
    acc = jnp.dot(x.astype(BF16), w_ref[...].astype(BF16), preferred_element_type=F32)
    if has_bias:
        acc = acc + b_ref[...]
    if act_out == "tanh_group":
        acc = jnp.where(pl.program_id(0) == tanh_group, jnp.tanh(acc), acc)
    elif act_out == "logsig_tau":
        acc = (jnp.minimum(acc, 0.0) - jnp.log(1.0 + jnp.exp(-jnp.abs(acc)))) * (1.0 / GLA_TAU)
    o_ref[...] = acc.astype(o_ref.dtype)


MM_X_TILE_BYTES = 9 * 1024 * 1024
MM_MAX_ROWS = 2304


def _row_tile(M, row_bytes):
    for parts in range(1, M // 8 + 1):
        rows = M // parts
        if M % parts == 0 and rows % 8 == 0 and rows <= MM_MAX_ROWS and rows * row_bytes <= MM_X_TILE_BYTES:
            return rows
    return M


def matmul(x, w, bias=None, *, x_off=0, act_in=None, act_out=None, tanh_group=0,
           out_dtype=F32, tn=512, name="mm"):
    G, K, N = w.shape
    M = x.shape[1]
    tm = _row_tile(M, K * x.dtype.itemsize)
    tn = _tile(N, tn)
    in_specs = [
        pl.BlockSpec((None, tm, K), lambda g, i, j: (g + x_off, i, 0)),
        pl.BlockSpec((None, K, tn), lambda g, i, j: (g, 0, j)),
    ]
    args = [x, w]
    if bias is not None:
        in_specs.append(pl.BlockSpec((None, 1, tn), lambda g, i, j: (g, 0, j)))
        args.append(bias)
    return pl.pallas_call(
        functools.partial(_mm_kernel, has_bias=bias is not None, act_in=act_in,
                          act_out=act_out, tanh_group=tanh_group),
        out_shape=jax.ShapeDtypeStruct((G, M, N), out_dtype),
        grid=(G, M // tm, N // tn),
        in_specs=in_specs,
        out_specs=pl.BlockSpec((None, tm, tn), lambda g, i, j: (g, i, j)),
        compiler_params=_cparams(("parallel", "parallel", "arbitrary")),
        name=name,
    )(*args)


def matmul2d(x, w, bias=None, **kw):
    b3 = None if bias is None else bias[None, None, :]
    return matmul(x[None], w[None], b3, **kw)[0]


def _hmm_kernel(x_ref, w_ref, o_ref):
    o_ref[...] = jnp.dot(x_ref[...].astype(BF16), w_ref[...],
                         preferred_element_type=F32).astype(o_ref.dtype)


def head_matmul(x, w, out_dtype, name):
    H, Kh, Nh = w.shape
    M = x.shape[0]
    tm = _tile(M, 512)
    return pl.pallas_call(
        _hmm_kernel,
        out_shape=jax.ShapeDtypeStruct((M, H * Nh), out_dtype),
        grid=(H, M // tm),
        in_specs=[pl.BlockSpec((tm, Kh), lambda h, i: (i, h)),
                  pl.BlockSpec((None, Kh, Nh), lambda h, i: (h, 0, 0))],
        out_specs=pl.BlockSpec((tm, Nh), lambda h, i: (i, h)),
        compiler_params=_cparams(("parallel", "arbitrary")),
        name=name,
    )(x, w)


class RowLayout:
    def __init__(self, B, T, DB, TS):
        self.B, self.T, self.DB, self.TS = B, T, DB, TS
        self.Mp = B * T
        self.Ms = DB * TS
        self.M = self.Mp + self.Ms
        self.te = math.gcd(math.gcd(ROW_TILE, T), self.Ms)
        self.n_p_blocks = self.Mp // self.te
        self.blocks_per_seq = T // self.te
        self.nblocks = self.M // self.te

    def mod_index(self, i):
        return jnp.where(i < self.n_p_blocks, i // self.blocks_per_seq,
                         self.B + i - self.n_p_blocks)

    def expand_mod(self, m):
        mp = jnp.broadcast_to(m[:self.B, None, :], (self.B, self.te, m.shape[-1]))
        ms = jnp.repeat(m[self.B:], self.TS, axis=0).reshape(self.Ms // self.te, self.te, -1)
        return jnp.concatenate([mp, ms], axis=0)


def _rms(x, g):
    ms = jnp.mean(x * x, axis=-1, keepdims=True)
    return x * lax.rsqrt(ms + NORM_EPS) * g


def _prenorm_kernel(x_ref, g_ref, shift_ref, scale_ref, h_ref):
    h = _rms(x_ref[...], g_ref[...]) * (1.0 + scale_ref[...]) + shift_ref[...]
    h_ref[...] = h.astype(h_ref.dtype)


def prenorm(lay, x, g_pre, mod, out_dtype):
    te = lay.te
    row = pl.BlockSpec((te, D), lambda i: (i, 0))
    return pl.pallas_call(
        _prenorm_kernel,
        out_shape=jax.ShapeDtypeStruct((lay.M, D), out_dtype),
        grid=(lay.nblocks,),
        in_specs=[row,
                  pl.BlockSpec((1, D), lambda i: (0, 0)),
                  pl.BlockSpec((None, te, D), lambda i: (lay.mod_index(i), 0, 0)),
                  pl.BlockSpec((None, te, D), lambda i: (lay.mod_index(i), 0, 1))],
        out_specs=row,
        compiler_params=_cparams(("parallel",)),
        name="prenorm",
    )(x, g_pre[None], mod, mod)


def _residual_kernel(x_ref, o_ref, g_ref, gate_ref, y_ref):
    y_ref[...] = x_ref[...] + gate_ref[...] * _rms(o_ref[...], g_ref[...])


def residual(lay, x, o, g_post, mod):
    te = lay.te
    row = pl.BlockSpec((te, D), lambda i: (i, 0))
    return pl.pallas_call(
        _residual_kernel,
        out_shape=jax.ShapeDtypeStruct((lay.M, D), F32),
        grid=(lay.nblocks,),
        in_specs=[row, row,
                  pl.BlockSpec((1, D), lambda i: (0, 0)),
                  pl.BlockSpec((None, te, D), lambda i: (lay.mod_index(i), 0, 2))],
        out_specs=row,
        compiler_params=_cparams(("parallel",)),
        name="residual",
    )(x, o, g_post[None], mod)


RW_MIX_ORDER = (0, 1, 3, 2, 4, 5)


def _rw_mix_kernel(h_ref, hp_ref, mu_ref, x_ref):
    h = h_ref[...]
    dx = hp_ref[...] - h
    for o, c in enumerate(RW_MIX_ORDER):
        x_ref[o] = (h + dx * mu_ref[c:c + 1, :]).astype(BF16)


def rw_mix(lay, h, h_prev, mu):
    te = lay.te
    row = pl.BlockSpec((te, D), lambda i: (i, 0))
    return pl.pallas_call(
        _rw_mix_kernel,
        out_shape=jax.ShapeDtypeStruct((6, lay.M, D), BF16),
        grid=(lay.nblocks,),
        in_specs=[row, row, pl.BlockSpec((6, D), lambda i: (0, 0))],
        out_specs=pl.BlockSpec((6, te, D), lambda i: (0, i, 0)),
        compiler_params=_cparams(("parallel",)),
        name="rw_mix",
    )(h, h_prev, mu)


def _head_fold(s):
    s = s + pltpu.roll(s, RW_H, 1)
    return s + pltpu.roll(s, 2 * RW_H, 1)


def _rw_prep_kernel(*refs, vres):
    if vres:
        p_ref, l_ref, par_ref, vf_ref, out_ref, v_ref, rk_ref = refs
        iw, ia = 1, 2
    else:
        p_ref, l_ref, par_ref, out_ref, v_ref, rk_ref = refs
        iw, ia = 0, 1
    rows = out_ref.shape[1]
    n2 = jnp.zeros((rows, LANE), F32)
    rk = jnp.zeros((rows, LANE), F32)
    for j in range(RW_TILES):
        js = slice(j * LANE, (j + 1) * LANE)
        par = lambda i: par_ref[i:i + 1, js]
        r = p_ref[0, :, js]
        k = p_ref[1, :, js]
        v = p_ref[3, :, js]
        decay = jnp.exp(-EXP_M05 * _sigmoid(par(0) + l_ref[iw, :, js]))
        a = _sigmoid(par(1) + l_ref[ia, :, js])
        kk = k * par(2)
        n2 = n2 + kk * kk
        kmod = k * (1.0 + (a - 1.0) * par(3))
        rk = rk + r * kmod * par(4)
        if vres:
            v = v + (vf_ref[:, js] - v) * _sigmoid(par(5) + l_ref[0, :, js])
        out_ref[0, :, js] = decay
        out_ref[1, :, js] = kmod
        out_ref[2, :, js] = kk
        out_ref[3, :, js] = a
        v_ref[:, js] = v
    inv = 1.0 / jnp.maximum(jnp.sqrt(_head_fold(n2)), 1e-12)
    rk_ref[...] = _head_fold(rk)
    for j in range(RW_TILES):
        js = slice(j * LANE, (j + 1) * LANE)
        kkn = out_ref[2, :, js] * inv
        out_ref[2, :, js] = kkn
        out_ref[3, :, js] = kkn * out_ref[3, :, js]


def rw_prep(lay, proj, lora, params, v_first):
    te = lay.te
    vres = v_first is not None
    nl = lora.shape[0]
    in_specs = [pl.BlockSpec((4, te, D), lambda i: (0, i, 0)),
                pl.BlockSpec((nl, te, D), lambda i: (0, i, 0)),
                pl.BlockSpec(params.shape, lambda i: (0, 0))]
    args = [proj, lora, params]
    if vres:
        in_specs.append(pl.BlockSpec((te, D), lambda i: (i, 0)))
        args.append(v_first)
    return pl.pallas_call(
        functools.partial(_rw_prep_kernel, vres=vres),
        out_shape=(jax.ShapeDtypeStruct((4, lay.M, D), F32),
                   jax.ShapeDtypeStruct((lay.M, D), F32),
                   jax.ShapeDtypeStruct((lay.M, LANE), F32)),
        grid=(lay.nblocks,),
        in_specs=in_specs,
        out_specs=(pl.BlockSpec((4, te, D), lambda i: (0, i, 0)),
                   pl.BlockSpec((te, D), lambda i: (i, 0)),
                   pl.BlockSpec((te, LANE), lambda i: (i, 0))),
        compiler_params=_cparams(("parallel",)),
        name="rw_prep",
    )(*args)


def _wkv_kernel(r_a, r_b, s_a, s_b, v_a, v_b, s0_ref, rep_ref, *rest, steps, offs):
    o_a, o_b, st_ref, vrep_a, vrep_b = rest[-5:]

    @pl.when(pl.program_id(1) == 0)
    def _():
        st_ref[...] = s0_ref[...]

    if offs != (0, 0):
        o_a[...] = jnp.zeros(o_a.shape, F32)
        o_b[...] = jnp.zeros(o_b.shape, F32)

    seqs = ((r_a, s_a, v_a, o_a, vrep_a, offs[0]), (r_b, s_b, v_b, o_b, vrep_b, offs[1]))
    for (_, _, v_ref, _, vrep, off) in seqs:
        y = v_ref[off:off + steps].reshape(steps * RW_TILES, LANE)
        hi = y.astype(BF16)
        lo = (y - hi.astype(F32)).astype(BF16)
        for g in range(RW_FOLD):
            yr = (jnp.dot(hi, rep_ref[g], preferred_element_type=F32)
                  + jnp.dot(lo, rep_ref[g], preferred_element_type=F32))
            vrep[:, g * RW_TILES:(g + 1) * RW_TILES, :] = yr.reshape(steps, RW_TILES, LANE)

    lane_group = lax.broadcasted_iota(jnp.int32, (RW_TILES, LANE), 1) // RW_H

    def kk_dot(q, s_ref, row):
        p = jnp.zeros((RW_N, LANE), F32)
        for j in range(RW_TILES):
            p = p + st_ref[q, j] * s_ref[2, row:row + 1, j * LANE:(j + 1) * LANE]
        return p

    def step(t, carry):
        t_next = jnp.minimum(t + 1, steps - 1)
        out, raw = [], []
        for q, (r_ref, s_ref, _, o_ref, vrep, off) in enumerate(seqs):
            sa = -(carry[q] if q == 0 else _head_fold(carry[q]))
            vr = vrep[t]
            w_row = s_ref[0, pl.ds(off + t, 1), :]
            k_row = s_ref[1, pl.ds(off + t, 1), :]
            b_row = s_ref[3, pl.ds(off + t, 1), :]
            r_row = r_ref[pl.ds(off + t, 1), :]
            kk_row = s_ref[2, pl.ds(off + t_next, 1), :]
            o = jnp.zeros((RW_N, LANE), F32)
            p = jnp.zeros((RW_N, LANE), F32)
            for j in range(RW_TILES):
                js = slice(j * LANE, (j + 1) * LANE)
                sn = st_ref[q, j] * w_row[:, js] + sa * b_row[:, js] + vr * k_row[:, js]
                st_ref[q, j] = sn
                o = o + sn * r_row[:, js]
                p = p + sn * kk_row[:, js]
            out.append(_head_fold(p) if q == 0 else p)
            emit(q, jnp.maximum(t - 1, 0), carry[2 + q])
            raw.append(o)
        return tuple(out + raw)

    def emit(q, t, o):
        o_ref, off = seqs[q][3], seqs[q][5]
        o = _head_fold(o)
        res = o[0:RW_TILES]
        for g in range(1, RW_FOLD):
            res = jnp.where(lane_group == g, o[g * RW_TILES:(g + 1) * RW_TILES], res)
        o_ref[off + t] = res

    init = tuple(kk_dot(q, s[1], s[5]) for q, s in enumerate(seqs))
    zero = jnp.zeros((RW_N, LANE), F32)
    last = lax.fori_loop(0, steps, step, (_head_fold(init[0]), init[1], zero, zero))
    for q in range(2):
        emit(q, steps - 1, last[2 + q])


def wkv(proj4, stack4, v, s0, *, n_pairs, n_chunks, tb, steps, offs, row_a, row_b, M, into=None):
    v3 = v.reshape(M, RW_TILES, LANE)
    lane = np.arange(LANE)
    rep = jnp.asarray(np.stack([(lane[:, None] // RW_H == g) & (lane[:, None] % RW_H == lane[None, :] % RW_H)
                                for g in range(RW_FOLD)]), BF16)
    rspec = lambda f: pl.BlockSpec((None, tb, D), lambda g, c: (0, f(g, c), 0))
    sspec = lambda f: pl.BlockSpec((4, tb, D), lambda g, c: (0, f(g, c), 0))
    vspec = lambda f: pl.BlockSpec((tb, RW_TILES, LANE), lambda g, c: (f(g, c), 0, 0))
    stspec = pl.BlockSpec((2, RW_TILES, RW_N, LANE), lambda g, c: (g, 0, 0, 0))
    in_specs = [rspec(row_a), rspec(row_b), sspec(row_a), sspec(row_b),
                vspec(row_a), vspec(row_b), stspec,
                pl.BlockSpec((RW_FOLD, LANE, LANE), lambda g, c: (0, 0, 0))]
    args = [proj4, proj4, stack4, stack4, v3, v3, s0, rep]
    aliases = {}
    if into is not None:
        aliases = {len(args): 0, len(args) + 1: 1}
        in_specs += [pl.BlockSpec(memory_space=pl.ANY)] * 2
        args += list(into)
    o_a, o_b, st = pl.pallas_call(
        functools.partial(_wkv_kernel, steps=steps, offs=offs),
        out_shape=(jax.ShapeDtypeStruct((M, RW_TILES, LANE), F32),
                   jax.ShapeDtypeStruct((M, RW_TILES, LANE), F32),
                   jax.ShapeDtypeStruct(s0.shape, F32)),
        grid=(n_pairs, n_chunks),
        in_specs=in_specs,
        out_specs=(vspec(row_a), vspec(row_b), stspec),
        scratch_shapes=[pltpu.VMEM((steps, RW_N, LANE), F32),
                        pltpu.VMEM((steps, RW_N, LANE), F32)],
        input_output_aliases=aliases,
        compiler_params=_cparams(("arbitrary", "arbitrary")),
        name="wkv",
    )(*args)
    return o_a, o_b, st


def _rw_post_kernel(oa_ref, ob_ref, v_ref, rk_ref, z_ref, ln_ref, y_ref, o_scr, *, n_seq_blocks):
    i = pl.program_id(0)
    rows = y_ref.shape[0]
    in_a = i < n_seq_blocks
    in_b = jnp.logical_and(i >= n_seq_blocks, i < 2 * n_seq_blocks)
    s = jnp.zeros((rows, LANE), F32)
    for j in range(RW_TILES):
        oa = oa_ref[:, j, :]
        ob = ob_ref[:, j, :]
        oj = jnp.where(in_a, oa, jnp.where(in_b, ob, oa + ob))
        o_scr[:, j * LANE:(j + 1) * LANE] = oj
        s = s + oj
    mean = _head_fold(s) * (1.0 / RW_N)
    s2 = jnp.zeros((rows, LANE), F32)
    for j in range(RW_TILES):
        d = o_scr[:, j * LANE:(j + 1) * LANE] - mean
        s2 = s2 + d * d
    rstd = lax.rsqrt(_head_fold(s2) * (1.0 / RW_N) + RW_GN_EPS)
    rk = rk_ref[...]
    for j in range(RW_TILES):
        js = slice(j * LANE, (j + 1) * LANE)
        o = (o_scr[:, js] - mean) * rstd * ln_ref[0:1, js] + ln_ref[1:2, js]
        o = o + rk * v_ref[:, js]
        y_ref[:, js] = (o * _silu(z_ref[:, js])).astype(BF16)


def rw_post(lay, o_a, o_b, v, rk, proj4, ln):
    te = lay.te
    row = pl.BlockSpec((te, D), lambda i: (i, 0))
    nsb = lay.blocks_per_seq
    a_idx = lambda i: jnp.where(jnp.logical_and(i >= nsb, i < 2 * nsb), i - nsb, i)
    b_idx = lambda i: jnp.where(i < nsb, i + nsb, i)
    return pl.pallas_call(
        functools.partial(_rw_post_kernel, n_seq_blocks=nsb),
        out_shape=jax.ShapeDtypeStruct((lay.M, D), BF16),
        grid=(lay.nblocks,),
        in_specs=[pl.BlockSpec((te, RW_TILES, LANE), lambda i: (a_idx(i), 0, 0)),
                  pl.BlockSpec((te, RW_TILES, LANE), lambda i: (b_idx(i), 0, 0)),
                  row,
                  pl.BlockSpec((te, LANE), lambda i: (i, 0)),
                  pl.BlockSpec((None, te, D), lambda i: (2, i, 0)),
                  pl.BlockSpec((2, D), lambda i: (0, 0))],
        out_specs=row,
        scratch_shapes=[pltpu.VMEM((te, D), F32)],
        compiler_params=_cparams(("parallel",)),
        name="rw_post",
    )(o_a, o_b, v, rk, proj4, ln)


def _head_minor(a):
    lead = a.shape[:-1]
    return a.reshape(lead + (RW_H, RW_N)).swapaxes(-1, -2).reshape(lead + (D,))


def _state_to_tiles(s):
    n = s.shape[0]
    s = s.reshape(n, RW_H, RW_TILES, RW_FOLD, RW_TILES, RW_FOLD)
    return s.transpose(0, 4, 3, 2, 5, 1).reshape(n, RW_TILES, RW_N, LANE)


def _tiles_to_state(s):
    n = s.shape[0]
    s = s.reshape(n, RW_TILES, RW_FOLD, RW_TILES, RW_FOLD, RW_H)
    return s.transpose(0, 5, 3, 2, 1, 4).reshape(n, RW_H, RW_N, RW_N)


def _pad_cols(w, n):
    return jnp.pad(w, ((0, 0), (0, n - w.shape[1])))


def _pad_rows(w, n):
    return jnp.pad(w, ((0, n - w.shape[0]), (0, 0)))


def rwkv_layer(lay, h, shift_state, wkv_state, v_first, w):
    B, T, DB, TS, Mp, M = lay.B, lay.T, lay.DB, lay.TS, lay.Mp, lay.M
    hp = h[:Mp].reshape(B, T, D)
    hs = h[Mp:].reshape(DB, TS, D)
    h_prev = jnp.concatenate([
        jnp.concatenate([jnp.zeros((B, 1, D), F32), hp[:, :-1]], axis=1).reshape(Mp, D),
        jnp.concatenate([shift_state[:, None, :], hs[:, :-1]], axis=1).reshape(lay.Ms, D)], axis=0)
    x6 = rw_mix(lay, h, h_prev, w["mu"])
    w_in = jnp.stack([_head_minor(w["w_in"][c]).astype(BF16) for c in (0, 1, 3, 2)])
    proj4 = matmul(x6, w_in, name="rw_proj")
    vres = v_first is not None
    l1 = [w["v1"]] if vres else []
    l1 += [w["w1"], w["a1"]]
    l2 = [w["v2"]] if vres else []
    l2 += [w["w2"], w["a2"]]
    l1 = jnp.stack([_pad_cols(a, RW_LORA_PAD) for a in l1]).astype(BF16)
    l2 = jnp.stack([_head_minor(_pad_rows(a, RW_LORA_PAD)) for a in l2]).astype(BF16)
    lo1 = matmul(x6, l1, x_off=3 if vres else 4, act_out="tanh_group",
                 tanh_group=1 if vres else 0, out_dtype=BF16, name="rw_lora1")
    lora = matmul(lo1, l2, name="rw_lora2")
    plist = [w["w0"], w["a0"], w["k_k"], w["k_a"], w["r_k"].reshape(-1)]
    if vres:
        plist.append(w["v0"])
    plist += [jnp.zeros((D,), F32)] * (8 - len(plist))
    params = _head_minor(jnp.stack(plist))
    stack4, v, rk = rw_prep(lay, proj4, lora, params, v_first)
    v_first_out = v_first if vres else v

    tb = _tile(T, 128)
    nc = T // tb
    zero_state = jnp.zeros((B, RW_TILES, RW_N, LANE), F32)
    op_a, op_b, p_st = wkv(proj4, stack4, v, zero_state, n_pairs=1, n_chunks=nc, tb=tb, steps=tb,
                           offs=(0, 0), row_a=lambda g, c: c, row_b=lambda g, c: nc + c, M=M)
    base = Mp // (2 * TS)
    rows_s = lambda g, c: base + g
    o_a, o_b, s_st = wkv(proj4, stack4, v, _state_to_tiles(wkv_state), n_pairs=DB // 2, n_chunks=1,
                         tb=2 * TS, steps=TS, offs=(0, TS), row_a=rows_s, row_b=rows_s, M=M,
                         into=(op_a, op_b))

    ln = _head_minor(jnp.stack([w["ln_w"], w["ln_b"]]))
    y = rw_post(lay, o_a, o_b, v, rk, proj4, ln)
    w_out = w["w_out"].reshape(RW_H, RW_N, D).swapaxes(0, 1).reshape(D, D)
    o_proj = matmul2d(y, w_out.astype(BF16), name="rw_out")
    p_shift = hp[:, -1]
    s_shift = hs[:, -1]
    return o_proj, _tiles_to_state(p_st), p_shift, _tiles_to_state(s_st), s_shift, v_first_out


def _rot_cols(w):
    half = MLA_ROPE // 2
    return jnp.concatenate([-w[..., half:], w[..., :half]], axis=-1)


def _mla_prep_kernel(pq_ref, pkv_ref, pkr_ref, gq_ref, gkv_ref, cs_ref,
                     cq_ref, ckv_ref, kr_ref, krb_ref):
    cq_ref[...] = _rms(pq_ref[...], gq_ref[...]).astype(BF16)
    ckv_ref[...] = _rms(pkv_ref[...], gkv_ref[...])
    t2 = pkr_ref[...]
    kr = t2 * cs_ref[0] + pltpu.roll(t2, MLA_ROPE, 1) * cs_ref[1]
    kr_ref[...] = kr[:, :MLA_ROPE]
    krb_ref[...] = kr.astype(BF16)


def mla_prep(lay, proj, g_q, g_kv, cs):
    te = lay.te
    M = lay.M
    return pl.pallas_call(
        _mla_prep_kernel,
        out_shape=(jax.ShapeDtypeStruct((M, MLA_QL), BF16),
                   jax.ShapeDtypeStruct((M, MLA_KVL), F32),
                   jax.ShapeDtypeStruct((M, MLA_ROPE), F32),
                   jax.ShapeDtypeStruct((M, LANE), BF16)),
        grid=(lay.nblocks,),
        in_specs=[pl.BlockSpec((te, MLA_QL), lambda i: (i, D // MLA_QL)),
                  pl.BlockSpec((te, MLA_KVL), lambda i: (i, D // MLA_KVL + 1)),
                  pl.BlockSpec((te, LANE), lambda i: (i, (D + MLA_QL + MLA_KVL) // LANE)),
                  pl.BlockSpec((1, MLA_QL), lambda i: (0, 0)),
                  pl.BlockSpec((1, MLA_KVL), lambda i: (0, 0)),
                  pl.BlockSpec((2, te, LANE), lambda i: (0, i, 0))],
        out_specs=(pl.BlockSpec((te, MLA_QL), lambda i: (i, 0)),
                   pl.BlockSpec((te, MLA_KVL), lambda i: (i, 0)),
                   pl.BlockSpec((te, MLA_ROPE), lambda i: (i, 0)),
                   pl.BlockSpec((te, LANE), lambda i: (i, 0))),
        compiler_params=_cparams(("parallel",)),
        name="mla_prep",
    )(proj, proj, proj, g_q[None], g_kv[None], cs)


def _mla_q_kernel(q_ref, cs_ref, qn_ref, qr_ref):
    for h in range(MLA_H):
        qn_ref[:, h * LANE:(h + 1) * LANE] = q_ref[:, 2 * h * LANE:(2 * h + 1) * LANE].astype(BF16)
        t2 = q_ref[:, (2 * h + 1) * LANE:(2 * h + 2) * LANE]
        qr = t2 * cs_ref[0] + pltpu.roll(t2, MLA_ROPE, 1) * cs_ref[1]
        qr_ref[:, h * LANE:(h + 1) * LANE] = qr.astype(BF16)


def mla_q(lay, q, cs):
    te = lay.te
    M = lay.M
    return pl.pallas_call(
        _mla_q_kernel,
        out_shape=(jax.ShapeDtypeStruct((M, MLA_H * LANE), BF16),
                   jax.ShapeDtypeStruct((M, MLA_H * LANE), BF16)),
        grid=(lay.nblocks,),
        in_specs=[pl.BlockSpec((te, MLA_H * 2 * LANE), lambda i: (i, 0)),
                  pl.BlockSpec((2, te, LANE), lambda i: (0, i, 0))],
        out_specs=(pl.BlockSpec((te, MLA_H * LANE), lambda i: (i, 0)),
                   pl.BlockSpec((te, MLA_H * LANE), lambda i: (i, 0))),
        compiler_params=_cparams(("parallel",)),
        name="mla_q",
    )(q, cs)


def _flash_kernel(qn_ref, qr_ref, kn_ref, kr_ref, v_ref, o_ref, *, tq):
    qi = pl.program_id(2)
    q = jnp.concatenate([qn_ref[...], qr_ref[...]], axis=1)
    nt = (((1,), (1,)), ((), ()))
    c = MLA_SCALE * math.log2(math.e)

    def update(start, carry, diagonal):
        m, l, acc = carry
        k = jnp.concatenate([kn_ref[pl.ds(start, tq), :], kr_ref[pl.ds(start, tq), :]], axis=1)
        s = lax.dot_general(q, k, nt, preferred_element_type=F32) * c
        if diagonal:
            s = jnp.where(lax.broadcasted_iota(jnp.int32, (tq, tq), 1)
                          <= lax.broadcasted_iota(jnp.int32, (tq, tq), 0), s, -1e30)
        m_new = jnp.maximum(m, jnp.max(s, axis=1, keepdims=True))
        p = jnp.exp2(s - m_new)
        alpha = jnp.exp2(m - m_new)
        l = alpha * l + jnp.sum(p, axis=1, keepdims=True)
        acc = alpha * acc + jnp.dot(p.astype(BF16), v_ref[pl.ds(start, tq), :],
                                    preferred_element_type=F32)
        return m_new, l, acc

    def body(ki, carry):
        return update(pl.multiple_of(ki * tq, tq), carry, False)

    init = (jnp.full((tq, 1), -1e30, F32), jnp.zeros((tq, 1), F32), jnp.zeros((tq, MLA_V), F32))
    carry = lax.fori_loop(0, qi, body, init)
    m, l, acc = update(pl.multiple_of(qi * tq, tq), carry, True)
    o_ref[...] = acc / l


def mla_flash(qn, qr, knv, krb, B, T):
    tq = _tile(T, 512)
    nq = T // tq
    return pl.pallas_call(
        functools.partial(_flash_kernel, tq=tq),
        out_shape=jax.ShapeDtypeStruct((B * T, MLA_H * MLA_V), F32),
        grid=(B, MLA_H, nq),
        in_specs=[pl.BlockSpec((tq, LANE), lambda b, h, i: (b * nq + i, h)),
                  pl.BlockSpec((tq, LANE), lambda b, h, i: (b * nq + i, h)),
                  pl.BlockSpec((T, LANE), lambda b, h, i: (b, h)),
                  pl.BlockSpec((T, LANE), lambda b, h, i: (b, 0)),
                  pl.BlockSpec((T, LANE), lambda b, h, i: (b, MLA_H + h))],
        out_specs=pl.BlockSpec((tq, MLA_V), lambda b, h, i: (b * nq + i, h)),
        compiler_params=_cparams(("parallel", "parallel", "arbitrary")),
        name="mla_flash",
    )(qn, qr, knv, krb, knv)


def _paged_kernel(pt_ref, ql_ref, qr_ref, cn_ref, kn_ref, *rest, pg, ts):
    ck_refs = rest[:pg]
    kr_refs = rest[pg:2 * pg]
    o_ref, m_ref, l_ref, acc_ref = rest[2 * pg:]
    p = pl.program_id(1)
    nt = (((1,), (1,)), ((), ()))

    @pl.when(p == 0)
    def _():
        m_ref[...] = jnp.full(m_ref.shape, -1e30, F32)
        l_ref[...] = jnp.zeros(l_ref.shape, F32)
        acc_ref[...] = jnp.zeros(acc_ref.shape, F32)

    rows = ts * MLA_H
    ql = ql_ref[...]
    qr = qr_ref[...][:, :MLA_ROPE]
    qlp = jnp.concatenate([ql, jnp.zeros((LANE - rows, MLA_KVL), BF16)], axis=0)
    ck = jnp.concatenate([r[...].astype(BF16) for r in ck_refs], axis=0)
    kr_t = jnp.concatenate([r[...].astype(BF16) for r in kr_refs], axis=1)
    s_t = lax.dot_general(ck, qlp, nt, preferred_element_type=F32)
    s = (s_t.T[:rows] + jnp.dot(qr, kr_t, preferred_element_type=F32)) * MLA_SCALE
    m = m_ref[...]
    m_new = jnp.maximum(m, jnp.max(s, axis=1, keepdims=True))
    pr = jnp.exp(s - m_new)
    alpha = jnp.exp(m - m_new)
    l = alpha * l_ref[...] + jnp.sum(pr, axis=1, keepdims=True)
    acc = alpha * acc_ref[...] + jnp.dot(pr.astype(BF16), ck, preferred_element_type=F32)
    m = m_new
    m_ref[...] = m
    l_ref[...] = l
    acc_ref[...] = acc

    @pl.when(p == pl.num_programs(1) - 1)
    def _():
        qlf = ql.astype(F32)
        qrf = qr.astype(F32)
        tok = lax.broadcasted_iota(jnp.int32, (ts * MLA_H, 1), 0) // MLA_H
        cols = []
        for j in range(ts):
            cn = cn_ref[j:j + 1, :].astype(BF16).astype(F32)
            kn = kn_ref[j:j + 1, :MLA_ROPE].astype(F32)
            sj = (jnp.sum(qlf * cn, axis=1, keepdims=True)
                  + jnp.sum(qrf * kn, axis=1, keepdims=True)) * MLA_SCALE
            cols.append(jnp.where(tok >= j, sj, -1e30))
        m2 = m
        for sj in cols:
            m2 = jnp.maximum(m2, sj)
        alpha = jnp.exp(m - m2)
        l2 = alpha * l
        acc2 = alpha * acc
        for j, sj in enumerate(cols):
            pj = jnp.exp(sj - m2)
            l2 = l2 + pj
            acc2 = acc2 + pj.astype(BF16).astype(F32) * cn_ref[j:j + 1, :].astype(BF16).astype(F32)
        o_ref[...] = (acc2 / l2).astype(o_ref.dtype)


def mla_paged(page_table, ql, qr, ckv_new, krb_new, cache_ckv, cache_kr, layer, DB, TS):
    n_pages = page_table.shape[1]
    pg = _tile(n_pages, 16) if n_pages >= 8 else n_pages
    rows = TS * MLA_H
    pt = page_table.reshape(-1)

    def page_spec(i, shape):
        return pl.BlockSpec((None, None) + shape,
                            lambda b, p, pt_ref: (layer, pt_ref[b * n_pages + p * pg + i], 0, 0))

    cache_kr_t = jnp.swapaxes(cache_kr, 2, 3)
    per_b = lambda width: pl.BlockSpec((None, rows, width), lambda b, p, pt_ref: (b, 0, 0))
    new_b = lambda width: pl.BlockSpec((None, TS, width), lambda b, p, pt_ref: (b, 0, 0))
    grid_spec = pltpu.PrefetchScalarGridSpec(
        num_scalar_prefetch=1,
        grid=(DB, n_pages // pg),
        in_specs=[per_b(MLA_KVL), per_b(LANE), new_b(MLA_KVL), new_b(LANE)]
                 + [page_spec(i, (PAGE, MLA_KVL)) for i in range(pg)]
                 + [page_spec(i, (MLA_ROPE, PAGE)) for i in range(pg)],
        out_specs=per_b(MLA_KVL),
        scratch_shapes=[pltpu.VMEM((rows, 1), F32), pltpu.VMEM((rows, 1), F32),
                        pltpu.VMEM((rows, MLA_KVL), F32)],
    )
    return pl.pallas_call(
        functools.partial(_paged_kernel, pg=pg, ts=TS),
        out_shape=jax.ShapeDtypeStruct((DB, rows, MLA_KVL), BF16),
        grid_spec=grid_spec,
        compiler_params=_cparams(("parallel", "arbitrary")),
        name="mla_paged",
    )(pt, ql, qr, ckv_new, krb_new, *([cache_ckv] * pg), *([cache_kr_t] * pg))


def _gate_kernel(o_ref, z_ref, y_ref):
    y_ref[...] = (o_ref[...] * _silu(z_ref[...])).astype(BF16)


def gate_mul(lay, o, zsrc, zcol):
    te = lay.te
    return pl.pallas_call(
        _gate_kernel,
        out_shape=jax.ShapeDtypeStruct((lay.M, D), BF16),
        grid=(lay.nblocks,),
        in_specs=[pl.BlockSpec((te, D), lambda i: (i, 0)),
                  pl.BlockSpec((te, D), lambda i: (i, zcol))],
        out_specs=pl.BlockSpec((te, D), lambda i: (i, 0)),
        compiler_params=_cparams(("parallel",)),
        name="gate_mul",
    )(o, zsrc)


def mla_layer(lay, h, cache_ckv, cache_kr, layer, page_table, w):
    B, T, DB, TS, Mp, M = lay.B, lay.T, lay.DB, lay.TS, lay.Mp, lay.M
    w_in = w["w_in"]
    w_kr = w_in[:, 2 * MLA_QL:2 * MLA_QL + MLA_ROPE]
    w1 = jnp.concatenate([w_in[:, 2 * MLA_QL + MLA_ROPE:], w_in[:, :2 * MLA_QL],
                          w_kr, _rot_cols(w_kr)], axis=1)
    proj = matmul2d(h, w1.astype(BF16), tn=640, name="mla_proj")
    pos = jnp.concatenate([jnp.tile(jnp.arange(T, dtype=F32), B),
                           jnp.tile(page_table.shape[1] * PAGE + jnp.arange(TS, dtype=F32), DB)])
    half = MLA_ROPE // 2
    inv_freq = ROPE_THETA ** (-jnp.arange(half, dtype=F32) / half)
    ang = pos[:, None] * inv_freq[None, :]
    zeros = jnp.zeros((M, MLA_ROPE), F32)
    cs = jnp.stack([jnp.concatenate([jnp.cos(ang), jnp.cos(ang), zeros], axis=1),
                    jnp.concatenate([jnp.sin(ang), jnp.sin(ang), zeros], axis=1)])
    cq, ckv, kr, krb = mla_prep(lay, proj, w["q_norm"], w["kv_norm"], cs)
    w_uq = w["w_uq"]
    wq = jnp.concatenate([w_uq, _rot_cols(w_uq[..., MLA_NOPE:])], axis=-1)
    q = matmul2d(cq, wq.reshape(MLA_QL, MLA_H * 2 * LANE).astype(BF16), name="mla_qproj")
    qn, qr = mla_q(lay, q, cs)
    w_kv = jnp.concatenate([w["w_uk"].reshape(MLA_KVL, -1), w["w_uv"].reshape(MLA_KVL, -1)], axis=1)
    knv = matmul2d(ckv[:Mp], w_kv.astype(BF16), out_dtype=BF16, name="mla_kv")
    o_p = mla_flash(qn, qr, knv, krb, B, T)
    w_ukT = w["w_uk"].transpose(1, 2, 0).astype(BF16)
    ql = head_matmul(qn[Mp:], w_ukT, BF16, "mla_qlat")
    o_lat = mla_paged(page_table, ql.reshape(DB, TS * MLA_H, MLA_KVL),
                      qr[Mp:].reshape(DB, TS * MLA_H, LANE),
                      ckv[Mp:].reshape(DB, TS, MLA_KVL), krb[Mp:].reshape(DB, TS, LANE),
                      cache_ckv, cache_kr, layer, DB, TS)
    o_s = head_matmul(o_lat.reshape(lay.Ms, MLA_H * MLA_KVL),
                      w["w_uv"].transpose(1, 0, 2).astype(BF16), F32, "mla_ouv")
    o = jnp.concatenate([o_p, o_s], axis=0)
    y = gate_mul(lay, o, proj, 0)
    o_proj = matmul2d(y, w["w_out"].astype(BF16), name="mla_out")
    return (o_proj, ckv[:Mp].reshape(B, T, MLA_KVL), kr[:Mp].reshape(B, T, MLA_ROPE),
            ckv[Mp:].reshape(DB, TS, MLA_KVL), kr[Mp:].reshape(DB, TS, MLA_ROPE))


def _gla_kernel(q_ref, k_ref, v_ref, g_ref, s0_ref, o_ref, sT_ref, st_ref, *, L):
    c = pl.program_id(2)
    C = GLA_SUB
    assert L % C == 0

    @pl.when(c == 0)
    def _():
        st_ref[...] = s0_ref[...].T

    q = q_ref[...] * (GLA_DK ** -0.5)
    k = k_ref[...]
    v = v_ref[...].astype(BF16)
    trow = lax.broadcasted_iota(jnp.int32, (L, 1), 0)
    b = g_ref[...]
    sh = 1
    while sh < L:
        b = b + jnp.where(trow >= sh, pltpu.roll(b, sh, 0), 0.0)
        sh *= 2
    st = st_ref[...]
    nt = (((1,), (1,)), ((), ()))
    tn = (((0,), (0,)), ((), ()))
    o = lax.dot_general((q * jnp.exp(b)).astype(BF16), st.astype(BF16), nt, preferred_element_type=F32)
    crow = lax.broadcasted_iota(jnp.int32, (C, 1), 0)
    col = lax.broadcasted_iota(jnp.int32, (C, L), 1)
    att_rows = []
    for i in range(L // C):
        lo = i * C
        qi, ki, bi = q[lo:lo + C], k[lo:lo + C], b[lo:lo + C]
        if i == 0:
            att_i = jnp.zeros((C, L), F32)
        else:
            beta = b[lo - 1:lo, :]
            q_in = (qi * jnp.exp(bi - beta)).astype(BF16)
            k_out = (k * jnp.exp(jnp.where(trow < lo, beta - b, -jnp.inf))).astype(BF16)
            att_i = lax.dot_general(q_in, k_out, nt, preferred_element_type=F32)
        for s in range(C):
            e = jnp.where(crow >= s, bi - bi[s:s + 1, :], -jnp.inf)
            a_s = jnp.sum(qi * ki[s:s + 1, :] * jnp.exp(e), axis=1, keepdims=True)
            att_i = jnp.where(col == lo + s, a_s, att_i)
        att_rows.append(att_i)
    att = att_rows[0] if len(att_rows) == 1 else jnp.concatenate(att_rows, axis=0)
    o = o + jnp.dot(att.astype(BF16), v, preferred_element_type=F32)
    o_ref[...] = o
    b_end = b[L - 1:L, :]
    kd = (k * jnp.exp(b_end - b)).astype(BF16)
    st = st * jnp.exp(b_end) + lax.dot_general(v, kd, tn, preferred_element_type=F32)
    st_ref[...] = st

    @pl.when(c == pl.num_programs(2) - 1)
    def _():
        sT_ref[...] = st.T


def gla_scan(q, k, v, g, s0, *, nseq, nchunk, L, row0, qcol, kcol, vcol):
    rows = lambda n, h, c: row0 + n * nchunk + c
    return pl.pallas_call(
        functools.partial(_gla_kernel, L=L),
        out_shape=(jax.ShapeDtypeStruct((q.shape[0], GLA_H * GLA_DV), F32),
                   jax.ShapeDtypeStruct((nseq, GLA_H, GLA_DK, GLA_DV), F32)),
        grid=(nseq, GLA_H, nchunk),
        in_specs=[pl.BlockSpec((L, GLA_DK), lambda n, h, c: (rows(n, h, c), qcol + h)),
                  pl.BlockSpec((L, GLA_DK), lambda n, h, c: (rows(n, h, c), kcol + h)),
                  pl.BlockSpec((L, GLA_DV), lambda n, h, c: (rows(n, h, c), vcol + h)),
                  pl.BlockSpec((L, GLA_DK), lambda n, h, c: (rows(n, h, c), h)),
                  pl.BlockSpec((None, None, GLA_DK, GLA_DV), lambda n, h, c: (n, h, 0, 0))],
        out_specs=(pl.BlockSpec((L, GLA_DV), lambda n, h, c: (rows(n, h, c), h)),
                   pl.BlockSpec((None, None, GLA_DK, GLA_DV), lambda n, h, c: (n, h, 0, 0))),
        scratch_shapes=[pltpu.VMEM((GLA_DV, GLA_DK), F32)],
        compiler_params=_cparams(("parallel", "parallel", "arbitrary")),
        name="gla_scan",
    )(q, k, v, g, s0)


def _gla_post_kernel(o_ref, z_ref, g_ref, y_ref):
    for h in range(GLA_H):
        hs = slice(h * GLA_DV, (h + 1) * GLA_DV)
        y_ref[:, hs] = (_rms(o_ref[:, hs], g_ref[...]) * _silu(z_ref[:, hs])).astype(BF16)


def gla_post(lay, o, proj, g_norm):
    te = lay.te
    return pl.pallas_call(
        _gla_post_kernel,
        out_shape=jax.ShapeDtypeStruct((lay.M, D), BF16),
        grid=(lay.nblocks,),
        in_specs=[pl.BlockSpec((te, D), lambda i: (i, 0)),
                  pl.BlockSpec((te, D), lambda i: (i, 2)),
                  pl.BlockSpec((1, GLA_DV), lambda i: (0, 0))],
        out_specs=pl.BlockSpec((te, D), lambda i: (i, 0)),
        compiler_params=_cparams(("parallel",)),
        name="gla_post",
    )(o, proj, g_norm[None])


def gla_layer(lay, h, state, w):
    B, T, DB, TS, Mp, M = lay.B, lay.T, lay.DB, lay.TS, lay.Mp, lay.M
    w_in = _pad_cols(w["w_in"], 2 * GLA_QK + 2 * D + GLA_LORA_PAD)
    proj = matmul2d(h, w_in.astype(BF16), tn=640, name="gla_proj")
    gl = proj[:, 2 * GLA_QK + 2 * D:]
    g = matmul2d(gl, _pad_rows(w["w_g2"], GLA_LORA_PAD).astype(BF16), w["b_g"],
                 act_out="logsig_tau", name="gla_gate")
    L = math.gcd(T, GLA_CHUNK)
    nk = GLA_QK // GLA_DK
    o_p, p_state = gla_scan(proj, proj, proj, g, jnp.zeros((B, GLA_H, GLA_DK, GLA_DV), F32),
                            nseq=B, nchunk=T // L, L=L, row0=0,
                            qcol=0, kcol=nk, vcol=2 * GLA_QK // GLA_DV)
    LS = GLA_SUB
    pad = lambda a: jnp.pad(a.reshape(DB, TS, -1), ((0, 0), (0, LS - TS), (0, 0))).reshape(DB * LS, -1)
    ps = pad(proj[Mp:, :2 * GLA_QK + D])
    o_s, s_state = gla_scan(ps, ps, ps, pad(g[Mp:]), state, nseq=DB, nchunk=1, L=LS, row0=0,
                            qcol=0, kcol=nk, vcol=2 * GLA_QK // GLA_DV)
    o_s = o_s.reshape(DB, LS, D)[:, :TS].reshape(lay.Ms, D)
    o = jnp.concatenate([o_p[:Mp], o_s], axis=0)
    y = gla_post(lay, o, proj, w["norm"])
    o_proj = matmul2d(y, w["w_out"].astype(BF16), name="gla_out")
    return o_proj, p_state, s_state


def kernel(x_prompt, x_sample, c_prompt, c_sample, state_rwkv_wkv, state_rwkv_shift, cache_mla_ckv, cache_mla_krope, page_table, state_gla, norm_pre, norm_post, ada_w, ada_b, rw_mu, rw_w_in, rw_w0, rw_w1, rw_w2, rw_a0, rw_a1, rw_a2, rw_v0, rw_v1, rw_v2, rw_k_k, rw_k_a, rw_r_k, rw_ln_w, rw_ln_b, rw_w_out, mla_w_in, mla_q_norm, mla_kv_norm, mla_w_uq, mla_w_uk, mla_w_uv, mla_w_out, gla_w_in, gla_w_g2, gla_b_g, gla_norm, gla_w_out):
    B, T, _ = x_prompt.shape
    DB, TS, _ = x_sample.shape
    depth = norm_pre.shape[0]
    lay = RowLayout(B, T, DB, TS)
    Mp = lay.Mp
    x = jnp.concatenate([x_prompt.reshape(Mp, D), x_sample.reshape(lay.Ms, D)], axis=0)
    c = jnp.concatenate([c_prompt, c_sample], axis=0)[None]
    c = jnp.broadcast_to(c, (depth,) + c.shape[1:])
    mods = matmul(c, ada_w, ada_b[:, None, :], act_in="silu", name="ada")

    p_wkv, p_shift, p_ckv, p_kr, p_gla = [], [], [], [], []
    s_wkv, s_shift, s_ckv, s_kr, s_gla = [], [], [], [], []
    v_first = None
    for i in range(depth):
        kind, j = i % 3, i // 3
        mod = lay.expand_mod(mods[i])
        h = prenorm(lay, x, norm_pre[i], mod, F32 if kind == 0 else BF16)
        if kind == 0:
            w = dict(mu=rw_mu[j], w_in=rw_w_in[j], w0=rw_w0[j], w1=rw_w1[j], w2=rw_w2[j],
                     a0=rw_a0[j], a1=rw_a1[j], a2=rw_a2[j], k_k=rw_k_k[j], k_a=rw_k_a[j],
                     r_k=rw_r_k[j], ln_w=rw_ln_w[j], ln_b=rw_ln_b[j], w_out=rw_w_out[j])
            if j > 0:
                w.update(v0=rw_v0[j - 1], v1=rw_v1[j - 1], v2=rw_v2[j - 1])
            o, pst, psh, sst, ssh, v_first = rwkv_layer(
                lay, h, state_rwkv_shift[j], state_rwkv_wkv[j], v_first if j > 0 else None, w)
            p_wkv.append(pst)
            p_shift.append(psh)
            s_wkv.append(sst)
            s_shift.append(ssh)
        elif kind == 1:
            w = dict(w_in=mla_w_in[j], q_norm=mla_q_norm[j], kv_norm=mla_kv_norm[j],
                     w_uq=mla_w_uq[j], w_uk=mla_w_uk[j], w_uv=mla_w_uv[j], w_out=mla_w_out[j])
            o, pc, pk, sc, sk = mla_layer(lay, h, cache_mla_ckv, cache_mla_krope, j, page_table, w)
            p_ckv.append(pc)
            p_kr.append(pk)
            s_ckv.append(sc)
            s_kr.append(sk)
        else:
            w = dict(w_in=gla_w_in[j], w_g2=gla_w_g2[j], b_g=gla_b_g[j], norm=gla_norm[j],
                     w_out=gla_w_out[j])
            o, pg, sg = gla_layer(lay, h, state_gla[j], w)
            p_gla.append(pg)
            s_gla.append(sg)
        x = residual(lay, x, o, norm_post[i], mod)
    yp = x[:Mp].reshape(B, T, D)
    ys = x[Mp:].reshape(DB, TS, D)
    return (yp, ys,
            jnp.stack(p_wkv), jnp.stack(p_shift), jnp.stack(p_ckv), jnp.stack(p_kr), jnp.stack(p_gla),
            jnp.stack(s_wkv), jnp.stack(s_shift), jnp.stack(s_ckv), jnp.stack(s_kr), jnp.stack(s_gla))
```

```python
import functools
import math

import jax
import jax.numpy as jnp
import numpy as np
from jax import lax
from jax.experimental import pallas as pl
from jax.experimental.pallas import tpu as pltpu

F32 = jnp.float32
BF16 = jnp.bfloat16

D = 2048
NORM_EPS = 1e-6
RW_N = 64
RW_H = D // RW_N
RW_LORA_PAD = 128
RW_GN_EPS = 64e-5
RW_TILES = D // 128
RW_FOLD = 128 // RW_H
EXP_M05 = math.exp(-0.5)
MLA_H = 16
MLA_NOPE = 128
MLA_ROPE = 64
MLA_V = 128
MLA_QL = 512
MLA_KVL = 512
MLA_SCALE = (MLA_NOPE + MLA_ROPE) ** -0.5
ROPE_THETA = 10000.0
PAGE = 128
GLA_H = 4
GLA_DK = 256
GLA_DV = 512
GLA_QK = GLA_H * GLA_DK
GLA_LORA_PAD = 256
GLA_TAU = 16.0
GLA_CHUNK = 64
GLA_SUB = 16

LANE = 128
ROW_TILE = 128
VMEM_LIMIT = 48 * 1024 * 1024


def _cparams(sem):
    return pltpu.CompilerParams(dimension_semantics=sem, vmem_limit_bytes=VMEM_LIMIT)


def _tile(n, pref):
    for t in (1024, 640, 512, 384, 256, 128, 64, 32, 16, 8):
        if t <= pref and n % t == 0:
            return t
    return n


def _sigmoid(x):
    return 1.0 / (1.0 + jnp.exp(-x))


def _silu(x):
    return x * _sigmoid(x)


def _mm_kernel(*refs, has_bias, act_in, act_out, tanh_group):
    if has_bias:
        x_ref, w_ref, b_ref, o_ref = refs
    else:
        x_ref, w_ref, o_ref = refs
    x = x_ref[...]
    if act_in == "silu":
        x = _silu(x.astype(F32))
    acc = jnp.dot(x.astype(BF16), w_ref[...].astype(BF16), preferred_element_type=F32)
    if has_bias:
        acc = acc + b_ref[...]
    if act_out == "tanh_group":
        acc = jnp.where(pl.program_id(0) == tanh_group, jnp.tanh(acc), acc)
    elif act_out == "logsig_tau":
        acc = (jnp.minimum(acc, 0.0) - jnp.log(1.0 + jnp.exp(-jnp.abs(acc)))) * (1.0 / GLA_TAU)
    o_ref[...] = acc.astype(o_ref.dtype)


MM_X_TILE_BYTES = 9 * 1024 * 1024
MM_MAX_ROWS = 2304


def _row_tile(M, row_bytes):
    for parts in range(1, M // 8 + 1):
        rows = M // parts
        if M % parts == 0 and rows % 8 == 0 and rows <= MM_MAX_ROWS and rows * row_bytes <= MM_X_TILE_BYTES:
            return rows
    return M


def matmul(x, w, bias=None, *, x_off=0, act_in=None, act_out=None, tanh_group=0,
           out_dtype=F32, tn=512, name="mm"):
    G, K, N = w.shape
    M = x.shape[1]
    tm = _row_tile(M, K * x.dtype.itemsize)
    tn = _tile(N, tn)
    in_specs = [
        pl.BlockSpec((None, tm, K), lambda g, i, j: (g + x_off, i, 0)),
        pl.BlockSpec((None, K, tn), lambda g, i, j: (g, 0, j)),
    ]
    args = [x, w]
    if bias is not None:
        in_specs.append(pl.BlockSpec((None, 1, tn), lambda g, i, j: (g, 0, j)))
        args.append(bias)
    return pl.pallas_call(
        functools.partial(_mm_kernel, has_bias=bias is not None, act_in=act_in,
                          act_out=act_out, tanh_group=tanh_group),
        out_shape=jax.ShapeDtypeStruct((G, M, N), out_dtype),
        grid=(G, M // tm, N // tn),
        in_specs=in_specs,
        out_specs=pl.BlockSpec((None, tm, tn), lambda g, i, j: (g, i, j)),
        compiler_params=_cparams(("parallel", "parallel", "arbitrary")),
        name=name,
    )(*args)


def matmul2d(x, w, bias=None, **kw):
    b3 = None if bias is None else bias[None, None, :]
    return matmul(x[None], w[None], b3, **kw)[0]


def _hmm_kernel(x_ref, w_ref, o_ref):
    o_ref[...] = jnp.dot(x_ref[...].astype(BF16), w_ref[...],
                         preferred_element_type=F32).astype(o_ref.dtype)


def head_matmul(x, w, out_dtype, name):
    H, Kh, Nh = w.shape
    M = x.shape[0]
    tm = _tile(M, 512)
    return pl.pallas_call(
        _hmm_kernel,
        out_shape=jax.ShapeDtypeStruct((M, H * Nh), out_dtype),
        grid=(H, M // tm),
        in_specs=[pl.BlockSpec((tm, Kh), lambda h, i: (i, h)),
                  pl.BlockSpec((None, Kh, Nh), lambda h, i: (h, 0, 0))],
        out_specs=pl.BlockSpec((tm, Nh), lambda h, i: (i, h)),
        compiler_params=_cparams(("parallel", "arbitrary")),
        name=name,
    )(x, w)


class RowLayout:
    def __init__(self, B, T, DB, TS):
        self.B, self.T, self.DB, self.TS = B, T, DB, TS
        self.Mp = B * T
        self.Ms = DB * TS
        self.M = self.Mp + self.Ms
        self.te = math.gcd(math.gcd(ROW_TILE, T), self.Ms)
        self.n_p_blocks = self.Mp // self.te
        self.blocks_per_seq = T // self.te
        self.nblocks = self.M // self.te

    def mod_index(self, i):
        return jnp.where(i < self.n_p_blocks, i // self.blocks_per_seq,
                         self.B + i - self.n_p_blocks)

    def expand_mod(self, m):
        mp = jnp.broadcast_to(m[:self.B, None, :], (self.B, self.te, m.shape[-1]))
        ms = jnp.repeat(m[self.B:], self.TS, axis=0).reshape(self.Ms // self.te, self.te, -1)
        return jnp.concatenate([mp, ms], axis=0)


def _rms(x, g):
    ms = jnp.mean(x * x, axis=-1, keepdims=True)
    return x * lax.rsqrt(ms + NORM_EPS) * g


def _prenorm_kernel(x_ref, g_ref, shift_ref, scale_ref, h_ref):
    h = _rms(x_ref[...], g_ref[...]) * (1.0 + scale_ref[...]) + shift_ref[...]
    h_ref[...] = h.astype(h_ref.dtype)


def prenorm(lay, x, g_pre, mod, out_dtype):
    te = lay.te
    row = pl.BlockSpec((te, D), lambda i: (i, 0))
    return pl.pallas_call(
        _prenorm_kernel,
        out_shape=jax.ShapeDtypeStruct((lay.M, D), out_dtype),
        grid=(lay.nblocks,),
        in_specs=[row,
                  pl.BlockSpec((1, D), lambda i: (0, 0)),
                  pl.BlockSpec((None, te, D), lambda i: (lay.mod_index(i), 0, 0)),
                  pl.BlockSpec((None, te, D), lambda i: (lay.mod_index(i), 0, 1))],
        out_specs=row,
        compiler_params=_cparams(("parallel",)),
        name="prenorm",
    )(x, g_pre[None], mod, mod)


def _residual_kernel(x_ref, o_ref, g_ref, gate_ref, y_ref):
    y_ref[...] = x_ref[...] + gate_ref[...] * _rms(o_ref[...], g_ref[...])


def residual(lay, x, o, g_post, mod):
    te = lay.te
    row = pl.BlockSpec((te, D), lambda i: (i, 0))
    return pl.pallas_call(
        _residual_kernel,
        out_shape=jax.ShapeDtypeStruct((lay.M, D), F32),
        grid=(lay.nblocks,),
        in_specs=[row, row,
                  pl.BlockSpec((1, D), lambda i: (0, 0)),
                  pl.BlockSpec((None, te, D), lambda i: (lay.mod_index(i), 0, 2))],
        out_specs=row,
        compiler_params=_cparams(("parallel",)),
        name="residual",
    )(x, o, g_post[None], mod)


RW_MIX_ORDER = (0, 1, 3, 2, 4, 5)


SUBLANES = 8


def _rw_mix_kernel(h_ref, tail_ref, first_ref, mu_ref, x_ref, *, blocks_per_seq, n_p_blocks, ts):
    i = pl.program_id(0)
    h = h_ref[...]
    rows = h.shape[0]
    ridx = lax.broadcasted_iota(jnp.int32, (rows, 1), 0)
    prev = pltpu.roll(h, 1, 0)
    tail = tail_ref[SUBLANES - 1:SUBLANES, :]
    opens = jnp.logical_and(i < n_p_blocks, i % blocks_per_seq == 0)
    prev = jnp.where(ridx == 0, jnp.where(opens, 0.0, tail), prev)
    is_sample = i >= n_p_blocks
    prev = jnp.where(jnp.logical_and(is_sample, ridx % ts == 0), first_ref[...], prev)
    dx = prev - h
    for o, c in enumerate(RW_MIX_ORDER):
        x_ref[o] = (h + dx * mu_ref[c:c + 1, :]).astype(BF16)


def rw_mix(lay, h, shift_rows, mu):
    te = lay.te
    per = te // SUBLANES
    return pl.pallas_call(
        functools.partial(_rw_mix_kernel, blocks_per_seq=lay.blocks_per_seq,
                          n_p_blocks=lay.n_p_blocks, ts=lay.TS),
        out_shape=jax.ShapeDtypeStruct((6, lay.M, D), BF16),
        grid=(lay.nblocks,),
        in_specs=[pl.BlockSpec((te, D), lambda i: (i, 0)),
                  pl.BlockSpec((SUBLANES, D), lambda i: (jnp.maximum(i * per - 1, 0), 0)),
                  pl.BlockSpec((te, D), lambda i: (jnp.maximum(i - lay.n_p_blocks, 0), 0)),
                  pl.BlockSpec((6, D), lambda i: (0, 0))],
        out_specs=pl.BlockSpec((6, te, D), lambda i: (0, i, 0)),
        compiler_params=_cparams(("parallel",)),
        name="rw_mix",
    )(h, h, shift_rows, mu)


def _head_fold(s):
    s = s + pltpu.roll(s, RW_H, 1)
    return s + pltpu.roll(s, 2 * RW_H, 1)


def _rw_prep_kernel(*refs, vres):
    if vres:
        p_ref, l_ref, par_ref, vf_ref, out_ref, v_ref, rk_ref = refs
        iw, ia = 1, 2
    else:
        p_ref, l_ref, par_ref, out_ref, v_ref, rk_ref = refs
        iw, ia = 0, 1
    rows = out_ref.shape[1]
    n2 = jnp.zeros((rows, LANE), F32)
    rk = jnp.zeros((rows, LANE), F32)
    for j in range(RW_TILES):
        js = slice(j * LANE, (j + 1) * LANE)
        par = lambda i: par_ref[i:i + 1, js]
        r = p_ref[0, :, js]
        k = p_ref[1, :, js]
        v = p_ref[3, :, js]
        decay = jnp.exp(-EXP_M05 * _sigmoid(par(0) + l_ref[iw, :, js]))
        a = _sigmoid(par(1) + l_ref[ia, :, js])
        kk = k * par(2)
        n2 = n2 + kk * kk
        kmod = k * (1.0 + (a - 1.0) * par(3))
        rk = rk + r * kmod * par(4)
        if vres:
            v = v + (vf_ref[:, js] - v) * _sigmoid(par(5) + l_ref[0, :, js])
        out_ref[0, :, js] = decay
        out_ref[1, :, js] = kmod
        out_ref[2, :, js] = kk
        out_ref[3, :, js] = a
        v_ref[:, js] = v
    inv = 1.0 / jnp.maximum(jnp.sqrt(_head_fold(n2)), 1e-12)
    rk_ref[...] = _head_fold(rk)
    for j in range(RW_TILES):
        js = slice(j * LANE, (j + 1) * LANE)
        kkn = out_ref[2, :, js] * inv
        out_ref[2, :, js] = kkn
        out_ref[3, :, js] = kkn * out_ref[3, :, js]


def rw_prep(lay, proj, lora, params, v_first):
    te = lay.te
    vres = v_first is not None
    nl = lora.shape[0]
    in_specs = [pl.BlockSpec((4, te, D), lambda i: (0, i, 0)),
                pl.BlockSpec((nl, te, D), lambda i: (0, i, 0)),
                pl.BlockSpec(params.shape, lambda i: (0, 0))]
    args = [proj, lora, params]
    if vres:
        in_specs.append(pl.BlockSpec((te, D), lambda i: (i, 0)))
        args.append(v_first)
    return pl.pallas_call(
        functools.partial(_rw_prep_kernel, vres=vres),
        out_shape=(jax.ShapeDtypeStruct((4, lay.M, D), F32),
                   jax.ShapeDtypeStruct((lay.M, D), F32),
                   jax.ShapeDtypeStruct((lay.M, LANE), F32)),
        grid=(lay.nblocks,),
        in_specs=in_specs,
        out_specs=(pl.BlockSpec((4, te, D), lambda i: (0, i, 0)),
                   pl.BlockSpec((te, D), lambda i: (i, 0)),
                   pl.BlockSpec((te, LANE), lambda i: (i, 0))),
        compiler_params=_cparams(("parallel",)),
        name="rw_prep",
    )(*args)


def _wkv_kernel(r_a, r_b, s_a, s_b, v_a, v_b, rep_ref, o_a, o_b, st_ref, vrep_a, vrep_b, *, steps):
    @pl.when(pl.program_id(0) == 0)
    def _():
        st_ref[...] = jnp.zeros(st_ref.shape, F32)

    seqs = ((r_a, s_a, v_a, o_a, vrep_a), (r_b, s_b, v_b, o_b, vrep_b))
    for (_, _, v_ref, _, vrep) in seqs:
        y = v_ref[...].reshape(steps * RW_TILES, LANE)
        hi = y.astype(BF16)
        lo = (y - hi.astype(F32)).astype(BF16)
        for g in range(RW_FOLD):
            yr = (jnp.dot(hi, rep_ref[g], preferred_element_type=F32)
                  + jnp.dot(lo, rep_ref[g], preferred_element_type=F32))
            vrep[:, g * RW_TILES:(g + 1) * RW_TILES, :] = yr.reshape(steps, RW_TILES, LANE)

    lane_group = lax.broadcasted_iota(jnp.int32, (RW_TILES, LANE), 1) // RW_H

    def kk_dot(q, s_ref):
        p = jnp.zeros((RW_N, LANE), F32)
        for j in range(RW_TILES):
            p = p + st_ref[q, j] * s_ref[2, 0:1, j * LANE:(j + 1) * LANE]
        return p

    def step(t, carry):
        t_next = jnp.minimum(t + 1, steps - 1)
        out, raw = [], []
        for q, (r_ref, s_ref, _, o_ref, vrep) in enumerate(seqs):
            sa = -(carry[q] if q == 0 else _head_fold(carry[q]))
            vr = vrep[t]
            w_row = s_ref[0, pl.ds(t, 1), :]
            k_row = s_ref[1, pl.ds(t, 1), :]
            b_row = s_ref[3, pl.ds(t, 1), :]
            r_row = r_ref[pl.ds(t, 1), :]
            kk_row = s_ref[2, pl.ds(t_next, 1), :]
            o = jnp.zeros((RW_N, LANE), F32)
            p = jnp.zeros((RW_N, LANE), F32)
            for j in range(RW_TILES):
                js = slice(j * LANE, (j + 1) * LANE)
                sn = st_ref[q, j] * w_row[:, js] + sa * b_row[:, js] + vr * k_row[:, js]
                st_ref[q, j] = sn
                o = o + sn * r_row[:, js]
                p = p + sn * kk_row[:, js]
            out.append(_head_fold(p) if q == 0 else p)
            emit(q, jnp.maximum(t - 1, 0), carry[2 + q])
            raw.append(o)
        return tuple(out + raw)

    def emit(q, t, o):
        o_ref = seqs[q][3]
        o = _head_fold(o)
        res = o[0:RW_TILES]
        for g in range(1, RW_FOLD):
            res = jnp.where(lane_group == g, o[g * RW_TILES:(g + 1) * RW_TILES], res)
        o_ref[t] = res

    init = tuple(kk_dot(q, s[1]) for q, s in enumerate(seqs))
    zero = jnp.zeros((RW_N, LANE), F32)
    last = lax.fori_loop(0, steps, step, (_head_fold(init[0]), init[1], zero, zero))
    for q in range(2):
        emit(q, steps - 1, last[2 + q])


def wkv_prompt(proj4, stack4, v, T):
    M = v.shape[0]
    tb = _tile(T, 128)
    nc = T // tb
    v3 = v.reshape(M, RW_TILES, LANE)
    lane = np.arange(LANE)
    rep = jnp.asarray(np.stack([(lane[:, None] // RW_H == g) & (lane[:, None] % RW_H == lane[None, :] % RW_H)
                                for g in range(RW_FOLD)]), BF16)
    rspec = lambda s: pl.BlockSpec((None, tb, D), lambda c: (0, s * nc + c, 0))
    sspec = lambda s: pl.BlockSpec((4, tb, D), lambda c: (0, s * nc + c, 0))
    vspec = lambda s: pl.BlockSpec((tb, RW_TILES, LANE), lambda c: (s * nc + c, 0, 0))
    ospec = pl.BlockSpec((tb, RW_TILES, LANE), lambda c: (c, 0, 0))
    return pl.pallas_call(
        functools.partial(_wkv_kernel, steps=tb),
        out_shape=(jax.ShapeDtypeStruct((T, RW_TILES, LANE), F32),
                   jax.ShapeDtypeStruct((T, RW_TILES, LANE), F32),
                   jax.ShapeDtypeStruct((2, RW_TILES, RW_N, LANE), F32)),
        grid=(nc,),
        in_specs=[rspec(0), rspec(1), sspec(0), sspec(1), vspec(0), vspec(1),
                  pl.BlockSpec((RW_FOLD, LANE, LANE), lambda c: (0, 0, 0))],
        out_specs=(ospec, ospec, pl.BlockSpec((2, RW_TILES, RW_N, LANE), lambda c: (0, 0, 0, 0))),
        scratch_shapes=[pltpu.VMEM((tb, RW_N, LANE), F32),
                        pltpu.VMEM((tb, RW_N, LANE), F32)],
        compiler_params=_cparams(("arbitrary",)),
        name="wkv",
    )(proj4, proj4, stack4, stack4, v3, v3, rep)


def _wkv_sample_kernel(x_ref, s0_ref, *rest, steps):
    o_ref, st_ref = rest[-2:]

    def row(v, carry):
        s = s0_ref[v]
        for t in range(steps):
            p = jnp.sum(s * x_ref[3, t], axis=0, keepdims=True)
            vt = x_ref[5, t, pl.ds(v, 1), :]
            s = s * x_ref[1, t] - p * x_ref[4, t] + vt * x_ref[2, t]
            o_ref[t, pl.ds(v, 1), :] = jnp.sum(s * x_ref[0, t], axis=0, keepdims=True)
        st_ref[v] = s
        return carry

    lax.fori_loop(0, RW_N, row, 0)


def wkv_sample(x6, state_t, layer, into=None):
    _, TS, H, _, DB = x6.shape
    L = state_t.shape[0]
    st_spec = pl.BlockSpec((None, None, RW_N, RW_N, DB), lambda h: (layer, h, 0, 0, 0))
    in_specs = [pl.BlockSpec((6, TS, None, RW_N, DB), lambda h: (0, 0, h, 0, 0)), st_spec]
    args = [x6, state_t]
    aliases = {}
    if into is not None:
        aliases = {2: 1}
        in_specs.append(pl.BlockSpec(memory_space=pl.ANY))
        args.append(into)
    return pl.pallas_call(
        functools.partial(_wkv_sample_kernel, steps=TS),
        out_shape=(jax.ShapeDtypeStruct((TS, H, RW_N, DB), F32),
                   jax.ShapeDtypeStruct((L, H, RW_N, RW_N, DB), F32)),
        grid=(H,),
        in_specs=in_specs,
        out_specs=(pl.BlockSpec((TS, None, RW_N, DB), lambda h: (0, h, 0, 0)), st_spec),
        input_output_aliases=aliases,
        compiler_params=_cparams(("parallel",)),
        name="wkv_sample",
    )(*args)


def _rw_post_kernel(oa_ref, ob_ref, os_ref, v_ref, rk_ref, z_ref, ln_ref, y_ref, o_scr, *, n_seq_blocks):
    i = pl.program_id(0)
    rows = y_ref.shape[0]
    in_a = i < n_seq_blocks
    in_b = jnp.logical_and(i >= n_seq_blocks, i < 2 * n_seq_blocks)
    s = jnp.zeros((rows, LANE), F32)
    for j in range(RW_TILES):
        js = slice(j * LANE, (j + 1) * LANE)
        oj = jnp.where(in_a, oa_ref[:, j, :], jnp.where(in_b, ob_ref[:, j, :], os_ref[:, js]))
        o_scr[:, js] = oj
        s = s + oj
    mean = _head_fold(s) * (1.0 / RW_N)
    s2 = jnp.zeros((rows, LANE), F32)
    for j in range(RW_TILES):
        d = o_scr[:, j * LANE:(j + 1) * LANE] - mean
        s2 = s2 + d * d
    rstd = lax.rsqrt(_head_fold(s2) * (1.0 / RW_N) + RW_GN_EPS)
    rk = rk_ref[...]
    for j in range(RW_TILES):
        js = slice(j * LANE, (j + 1) * LANE)
        o = (o_scr[:, js] - mean) * rstd * ln_ref[0:1, js] + ln_ref[1:2, js]
        o = o + rk * v_ref[:, js]
        y_ref[:, js] = (o * _silu(z_ref[:, js])).astype(BF16)


def rw_post(lay, o_a, o_b, o_s, v, rk, proj4, ln):
    te = lay.te
    row = pl.BlockSpec((te, D), lambda i: (i, 0))
    nsb = lay.blocks_per_seq
    npb = lay.n_p_blocks
    a_idx = lambda i: jnp.where(i < nsb, i, 0)
    b_idx = lambda i: jnp.where(jnp.logical_and(i >= nsb, i < 2 * nsb), i - nsb, 0)
    s_idx = lambda i: jnp.maximum(i - npb, 0)
    return pl.pallas_call(
        functools.partial(_rw_post_kernel, n_seq_blocks=nsb),
        out_shape=jax.ShapeDtypeStruct((lay.M, D), BF16),
        grid=(lay.nblocks,),
        in_specs=[pl.BlockSpec((te, RW_TILES, LANE), lambda i: (a_idx(i), 0, 0)),
                  pl.BlockSpec((te, RW_TILES, LANE), lambda i: (b_idx(i), 0, 0)),
                  pl.BlockSpec((te, D), lambda i: (s_idx(i), 0)),
                  row,
                  pl.BlockSpec((te, LANE), lambda i: (i, 0)),
                  pl.BlockSpec((None, te, D), lambda i: (2, i, 0)),
                  pl.BlockSpec((2, D), lambda i: (0, 0))],
        out_specs=row,
        scratch_shapes=[pltpu.VMEM((te, D), F32)],
        compiler_params=_cparams(("parallel",)),
        name="rw_post",
    )(o_a, o_b, o_s, v, rk, proj4, ln)


def _head_minor(a):
    lead = a.shape[:-1]
    return a.reshape(lead + (RW_H, RW_N)).swapaxes(-1, -2).reshape(lead + (D,))


def _tiles_to_state(s):
    n = s.shape[0]
    s = s.reshape(n, RW_TILES, RW_FOLD, RW_TILES, RW_FOLD, RW_H)
    return s.transpose(0, 5, 3, 2, 1, 4).reshape(n, RW_H, RW_N, RW_N)


def _pad_cols(w, n):
    return jnp.pad(w, ((0, 0), (0, n - w.shape[1])))


def _pad_rows(w, n):
    return jnp.pad(w, ((0, n - w.shape[0]), (0, 0)))


def rwkv_layer(lay, h, shift_state, state_t, layer, s_states, v_first, w):
    B, T, DB, TS, Mp, M = lay.B, lay.T, lay.DB, lay.TS, lay.Mp, lay.M
    x6 = rw_mix(lay, h, jnp.repeat(shift_state, TS, axis=0), w["mu"])
    w_in = jnp.stack([_head_minor(w["w_in"][c]).astype(BF16) for c in (0, 1, 3, 2)])
    proj4 = matmul(x6, w_in, name="rw_proj")
    vres = v_first is not None
    l1 = [w["v1"]] if vres else []
    l1 += [w["w1"], w["a1"]]
    l2 = [w["v2"]] if vres else []
    l2 += [w["w2"], w["a2"]]
    l1 = jnp.stack([_pad_cols(a, RW_LORA_PAD) for a in l1]).astype(BF16)
    l2 = jnp.stack([_head_minor(_pad_rows(a, RW_LORA_PAD)) for a in l2]).astype(BF16)
    lo1 = matmul(x6, l1, x_off=3 if vres else 4, act_out="tanh_group",
                 tanh_group=1 if vres else 0, out_dtype=BF16, name="rw_lora1")
    lora = matmul(lo1, l2, name="rw_lora2")
    plist = [w["w0"], w["a0"], w["k_k"], w["k_a"], w["r_k"].reshape(-1)]
    if vres:
        plist.append(w["v0"])
    plist += [jnp.zeros((D,), F32)] * (8 - len(plist))
    params = _head_minor(jnp.stack(plist))
    stack4, v, rk = rw_prep(lay, proj4, lora, params, v_first)
    v_first_out = v_first if vres else v

    assert B == 2
    o_a, o_b, p_st = wkv_prompt(proj4, stack4, v, T)
    xs = jnp.concatenate([proj4[0:1, Mp:], stack4[:, Mp:], v[None, Mp:]], axis=0)
    xs = xs.reshape(6, DB, TS, RW_N, RW_H).transpose(0, 2, 4, 3, 1)
    o_s, s_states = wkv_sample(xs, state_t, layer, s_states)
    o_s = o_s.transpose(3, 0, 2, 1).reshape(lay.Ms, D)

    ln = _head_minor(jnp.stack([w["ln_w"], w["ln_b"]]))
    y = rw_post(lay, o_a, o_b, o_s, v, rk, proj4, ln)
    w_out = w["w_out"].reshape(RW_H, RW_N, D).swapaxes(0, 1).reshape(D, D)
    o_proj = matmul2d(y, w_out.astype(BF16), name="rw_out")
    p_shift = h[T - 1:Mp:T]
    s_shift = h[Mp:].reshape(DB, TS, D)[:, -1]
    return o_proj, _tiles_to_state(p_st), p_shift, s_states, s_shift, v_first_out


def _rot_cols(w):
    half = MLA_ROPE // 2
    return jnp.concatenate([-w[..., half:], w[..., :half]], axis=-1)


def _mla_prep_kernel(pq_ref, pkv_ref, pkr_ref, gq_ref, gkv_ref, cs_ref,
                     cq_ref, ckv_ref, kr_ref, krb_ref):
    cq_ref[...] = _rms(pq_ref[...], gq_ref[...]).astype(BF16)
    ckv_ref[...] = _rms(pkv_ref[...], gkv_ref[...])
    t2 = pkr_ref[...]
    kr = t2 * cs_ref[0] + pltpu.roll(t2, MLA_ROPE, 1) * cs_ref[1]
    kr_ref[...] = kr[:, :MLA_ROPE]
    krb_ref[...] = kr.astype(BF16)


def mla_prep(lay, proj, g_q, g_kv, cs):
    te = lay.te
    M = lay.M
    return pl.pallas_call(
        _mla_prep_kernel,
        out_shape=(jax.ShapeDtypeStruct((M, MLA_QL), BF16),
                   jax.ShapeDtypeStruct((M, MLA_KVL), F32),
                   jax.ShapeDtypeStruct((M, MLA_ROPE), F32),
                   jax.ShapeDtypeStruct((M, LANE), BF16)),
        grid=(lay.nblocks,),
        in_specs=[pl.BlockSpec((te, MLA_QL), lambda i: (i, D // MLA_QL)),
                  pl.BlockSpec((te, MLA_KVL), lambda i: (i, D // MLA_KVL + 1)),
                  pl.BlockSpec((te, LANE), lambda i: (i, (D + MLA_QL + MLA_KVL) // LANE)),
                  pl.BlockSpec((1, MLA_QL), lambda i: (0, 0)),
                  pl.BlockSpec((1, MLA_KVL), lambda i: (0, 0)),
                  pl.BlockSpec((2, te, LANE), lambda i: (0, i, 0))],
        out_specs=(pl.BlockSpec((te, MLA_QL), lambda i: (i, 0)),
                   pl.BlockSpec((te, MLA_KVL), lambda i: (i, 0)),
                   pl.BlockSpec((te, MLA_ROPE), lambda i: (i, 0)),
                   pl.BlockSpec((te, LANE), lambda i: (i, 0))),
        compiler_params=_cparams(("parallel",)),
        name="mla_prep",
    )(proj, proj, proj, g_q[None], g_kv[None], cs)


def _mla_q_kernel(q_ref, cs_ref, qn_ref, qr_ref):
    for h in range(MLA_H):
        qn_ref[:, h * LANE:(h + 1) * LANE] = q_ref[:, 2 * h * LANE:(2 * h + 1) * LANE].astype(BF16)
        t2 = q_ref[:, (2 * h + 1) * LANE:(2 * h + 2) * LANE]
        qr = t2 * cs_ref[0] + pltpu.roll(t2, MLA_ROPE, 1) * cs_ref[1]
        qr_ref[:, h * LANE:(h + 1) * LANE] = qr.astype(BF16)


def mla_q(lay, q, cs):
    te = lay.te
    M = lay.M
    return pl.pallas_call(
        _mla_q_kernel,
        out_shape=(jax.ShapeDtypeStruct((M, MLA_H * LANE), BF16),
                   jax.ShapeDtypeStruct((M, MLA_H * LANE), BF16)),
        grid=(lay.nblocks,),
        in_specs=[pl.BlockSpec((te, MLA_H * 2 * LANE), lambda i: (i, 0)),
                  pl.BlockSpec((2, te, LANE), lambda i: (0, i, 0))],
        out_specs=(pl.BlockSpec((te, MLA_H * LANE), lambda i: (i, 0)),
                   pl.BlockSpec((te, MLA_H * LANE), lambda i: (i, 0))),
        compiler_params=_cparams(("parallel",)),
        name="mla_q",
    )(q, cs)


def _flash_kernel(qn_ref, qr_ref, kn_ref, kr_ref, v_ref, o_ref, *, tq):
    qi = pl.program_id(2)
    q = jnp.concatenate([qn_ref[...], qr_ref[...]], axis=1)
    nt = (((1,), (1,)), ((), ()))
    c = MLA_SCALE * math.log2(math.e)

    def update(start, carry, diagonal):
        m, l, acc = carry
        k = jnp.concatenate([kn_ref[pl.ds(start, tq), :], kr_ref[pl.ds(start, tq), :]], axis=1)
        s = lax.dot_general(q, k, nt, preferred_element_type=F32) * c
        if diagonal:
            s = jnp.where(lax.broadcasted_iota(jnp.int32, (tq, tq), 1)
                          <= lax.broadcasted_iota(jnp.int32, (tq, tq), 0), s, -1e30)
        m_new = jnp.maximum(m, jnp.max(s, axis=1, keepdims=True))
        p = jnp.exp2(s - m_new)
        alpha = jnp.exp2(m - m_new)
        l = alpha * l + jnp.sum(p, axis=1, keepdims=True)
        acc = alpha * acc + jnp.dot(p.astype(BF16), v_ref[pl.ds(start, tq), :],
                                    preferred_element_type=F32)
        return m_new, l, acc

    def body(ki, carry):
        return update(pl.multiple_of(ki * tq, tq), carry, False)

    init = (jnp.full((tq, 1), -1e30, F32), jnp.zeros((tq, 1), F32), jnp.zeros((tq, MLA_V), F32))
    carry = lax.fori_loop(0, qi, body, init)
    m, l, acc = update(pl.multiple_of(qi * tq, tq), carry, True)
    o_ref[...] = acc / l


def mla_flash(qn, qr, knv, krb, B, T):
    tq = _tile(T, 512)
    nq = T // tq
    return pl.pallas_call(
        functools.partial(_flash_kernel, tq=tq),
        out_shape=jax.ShapeDtypeStruct((B * T, MLA_H * MLA_V), F32),
        grid=(B, MLA_H, nq),
        in_specs=[pl.BlockSpec((tq, LANE), lambda b, h, i: (b * nq + i, h)),
                  pl.BlockSpec((tq, LANE), lambda b, h, i: (b * nq + i, h)),
                  pl.BlockSpec((T, LANE), lambda b, h, i: (b, h)),
                  pl.BlockSpec((T, LANE), lambda b, h, i: (b, 0)),
                  pl.BlockSpec((T, LANE), lambda b, h, i: (b, MLA_H + h))],
        out_specs=pl.BlockSpec((tq, MLA_V), lambda b, h, i: (b * nq + i, h)),
        compiler_params=_cparams(("parallel", "parallel", "arbitrary")),
        name="mla_flash",
    )(qn, qr, knv, krb, knv)


def _paged_kernel(pt_ref, ql_ref, qr_ref, cn_ref, kn_ref, *rest, pg, ts):
    ck_refs = rest[:pg]
    kr_refs = rest[pg:2 * pg]
    o_ref, m_ref, l_ref, acc_ref = rest[2 * pg:]
    p = pl.program_id(1)
    nt = (((1,), (1,)), ((), ()))

    @pl.when(p == 0)
    def _():
        m_ref[...] = jnp.full(m_ref.shape, -1e30, F32)
        l_ref[...] = jnp.zeros(l_ref.shape, F32)
        acc_ref[...] = jnp.zeros(acc_ref.shape, F32)

    rows = ts * MLA_H
    ql = ql_ref[...]
    qr = qr_ref[...][:, :MLA_ROPE]
    qlp = jnp.concatenate([ql, jnp.zeros((LANE - rows, MLA_KVL), BF16)], axis=0)
    ck = jnp.concatenate([r[...].astype(BF16) for r in ck_refs], axis=0)
    kr_t = jnp.concatenate([r[...].astype(BF16) for r in kr_refs], axis=1)
    s_t = lax.dot_general(ck, qlp, nt, preferred_element_type=F32)
    s = (s_t.T[:rows] + jnp.dot(qr, kr_t, preferred_element_type=F32)) * MLA_SCALE
    m = m_ref[...]
    m_new = jnp.maximum(m, jnp.max(s, axis=1, keepdims=True))
    pr = jnp.exp(s - m_new)
    alpha = jnp.exp(m - m_new)
    l = alpha * l_ref[...] + jnp.sum(pr, axis=1, keepdims=True)
    acc = alpha * acc_ref[...] + jnp.dot(pr.astype(BF16), ck, preferred_element_type=F32)
    m = m_new
    m_ref[...] = m
    l_ref[...] = l
    acc_ref[...] = acc

    @pl.when(p == pl.num_programs(1) - 1)
    def _():
        qlf = ql.astype(F32)
        qrf = qr.astype(F32)
        tok = lax.broadcasted_iota(jnp.int32, (ts * MLA_H, 1), 0) // MLA_H
        cols = []
        for j in range(ts):
            cn = cn_ref[j:j + 1, :].astype(BF16).astype(F32)
            kn = kn_ref[j:j + 1, :MLA_ROPE].astype(F32)
            sj = (jnp.sum(qlf * cn, axis=1, keepdims=True)
                  + jnp.sum(qrf * kn, axis=1, keepdims=True)) * MLA_SCALE
            cols.append(jnp.where(tok >= j, sj, -1e30))
        m2 = m
        for sj in cols:
            m2 = jnp.maximum(m2, sj)
        alpha = jnp.exp(m - m2)
        l2 = alpha * l
        acc2 = alpha * acc
        for j, sj in enumerate(cols):
            pj = jnp.exp(sj - m2)
            l2 = l2 + pj
            acc2 = acc2 + pj.astype(BF16).astype(F32) * cn_ref[j:j + 1, :].astype(BF16).astype(F32)
        o_ref[...] = (acc2 / l2).astype(o_ref.dtype)


def mla_paged(page_table, ql, qr, ckv_new, krb_new, cache_ckv, cache_kr, layer, DB, TS):
    n_pages = page_table.shape[1]
    pg = _tile(n_pages, 16) if n_pages >= 8 else n_pages
    rows = TS * MLA_H
    pt = page_table.reshape(-1)

    def page_spec(i, shape):
        return pl.BlockSpec((None, None) + shape,
                            lambda b, p, pt_ref: (layer, pt_ref[b * n_pages + p * pg + i], 0, 0))

    cache_kr_t = jnp.swapaxes(cache_kr, 2, 3)
    per_b = lambda width: pl.BlockSpec((None, rows, width), lambda b, p, pt_ref: (b, 0, 0))
    new_b = lambda width: pl.BlockSpec((None, TS, width), lambda b, p, pt_ref: (b, 0, 0))
    grid_spec = pltpu.PrefetchScalarGridSpec(
        num_scalar_prefetch=1,
        grid=(DB, n_pages // pg),
        in_specs=[per_b(MLA_KVL), per_b(LANE), new_b(MLA_KVL), new_b(LANE)]
                 + [page_spec(i, (PAGE, MLA_KVL)) for i in range(pg)]
                 + [page_spec(i, (MLA_ROPE, PAGE)) for i in range(pg)],
        out_specs=per_b(MLA_KVL),
        scratch_shapes=[pltpu.VMEM((rows, 1), F32), pltpu.VMEM((rows, 1), F32),
                        pltpu.VMEM((rows, MLA_KVL), F32)],
    )
    return pl.pallas_call(
        functools.partial(_paged_kernel, pg=pg, ts=TS),
        out_shape=jax.ShapeDtypeStruct((DB, rows, MLA_KVL), BF16),
        grid_spec=grid_spec,
        compiler_params=_cparams(("parallel", "arbitrary")),
        name="mla_paged",
    )(pt, ql, qr, ckv_new, krb_new, *([cache_ckv] * pg), *([cache_kr_t] * pg))


def _gate_kernel(o_ref, z_ref, y_ref):
    y_ref[...] = (o_ref[...] * _silu(z_ref[...])).astype(BF16)


def gate_mul(lay, o, zsrc, zcol):
    te = lay.te
    return pl.pallas_call(
        _gate_kernel,
        out_shape=jax.ShapeDtypeStruct((lay.M, D), BF16),
        grid=(lay.nblocks,),
        in_specs=[pl.BlockSpec((te, D), lambda i: (i, 0)),
                  pl.BlockSpec((te, D), lambda i: (i, zcol))],
        out_specs=pl.BlockSpec((te, D), lambda i: (i, 0)),
        compiler_params=_cparams(("parallel",)),
        name="gate_mul",
    )(o, zsrc)


def mla_layer(lay, h, cache_ckv, cache_kr, layer, page_table, w):
    B, T, DB, TS, Mp, M = lay.B, lay.T, lay.DB, lay.TS, lay.Mp, lay.M
    w_in = w["w_in"]
    w_kr = w_in[:, 2 * MLA_QL:2 * MLA_QL + MLA_ROPE]
    w1 = jnp.concatenate([w_in[:, 2 * MLA_QL + MLA_ROPE:], w_in[:, :2 * MLA_QL],
                          w_kr, _rot_cols(w_kr)], axis=1)
    proj = matmul2d(h, w1.astype(BF16), tn=640, name="mla_proj")
    pos = jnp.concatenate([jnp.tile(jnp.arange(T, dtype=F32), B),
                           jnp.tile(page_table.shape[1] * PAGE + jnp.arange(TS, dtype=F32), DB)])
    half = MLA_ROPE // 2
    inv_freq = ROPE_THETA ** (-jnp.arange(half, dtype=F32) / half)
    ang = pos[:, None] * inv_freq[None, :]
    zeros = jnp.zeros((M, MLA_ROPE), F32)
    cs = jnp.stack([jnp.concatenate([jnp.cos(ang), jnp.cos(ang), zeros], axis=1),
                    jnp.concatenate([jnp.sin(ang), jnp.sin(ang), zeros], axis=1)])
    cq, ckv, kr, krb = mla_prep(lay, proj, w["q_norm"], w["kv_norm"], cs)
    w_uq = w["w_uq"]
    wq = jnp.concatenate([w_uq, _rot_cols(w_uq[..., MLA_NOPE:])], axis=-1)
    q = matmul2d(cq, wq.reshape(MLA_QL, MLA_H * 2 * LANE).astype(BF16), name="mla_qproj")
    qn, qr = mla_q(lay, q, cs)
    w_kv = jnp.concatenate([w["w_uk"].reshape(MLA_KVL, -1), w["w_uv"].reshape(MLA_KVL, -1)], axis=1)
    knv = matmul2d(ckv[:Mp], w_kv.astype(BF16), out_dtype=BF16, name="mla_kv")
    o_p = mla_flash(qn, qr, knv, krb, B, T)
    w_ukT = w["w_uk"].transpose(1, 2, 0).astype(BF16)
    ql = head_matmul(qn[Mp:], w_ukT, BF16, "mla_qlat")
    o_lat = mla_paged(page_table, ql.reshape(DB, TS * MLA_H, MLA_KVL),
                      qr[Mp:].reshape(DB, TS * MLA_H, LANE),
                      ckv[Mp:].reshape(DB, TS, MLA_KVL), krb[Mp:].reshape(DB, TS, LANE),
                      cache_ckv, cache_kr, layer, DB, TS)
    o_s = head_matmul(o_lat.reshape(lay.Ms, MLA_H * MLA_KVL),
                      w["w_uv"].transpose(1, 0, 2).astype(BF16), F32, "mla_ouv")
    o = jnp.concatenate([o_p, o_s], axis=0)
    y = gate_mul(lay, o, proj, 0)
    o_proj = matmul2d(y, w["w_out"].astype(BF16), name="mla_out")
    return (o_proj, ckv[:Mp].reshape(B, T, MLA_KVL), kr[:Mp].reshape(B, T, MLA_ROPE),
            ckv[Mp:].reshape(DB, TS, MLA_KVL), kr[Mp:].reshape(DB, TS, MLA_ROPE))


def _gla_kernel(q_ref, k_ref, v_ref, g_ref, s0_ref, o_ref, sT_ref, st_ref, *, L):
    c = pl.program_id(2)
    C = GLA_SUB
    assert L % C == 0

    @pl.when(c == 0)
    def _():
        st_ref[...] = s0_ref[...].T

    q = q_ref[...] * (GLA_DK ** -0.5)
    k = k_ref[...]
    v = v_ref[...].astype(BF16)
    trow = lax.broadcasted_iota(jnp.int32, (L, 1), 0)
    b = g_ref[...]
    sh = 1
    while sh < L:
        b = b + jnp.where(trow >= sh, pltpu.roll(b, sh, 0), 0.0)
        sh *= 2
    st = st_ref[...]
    nt = (((1,), (1,)), ((), ()))
    tn = (((0,), (0,)), ((), ()))
    o = lax.dot_general((q * jnp.exp(b)).astype(BF16), st.astype(BF16), nt, preferred_element_type=F32)
    crow = lax.broadcasted_iota(jnp.int32, (C, 1), 0)
    col = lax.broadcasted_iota(jnp.int32, (C, L), 1)
    att_rows = []
    for i in range(L // C):
        lo = i * C
        qi, ki, bi = q[lo:lo + C], k[lo:lo + C], b[lo:lo + C]
        if i == 0:
            att_i = jnp.zeros((C, L), F32)
        else:
            beta = b[lo - 1:lo, :]
            q_in = (qi * jnp.exp(bi - beta)).astype(BF16)
            k_out = (k * jnp.exp(jnp.where(trow < lo, beta - b, -jnp.inf))).astype(BF16)
            att_i = lax.dot_general(q_in, k_out, nt, preferred_element_type=F32)
        for s in range(C):
            e = jnp.where(crow >= s, bi - bi[s:s + 1, :], -jnp.inf)
            a_s = jnp.sum(qi * ki[s:s + 1, :] * jnp.exp(e), axis=1, keepdims=True)
            att_i = jnp.where(col == lo + s, a_s, att_i)
        att_rows.append(att_i)
    att = att_rows[0] if len(att_rows) == 1 else jnp.concatenate(att_rows, axis=0)
    o = o + jnp.dot(att.astype(BF16), v, preferred_element_type=F32)
    o_ref[...] = o
    b_end = b[L - 1:L, :]
    kd = (k * jnp.exp(b_end - b)).astype(BF16)
    st = st * jnp.exp(b_end) + lax.dot_general(v, kd, tn, preferred_element_type=F32)
    st_ref[...] = st

    @pl.when(c == pl.num_programs(2) - 1)
    def _():
        sT_ref[...] = st.T


def gla_scan(q, k, v, g, s0, *, nseq, nchunk, L, row0, qcol, kcol, vcol):
    rows = lambda n, h, c: row0 + n * nchunk + c
    return pl.pallas_call(
        functools.partial(_gla_kernel, L=L),
        out_shape=(jax.ShapeDtypeStruct((q.shape[0], GLA_H * GLA_DV), F32),
                   jax.ShapeDtypeStruct((nseq, GLA_H, GLA_DK, GLA_DV), F32)),
        grid=(nseq, GLA_H, nchunk),
        in_specs=[pl.BlockSpec((L, GLA_DK), lambda n, h, c: (rows(n, h, c), qcol + h)),
                  pl.BlockSpec((L, GLA_DK), lambda n, h, c: (rows(n, h, c), kcol + h)),
                  pl.BlockSpec((L, GLA_DV), lambda n, h, c: (rows(n, h, c), vcol + h)),
                  pl.BlockSpec((L, GLA_DK), lambda n, h, c: (rows(n, h, c), h)),
                  pl.BlockSpec((None, None, GLA_DK, GLA_DV), lambda n, h, c: (n, h, 0, 0))],
        out_specs=(pl.BlockSpec((L, GLA_DV), lambda n, h, c: (rows(n, h, c), h)),
                   pl.BlockSpec((None, None, GLA_DK, GLA_DV), lambda n, h, c: (n, h, 0, 0))),
        scratch_shapes=[pltpu.VMEM((GLA_DV, GLA_DK), F32)],
        compiler_params=_cparams(("parallel", "parallel", "arbitrary")),
        name="gla_scan",
    )(q, k, v, g, s0)


def _gla_post_kernel(o_ref, z_ref, g_ref, y_ref):
    for h in range(GLA_H):
        hs = slice(h * GLA_DV, (h + 1) * GLA_DV)
        y_ref[:, hs] = (_rms(o_ref[:, hs], g_ref[...]) * _silu(z_ref[:, hs])).astype(BF16)


def gla_post(lay, o, proj, g_norm):
    te = lay.te
    return pl.pallas_call(
        _gla_post_kernel,
        out_shape=jax.ShapeDtypeStruct((lay.M, D), BF16),
        grid=(lay.nblocks,),
        in_specs=[pl.BlockSpec((te, D), lambda i: (i, 0)),
                  pl.BlockSpec((te, D), lambda i: (i, 2)),
                  pl.BlockSpec((1, GLA_DV), lambda i: (0, 0))],
        out_specs=pl.BlockSpec((te, D), lambda i: (i, 0)),
        compiler_params=_cparams(("parallel",)),
        name="gla_post",
    )(o, proj, g_norm[None])


def gla_layer(lay, h, state, w):
    B, T, DB, TS, Mp, M = lay.B, lay.T, lay.DB, lay.TS, lay.Mp, lay.M
    w_in = _pad_cols(w["w_in"], 2 * GLA_QK + 2 * D + GLA_LORA_PAD)
    proj = matmul2d(h, w_in.astype(BF16), tn=640, name="gla_proj")
    gl = proj[:, 2 * GLA_QK + 2 * D:]
    g = matmul2d(gl, _pad_rows(w["w_g2"], GLA_LORA_PAD).astype(BF16), w["b_g"],
                 act_out="logsig_tau", name="gla_gate")
    L = math.gcd(T, GLA_CHUNK)
    nk = GLA_QK // GLA_DK
    o_p, p_state = gla_scan(proj, proj, proj, g, jnp.zeros((B, GLA_H, GLA_DK, GLA_DV), F32),
                            nseq=B, nchunk=T // L, L=L, row0=0,
                            qcol=0, kcol=nk, vcol=2 * GLA_QK // GLA_DV)
    LS = GLA_SUB
    pad = lambda a: jnp.pad(a.reshape(DB, TS, -1), ((0, 0), (0, LS - TS), (0, 0))).reshape(DB * LS, -1)
    ps = pad(proj[Mp:, :2 * GLA_QK + D])
    o_s, s_state = gla_scan(ps, ps, ps, pad(g[Mp:]), state, nseq=DB, nchunk=1, L=LS, row0=0,
                            qcol=0, kcol=nk, vcol=2 * GLA_QK // GLA_DV)
    o_s = o_s.reshape(DB, LS, D)[:, :TS].reshape(lay.Ms, D)
    o = jnp.concatenate([o_p[:Mp], o_s], axis=0)
    y = gla_post(lay, o, proj, w["norm"])
    o_proj = matmul2d(y, w["w_out"].astype(BF16), name="gla_out")
    return o_proj, p_state, s_state


def kernel(x_prompt, x_sample, c_prompt, c_sample, state_rwkv_wkv, state_rwkv_shift, cache_mla_ckv, cache_mla_krope, page_table, state_gla, norm_pre, norm_post, ada_w, ada_b, rw_mu, rw_w_in, rw_w0, rw_w1, rw_w2, rw_a0, rw_a1, rw_a2, rw_v0, rw_v1, rw_v2, rw_k_k, rw_k_a, rw_r_k, rw_ln_w, rw_ln_b, rw_w_out, mla_w_in, mla_q_norm, mla_kv_norm, mla_w_uq, mla_w_uk, mla_w_uv, mla_w_out, gla_w_in, gla_w_g2, gla_b_g, gla_norm, gla_w_out):
    B, T, _ = x_prompt.shape
    DB, TS, _ = x_sample.shape
    depth = norm_pre.shape[0]
    lay = RowLayout(B, T, DB, TS)
    Mp = lay.Mp
    x = jnp.concatenate([x_prompt.reshape(Mp, D), x_sample.reshape(lay.Ms, D)], axis=0)
    c = jnp.concatenate([c_prompt, c_sample], axis=0)[None]
    c = jnp.broadcast_to(c, (depth,) + c.shape[1:])
    mods = matmul(c, ada_w, ada_b[:, None, :], act_in="silu", name="ada")

    p_wkv, p_shift, p_ckv, p_kr, p_gla = [], [], [], [], []
    s_shift, s_ckv, s_kr, s_gla = [], [], [], []
    v_first = None
    state_t = jnp.transpose(state_rwkv_wkv, (0, 2, 3, 4, 1))
    s_wkv_t = None
    for i in range(depth):
        kind, j = i % 3, i // 3
        mod = lay.expand_mod(mods[i])
        h = prenorm(lay, x, norm_pre[i], mod, F32 if kind == 0 else BF16)
        if kind == 0:
            w = dict(mu=rw_mu[j], w_in=rw_w_in[j], w0=rw_w0[j], w1=rw_w1[j], w2=rw_w2[j],
                     a0=rw_a0[j], a1=rw_a1[j], a2=rw_a2[j], k_k=rw_k_k[j], k_a=rw_k_a[j],
                     r_k=rw_r_k[j], ln_w=rw_ln_w[j], ln_b=rw_ln_b[j], w_out=rw_w_out[j])
            if j > 0:
                w.update(v0=rw_v0[j - 1], v1=rw_v1[j - 1], v2=rw_v2[j - 1])
            o, pst, psh, s_wkv_t, ssh, v_first = rwkv_layer(
                lay, h, state_rwkv_shift[j], state_t, j, s_wkv_t, v_first if j > 0 else None, w)
            p_wkv.append(pst)
            p_shift.append(psh)
            s_shift.append(ssh)
        elif kind == 1:
            w = dict(w_in=mla_w_in[j], q_norm=mla_q_norm[j], kv_norm=mla_kv_norm[j],
                     w_uq=mla_w_uq[j], w_uk=mla_w_uk[j], w_uv=mla_w_uv[j], w_out=mla_w_out[j])
            o, pc, pk, sc, sk = mla_layer(lay, h, cache_mla_ckv, cache_mla_krope, j, page_table, w)
            p_ckv.append(pc)
            p_kr.append(pk)
            s_ckv.append(sc)
            s_kr.append(sk)
        else:
            w = dict(w_in=gla_w_in[j], w_g2=gla_w_g2[j], b_g=gla_b_g[j], norm=gla_norm[j],
                     w_out=gla_w_out[j])
            o, pg, sg = gla_layer(lay, h, state_gla[j], w)
            p_gla.append(pg)
            s_gla.append(sg)
        x = residual(lay, x, o, norm_post[i], mod)
    yp = x[:Mp].reshape(B, T, D)
    ys = x[Mp:].reshape(DB, TS, D)
    return (yp, ys,
            jnp.stack(p_wkv), jnp.stack(p_shift), jnp.stack(p_ckv), jnp.stack(p_kr), jnp.stack(p_gla),
            jnp.transpose(s_wkv_t, (0, 4, 1, 2, 3)), jnp.stack(s_shift), jnp.stack(s_ckv),
            jnp.stack(s_kr), jnp.stack(s_gla))
```

```python
import functools
import math

import jax
import jax.numpy as jnp
import numpy as np
from jax import lax
from jax.experimental import pallas as pl
from jax.experimental.pallas import tpu as pltpu

F32 = jnp.float32
BF16 = jnp.bfloat16

D = 2048
NORM_EPS = 1e-6
RW_N = 64
RW_H = D // RW_N
RW_LORA_PAD = 128
RW_GN_EPS = 64e-5
RW_TILES = D // 128
RW_FOLD = 128 // RW_H
EXP_M05 = math.exp(-0.5)
MLA_H = 16
MLA_NOPE = 128
MLA_ROPE = 64
MLA_V = 128
MLA_QL = 512
MLA_KVL = 512
MLA_SCALE = (MLA_NOPE + MLA_ROPE) ** -0.5
ROPE_THETA = 10000.0
PAGE = 128
GLA_H = 4
GLA_DK = 256
GLA_DV = 512
GLA_QK = GLA_H * GLA_DK
GLA_LORA_PAD = 256
GLA_TAU = 16.0
GLA_CHUNK = 64
GLA_SUB = 16

LANE = 128
ROW_TILE = 128
VMEM_LIMIT = 48 * 1024 * 1024


def _cparams(sem):
    return pltpu.CompilerParams(dimension_semantics=sem, vmem_limit_bytes=VMEM_LIMIT)


def _tile(n, pref):
    for t in (1024, 640, 512, 384, 256, 128, 64, 32, 16, 8):
        if t <= pref and n % t == 0:
            return t
    return n


def _sigmoid(x):
    return 1.0 / (1.0 + jnp.exp(-x))


def _silu(x):
    return x * _sigmoid(x)


def _mm_kernel(*refs, has_bias, act_in, act_out, tanh_group):
    if has_bias:
        x_ref, w_ref, b_ref, o_ref = refs
    else:
        x_ref, w_ref, o_ref = refs
    x = x_ref[...]
    if act_in == "silu":
        x = _silu(x.astype(F32))
    acc = jnp.dot(x.astype(BF16), w_ref[...].astype(BF16), preferred_element_type=F32)
    if has_bias:
        acc = acc + b_ref[...]
    if act_out == "tanh_group":
        acc = jnp.where(pl.program_id(0) == tanh_group, jnp.tanh(acc), acc)
    elif act_out == "logsig_tau":
        acc = (jnp.minimum(acc, 0.0) - jnp.log(1.0 + jnp.exp(-jnp.abs(acc)))) * (1.0 / GLA_TAU)
    o_ref[...] = acc.astype(o_ref.dtype)


MM_X_TILE_BYTES = 9 * 1024 * 1024
MM_MAX_ROWS = 2304


def _row_tile(M, row_bytes):
    for parts in range(1, M // 8 + 1):
        rows = M // parts
        if M % parts == 0 and rows % 8 == 0 and rows <= MM_MAX_ROWS and rows * row_bytes <= MM_X_TILE_BYTES:
            return rows
    return M


def matmul(x, w, bias=None, *, x_off=0, act_in=None, act_out=None, tanh_group=0,
           out_dtype=F32, tn=512, name="mm"):
    G, K, N = w.shape
    M = x.shape[1]
    tm = _row_tile(M, K * x.dtype.itemsize)
    tn = _tile(N, tn)
    in_specs = [
        pl.BlockSpec((None, tm, K), lambda g, i, j: (g + x_off, i, 0)),
        pl.BlockSpec((None, K, tn), lambda g, i, j: (g, 0, j)),
    ]
    args = [x, w]
    if bias is not None:
        in_specs.append(pl.BlockSpec((None, 1, tn), lambda g, i, j: (g, 0, j)))
        args.append(bias)
    return pl.pallas_call(
        functools.partial(_mm_kernel, has_bias=bias is not None, act_in=act_in,
                          act_out=act_out, tanh_group=tanh_group),
        out_shape=jax.ShapeDtypeStruct((G, M, N), out_dtype),
        grid=(G, M // tm, N // tn),
        in_specs=in_specs,
        out_specs=pl.BlockSpec((None, tm, tn), lambda g, i, j: (g, i, j)),
        compiler_params=_cparams(("parallel", "parallel", "arbitrary")),
        name=name,
    )(*args)


def matmul2d(x, w, bias=None, **kw):
    b3 = None if bias is None else bias[None, None, :]
    return matmul(x[None], w[None], b3, **kw)[0]


def _hmm_kernel(x_ref, w_ref, o_ref):
    o_ref[...] = jnp.dot(x_ref[...].astype(BF16), w_ref[...],
                         preferred_element_type=F32).astype(o_ref.dtype)


def head_matmul(x, w, out_dtype, name):
    H, Kh, Nh = w.shape
    M = x.shape[0]
    tm = _tile(M, 512)
    return pl.pallas_call(
        _hmm_kernel,
        out_shape=jax.ShapeDtypeStruct((M, H * Nh), out_dtype),
        grid=(H, M // tm),
        in_specs=[pl.BlockSpec((tm, Kh), lambda h, i: (i, h)),
                  pl.BlockSpec((None, Kh, Nh), lambda h, i: (h, 0, 0))],
        out_specs=pl.BlockSpec((tm, Nh), lambda h, i: (i, h)),
        compiler_params=_cparams(("parallel", "arbitrary")),
        name=name,
    )(x, w)


class RowLayout:
    def __init__(self, B, T, DB, TS):
        self.B, self.T, self.DB, self.TS = B, T, DB, TS
        self.Mp = B * T
        self.Ms = DB * TS
        self.M = self.Mp + self.Ms
        self.te = math.gcd(math.gcd(ROW_TILE, T), self.Ms)
        self.n_p_blocks = self.Mp // self.te
        self.blocks_per_seq = T // self.te
        self.nblocks = self.M // self.te

    def mod_index(self, i):
        return jnp.where(i < self.n_p_blocks, i // self.blocks_per_seq,
                         self.B + i - self.n_p_blocks)

    def expand_mod(self, m):
        mp = jnp.broadcast_to(m[:self.B, None, :], (self.B, self.te, m.shape[-1]))
        ms = jnp.repeat(m[self.B:], self.TS, axis=0).reshape(self.Ms // self.te, self.te, -1)
        return jnp.concatenate([mp, ms], axis=0)


def _rms(x, g):
    ms = jnp.mean(x * x, axis=-1, keepdims=True)
    return x * lax.rsqrt(ms + NORM_EPS) * g


def _prenorm_kernel(x_ref, g_ref, shift_ref, scale_ref, h_ref):
    h = _rms(x_ref[...], g_ref[...]) * (1.0 + scale_ref[...]) + shift_ref[...]
    h_ref[...] = h.astype(h_ref.dtype)


def prenorm(lay, x, g_pre, mod, out_dtype):
    te = lay.te
    row = pl.BlockSpec((te, D), lambda i: (i, 0))
    return pl.pallas_call(
        _prenorm_kernel,
        out_shape=jax.ShapeDtypeStruct((lay.M, D), out_dtype),
        grid=(lay.nblocks,),
        in_specs=[row,
                  pl.BlockSpec((1, D), lambda i: (0, 0)),
                  pl.BlockSpec((None, te, D), lambda i: (lay.mod_index(i), 0, 0)),
                  pl.BlockSpec((None, te, D), lambda i: (lay.mod_index(i), 0, 1))],
        out_specs=row,
        compiler_params=_cparams(("parallel",)),
        name="prenorm",
    )(x, g_pre[None], mod, mod)


def _residual_kernel(x_ref, o_ref, g_ref, gate_ref, y_ref):
    y_ref[...] = x_ref[...] + gate_ref[...] * _rms(o_ref[...], g_ref[...])


def residual(lay, x, o, g_post, mod):
    te = lay.te
    row = pl.BlockSpec((te, D), lambda i: (i, 0))
    return pl.pallas_call(
        _residual_kernel,
        out_shape=jax.ShapeDtypeStruct((lay.M, D), F32),
        grid=(lay.nblocks,),
        in_specs=[row, row,
                  pl.BlockSpec((1, D), lambda i: (0, 0)),
                  pl.BlockSpec((None, te, D), lambda i: (lay.mod_index(i), 0, 2))],
        out_specs=row,
        compiler_params=_cparams(("parallel",)),
        name="residual",
    )(x, o, g_post[None], mod)


RW_MIX_ORDER = (0, 1, 3, 2, 4, 5)


SUBLANES = 8


def _rw_mix_kernel(h_ref, tail_ref, first_ref, mu_ref, x_ref, *, blocks_per_seq, n_p_blocks, ts):
    i = pl.program_id(0)
    h = h_ref[...]
    rows = h.shape[0]
    ridx = lax.broadcasted_iota(jnp.int32, (rows, 1), 0)
    prev = pltpu.roll(h, 1, 0)
    tail = tail_ref[SUBLANES - 1:SUBLANES, :]
    opens = jnp.logical_and(i < n_p_blocks, i % blocks_per_seq == 0)
    prev = jnp.where(ridx == 0, jnp.where(opens, 0.0, tail), prev)
    is_sample = i >= n_p_blocks
    prev = jnp.where(jnp.logical_and(is_sample, ridx % ts == 0), first_ref[...], prev)
    dx = prev - h
    for o, c in enumerate(RW_MIX_ORDER):
        x_ref[o] = (h + dx * mu_ref[c:c + 1, :]).astype(BF16)


def rw_mix(lay, h, shift_rows, mu):
    te = lay.te
    per = te // SUBLANES
    return pl.pallas_call(
        functools.partial(_rw_mix_kernel, blocks_per_seq=lay.blocks_per_seq,
                          n_p_blocks=lay.n_p_blocks, ts=lay.TS),
        out_shape=jax.ShapeDtypeStruct((6, lay.M, D), BF16),
        grid=(lay.nblocks,),
        in_specs=[pl.BlockSpec((te, D), lambda i: (i, 0)),
                  pl.BlockSpec((SUBLANES, D), lambda i: (jnp.maximum(i * per - 1, 0), 0)),
                  pl.BlockSpec((te, D), lambda i: (jnp.maximum(i - lay.n_p_blocks, 0), 0)),
                  pl.BlockSpec((6, D), lambda i: (0, 0))],
        out_specs=pl.BlockSpec((6, te, D), lambda i: (0, i, 0)),
        compiler_params=_cparams(("parallel",)),
        name="rw_mix",
    )(h, h, shift_rows, mu)


def _head_fold(s):
    s = s + pltpu.roll(s, RW_H, 1)
    return s + pltpu.roll(s, 2 * RW_H, 1)


def _rw_prep_kernel(*refs, vres):
    if vres:
        p_ref, l_ref, par_ref, vf_ref, out_ref, v_ref, rk_ref = refs
        iw, ia = 1, 2
    else:
        p_ref, l_ref, par_ref, out_ref, v_ref, rk_ref = refs
        iw, ia = 0, 1
    rows = out_ref.shape[1]
    n2 = jnp.zeros((rows, LANE), F32)
    rk = jnp.zeros((rows, LANE), F32)
    for j in range(RW_TILES):
        js = slice(j * LANE, (j + 1) * LANE)
        par = lambda i: par_ref[i:i + 1, js]
        r = p_ref[0, :, js]
        k = p_ref[1, :, js]
        v = p_ref[3, :, js]
        decay = jnp.exp(-EXP_M05 * _sigmoid(par(0) + l_ref[iw, :, js]))
        a = _sigmoid(par(1) + l_ref[ia, :, js])
        kk = k * par(2)
        n2 = n2 + kk * kk
        kmod = k * (1.0 + (a - 1.0) * par(3))
        rk = rk + r * kmod * par(4)
        if vres:
            v = v + (vf_ref[:, js] - v) * _sigmoid(par(5) + l_ref[0, :, js])
        out_ref[0, :, js] = decay
        out_ref[1, :, js] = kmod
        out_ref[2, :, js] = kk
        out_ref[3, :, js] = a
        v_ref[:, js] = v
    inv = 1.0 / jnp.maximum(jnp.sqrt(_head_fold(n2)), 1e-12)
    rk_ref[...] = _head_fold(rk)
    for j in range(RW_TILES):
        js = slice(j * LANE, (j + 1) * LANE)
        kkn = out_ref[2, :, js] * inv
        out_ref[2, :, js] = kkn
        out_ref[3, :, js] = kkn * out_ref[3, :, js]


def rw_prep(lay, proj, lora, params, v_first):
    te = lay.te
    vres = v_first is not None
    nl = lora.shape[0]
    in_specs = [pl.BlockSpec((4, te, D), lambda i: (0, i, 0)),
                pl.BlockSpec((nl, te, D), lambda i: (0, i, 0)),
                pl.BlockSpec(params.shape, lambda i: (0, 0))]
    args = [proj, lora, params]
    if vres:
        in_specs.append(pl.BlockSpec((te, D), lambda i: (i, 0)))
        args.append(v_first)
    return pl.pallas_call(
        functools.partial(_rw_prep_kernel, vres=vres),
        out_shape=(jax.ShapeDtypeStruct((4, lay.M, D), F32),
                   jax.ShapeDtypeStruct((lay.M, D), F32),
                   jax.ShapeDtypeStruct((lay.M, LANE), F32)),
        grid=(lay.nblocks,),
        in_specs=in_specs,
        out_specs=(pl.BlockSpec((4, te, D), lambda i: (0, i, 0)),
                   pl.BlockSpec((te, D), lambda i: (i, 0)),
                   pl.BlockSpec((te, LANE), lambda i: (i, 0))),
        compiler_params=_cparams(("parallel",)),
        name="rw_prep",
    )(*args)


def _wkv_kernel(r_a, r_b, s_a, s_b, v_a, v_b, rep_ref, o_a, o_b, st_ref, vrep_a, vrep_b, *, steps):
    @pl.when(pl.program_id(0) == 0)
    def _():
        st_ref[...] = jnp.zeros(st_ref.shape, F32)

    seqs = ((r_a, s_a, v_a, o_a, vrep_a), (r_b, s_b, v_b, o_b, vrep_b))
    for (_, _, v_ref, _, vrep) in seqs:
        y = v_ref[...].reshape(steps * RW_TILES, LANE)
        hi = y.astype(BF16)
        lo = (y - hi.astype(F32)).astype(BF16)
        for g in range(RW_FOLD):
            yr = (jnp.dot(hi, rep_ref[g], preferred_element_type=F32)
                  + jnp.dot(lo, rep_ref[g], preferred_element_type=F32))
            vrep[:, g * RW_TILES:(g + 1) * RW_TILES, :] = yr.reshape(steps, RW_TILES, LANE)

    lane_group = lax.broadcasted_iota(jnp.int32, (RW_TILES, LANE), 1) // RW_H

    def kk_dot(q, s_ref):
        p = jnp.zeros((RW_N, LANE), F32)
        for j in range(RW_TILES):
            p = p + st_ref[q, j] * s_ref[2, 0:1, j * LANE:(j + 1) * LANE]
        return p

    def step(t, carry):
        t_next = jnp.minimum(t + 1, steps - 1)
        out, raw = [], []
        for q, (r_ref, s_ref, _, o_ref, vrep) in enumerate(seqs):
            sa = -(carry[q] if q == 0 else _head_fold(carry[q]))
            vr = vrep[t]
            w_row = s_ref[0, pl.ds(t, 1), :]
            k_row = s_ref[1, pl.ds(t, 1), :]
            b_row = s_ref[3, pl.ds(t, 1), :]
            r_row = r_ref[pl.ds(t, 1), :]
            kk_row = s_ref[2, pl.ds(t_next, 1), :]
            o = jnp.zeros((RW_N, LANE), F32)
            p = jnp.zeros((RW_N, LANE), F32)
            for j in range(RW_TILES):
                js = slice(j * LANE, (j + 1) * LANE)
                sn = st_ref[q, j] * w_row[:, js] + sa * b_row[:, js] + vr * k_row[:, js]
                st_ref[q, j] = sn
                o = o + sn * r_row[:, js]
                p = p + sn * kk_row[:, js]
            out.append(_head_fold(p) if q == 0 else p)
            emit(q, jnp.maximum(t - 1, 0), carry[2 + q])
            raw.append(o)
        return tuple(out + raw)

    def emit(q, t, o):
        o_ref = seqs[q][3]
        o = _head_fold(o)
        res = o[0:RW_TILES]
        for g in range(1, RW_FOLD):
            res = jnp.where(lane_group == g, o[g * RW_TILES:(g + 1) * RW_TILES], res)
        o_ref[t] = res

    init = tuple(kk_dot(q, s[1]) for q, s in enumerate(seqs))
    zero = jnp.zeros((RW_N, LANE), F32)
    last = lax.fori_loop(0, steps, step, (_head_fold(init[0]), init[1], zero, zero))
    for q in range(2):
        emit(q, steps - 1, last[2 + q])


def wkv_prompt(proj4, stack4, v, T):
    M = v.shape[0]
    tb = _tile(T, 128)
    nc = T // tb
    v3 = v.reshape(M, RW_TILES, LANE)
    lane = np.arange(LANE)
    rep = jnp.asarray(np.stack([(lane[:, None] // RW_H == g) & (lane[:, None] % RW_H == lane[None, :] % RW_H)
                                for g in range(RW_FOLD)]), BF16)
    rspec = lambda s: pl.BlockSpec((None, tb, D), lambda c: (0, s * nc + c, 0))
    sspec = lambda s: pl.BlockSpec((4, tb, D), lambda c: (0, s * nc + c, 0))
    vspec = lambda s: pl.BlockSpec((tb, RW_TILES, LANE), lambda c: (s * nc + c, 0, 0))
    ospec = pl.BlockSpec((tb, RW_TILES, LANE), lambda c: (c, 0, 0))
    return pl.pallas_call(
        functools.partial(_wkv_kernel, steps=tb),
        out_shape=(jax.ShapeDtypeStruct((T, RW_TILES, LANE), F32),
                   jax.ShapeDtypeStruct((T, RW_TILES, LANE), F32),
                   jax.ShapeDtypeStruct((2, RW_TILES, RW_N, LANE), F32)),
        grid=(nc,),
        in_specs=[rspec(0), rspec(1), sspec(0), sspec(1), vspec(0), vspec(1),
                  pl.BlockSpec((RW_FOLD, LANE, LANE), lambda c: (0, 0, 0))],
        out_specs=(ospec, ospec, pl.BlockSpec((2, RW_TILES, RW_N, LANE), lambda c: (0, 0, 0, 0))),
        scratch_shapes=[pltpu.VMEM((tb, RW_N, LANE), F32),
                        pltpu.VMEM((tb, RW_N, LANE), F32)],
        compiler_params=_cparams(("arbitrary",)),
        name="wkv",
    )(proj4, proj4, stack4, stack4, v3, v3, rep)


def _wkv_sample_kernel(x_ref, s0_ref, *rest, steps):
    o_ref, st_ref = rest[-2:]

    def row(v, carry):
        s = s0_ref[v]
        for t in range(steps):
            p = jnp.sum(s * x_ref[3, t], axis=0, keepdims=True)
            vt = x_ref[5, t, pl.ds(v, 1), :]
            s = s * x_ref[1, t] - p * x_ref[4, t] + vt * x_ref[2, t]
            o_ref[t, pl.ds(v, 1), :] = jnp.sum(s * x_ref[0, t], axis=0, keepdims=True)
        st_ref[v] = s
        return carry

    lax.fori_loop(0, RW_N, row, 0, unroll=4)


def wkv_sample(x6, state_t, layer, into=None):
    _, TS, H, _, DB = x6.shape
    L = state_t.shape[0]
    st_spec = pl.BlockSpec((None, None, RW_N, RW_N, DB), lambda h: (layer, h, 0, 0, 0))
    in_specs = [pl.BlockSpec((6, TS, None, RW_N, DB), lambda h: (0, 0, h, 0, 0)), st_spec]
    args = [x6, state_t]
    aliases = {}
    if into is not None:
        aliases = {2: 1}
        in_specs.append(pl.BlockSpec(memory_space=pl.ANY))
        args.append(into)
    return pl.pallas_call(
        functools.partial(_wkv_sample_kernel, steps=TS),
        out_shape=(jax.ShapeDtypeStruct((TS, H, RW_N, DB), F32),
                   jax.ShapeDtypeStruct((L, H, RW_N, RW_N, DB), F32)),
        grid=(H,),
        in_specs=in_specs,
        out_specs=(pl.BlockSpec((TS, None, RW_N, DB), lambda h: (0, h, 0, 0)), st_spec),
        input_output_aliases=aliases,
        compiler_params=_cparams(("parallel",)),
        name="wkv_sample",
    )(*args)


def _rw_post_kernel(oa_ref, ob_ref, os_ref, v_ref, rk_ref, z_ref, ln_ref, y_ref, o_scr, *, n_seq_blocks):
    i = pl.program_id(0)
    rows = y_ref.shape[0]
    in_a = i < n_seq_blocks
    in_b = jnp.logical_and(i >= n_seq_blocks, i < 2 * n_seq_blocks)
    s = jnp.zeros((rows, LANE), F32)
    for j in range(RW_TILES):
        js = slice(j * LANE, (j + 1) * LANE)
        oj = jnp.where(in_a, oa_ref[:, j, :], jnp.where(in_b, ob_ref[:, j, :], os_ref[:, js]))
        o_scr[:, js] = oj
        s = s + oj
    mean = _head_fold(s) * (1.0 / RW_N)
    s2 = jnp.zeros((rows, LANE), F32)
    for j in range(RW_TILES):
        d = o_scr[:, j * LANE:(j + 1) * LANE] - mean
        s2 = s2 + d * d
    rstd = lax.rsqrt(_head_fold(s2) * (1.0 / RW_N) + RW_GN_EPS)
    rk = rk_ref[...]
    for j in range(RW_TILES):
        js = slice(j * LANE, (j + 1) * LANE)
        o = (o_scr[:, js] - mean) * rstd * ln_ref[0:1, js] + ln_ref[1:2, js]
        o = o + rk * v_ref[:, js]
        y_ref[:, js] = (o * _silu(z_ref[:, js])).astype(BF16)


def rw_post(lay, o_a, o_b, o_s, v, rk, proj4, ln):
    te = lay.te
    row = pl.BlockSpec((te, D), lambda i: (i, 0))
    nsb = lay.blocks_per_seq
    npb = lay.n_p_blocks
    a_idx = lambda i: jnp.where(i < nsb, i, 0)
    b_idx = lambda i: jnp.where(jnp.logical_and(i >= nsb, i < 2 * nsb), i - nsb, 0)
    s_idx = lambda i: jnp.maximum(i - npb, 0)
    return pl.pallas_call(
        functools.partial(_rw_post_kernel, n_seq_blocks=nsb),
        out_shape=jax.ShapeDtypeStruct((lay.M, D), BF16),
        grid=(lay.nblocks,),
        in_specs=[pl.BlockSpec((te, RW_TILES, LANE), lambda i: (a_idx(i), 0, 0)),
                  pl.BlockSpec((te, RW_TILES, LANE), lambda i: (b_idx(i), 0, 0)),
                  pl.BlockSpec((te, D), lambda i: (s_idx(i), 0)),
                  row,
                  pl.BlockSpec((te, LANE), lambda i: (i, 0)),
                  pl.BlockSpec((None, te, D), lambda i: (2, i, 0)),
                  pl.BlockSpec((2, D), lambda i: (0, 0))],
        out_specs=row,
        scratch_shapes=[pltpu.VMEM((te, D), F32)],
        compiler_params=_cparams(("parallel",)),
        name="rw_post",
    )(o_a, o_b, o_s, v, rk, proj4, ln)


def _head_minor(a):
    lead = a.shape[:-1]
    return a.reshape(lead + (RW_H, RW_N)).swapaxes(-1, -2).reshape(lead + (D,))


def _tiles_to_state(s):
    n = s.shape[0]
    s = s.reshape(n, RW_TILES, RW_FOLD, RW_TILES, RW_FOLD, RW_H)
    return s.transpose(0, 5, 3, 2, 1, 4).reshape(n, RW_H, RW_N, RW_N)


def _pad_cols(w, n):
    return jnp.pad(w, ((0, 0), (0, n - w.shape[1])))


def _pad_rows(w, n):
    return jnp.pad(w, ((0, n - w.shape[0]), (0, 0)))


def rwkv_layer(lay, h, shift_state, state_t, layer, s_states, v_first, w):
    B, T, DB, TS, Mp, M = lay.B, lay.T, lay.DB, lay.TS, lay.Mp, lay.M
    x6 = rw_mix(lay, h, jnp.repeat(shift_state, TS, axis=0), w["mu"])
    w_in = jnp.stack([_head_minor(w["w_in"][c]).astype(BF16) for c in (0, 1, 3, 2)])
    proj4 = matmul(x6, w_in, name="rw_proj")
    vres = v_first is not None
    l1 = [w["v1"]] if vres else []
    l1 += [w["w1"], w["a1"]]
    l2 = [w["v2"]] if vres else []
    l2 += [w["w2"], w["a2"]]
    l1 = jnp.stack([_pad_cols(a, RW_LORA_PAD) for a in l1]).astype(BF16)
    l2 = jnp.stack([_head_minor(_pad_rows(a, RW_LORA_PAD)) for a in l2]).astype(BF16)
    lo1 = matmul(x6, l1, x_off=3 if vres else 4, act_out="tanh_group",
                 tanh_group=1 if vres else 0, out_dtype=BF16, name="rw_lora1")
    lora = matmul(lo1, l2, name="rw_lora2")
    plist = [w["w0"], w["a0"], w["k_k"], w["k_a"], w["r_k"].reshape(-1)]
    if vres:
        plist.append(w["v0"])
    plist += [jnp.zeros((D,), F32)] * (8 - len(plist))
    params = _head_minor(jnp.stack(plist))
    stack4, v, rk = rw_prep(lay, proj4, lora, params, v_first)
    v_first_out = v_first if vres else v

    assert B == 2
    o_a, o_b, p_st = wkv_prompt(proj4, stack4, v, T)
    xs = jnp.concatenate([proj4[0:1, Mp:], stack4[:, Mp:], v[None, Mp:]], axis=0)
    xs = xs.reshape(6, DB, TS, RW_N, RW_H).transpose(0, 2, 4, 3, 1)
    o_s, s_states = wkv_sample(xs, state_t, layer, s_states)
    o_s = o_s.transpose(3, 0, 2, 1).reshape(lay.Ms, D)

    ln = _head_minor(jnp.stack([w["ln_w"], w["ln_b"]]))
    y = rw_post(lay, o_a, o_b, o_s, v, rk, proj4, ln)
    w_out = w["w_out"].reshape(RW_H, RW_N, D).swapaxes(0, 1).reshape(D, D)
    o_proj = matmul2d(y, w_out.astype(BF16), name="rw_out")
    p_shift = h[T - 1:Mp:T]
    s_shift = h[Mp:].reshape(DB, TS, D)[:, -1]
    return o_proj, _tiles_to_state(p_st), p_shift, s_states, s_shift, v_first_out


def _rot_cols(w):
    half = MLA_ROPE // 2
    return jnp.concatenate([-w[..., half:], w[..., :half]], axis=-1)


def _mla_prep_kernel(pq_ref, pkv_ref, pkr_ref, gq_ref, gkv_ref, cs_ref,
                     cq_ref, ckv_ref, kr_ref, krb_ref):
    cq_ref[...] = _rms(pq_ref[...], gq_ref[...]).astype(BF16)
    ckv_ref[...] = _rms(pkv_ref[...], gkv_ref[...])
    t2 = pkr_ref[...]
    kr = t2 * cs_ref[0] + pltpu.roll(t2, MLA_ROPE, 1) * cs_ref[1]
    kr_ref[...] = kr[:, :MLA_ROPE]
    krb_ref[...] = kr.astype(BF16)


def mla_prep(lay, proj, g_q, g_kv, cs):
    te = lay.te
    M = lay.M
    return pl.pallas_call(
        _mla_prep_kernel,
        out_shape=(jax.ShapeDtypeStruct((M, MLA_QL), BF16),
                   jax.ShapeDtypeStruct((M, MLA_KVL), F32),
                   jax.ShapeDtypeStruct((M, MLA_ROPE), F32),
                   jax.ShapeDtypeStruct((M, LANE), BF16)),
        grid=(lay.nblocks,),
        in_specs=[pl.BlockSpec((te, MLA_QL), lambda i: (i, D // MLA_QL)),
                  pl.BlockSpec((te, MLA_KVL), lambda i: (i, D // MLA_KVL + 1)),
                  pl.BlockSpec((te, LANE), lambda i: (i, (D + MLA_QL + MLA_KVL) // LANE)),
                  pl.BlockSpec((1, MLA_QL), lambda i: (0, 0)),
                  pl.BlockSpec((1, MLA_KVL), lambda i: (0, 0)),
                  pl.BlockSpec((2, te, LANE), lambda i: (0, i, 0))],
        out_specs=(pl.BlockSpec((te, MLA_QL), lambda i: (i, 0)),
                   pl.BlockSpec((te, MLA_KVL), lambda i: (i, 0)),
                   pl.BlockSpec((te, MLA_ROPE), lambda i: (i, 0)),
                   pl.BlockSpec((te, LANE), lambda i: (i, 0))),
        compiler_params=_cparams(("parallel",)),
        name="mla_prep",
    )(proj, proj, proj, g_q[None], g_kv[None], cs)


def _mla_q_kernel(q_ref, cs_ref, qn_ref, qr_ref):
    for h in range(MLA_H):
        qn_ref[:, h * LANE:(h + 1) * LANE] = q_ref[:, 2 * h * LANE:(2 * h + 1) * LANE].astype(BF16)
        t2 = q_ref[:, (2 * h + 1) * LANE:(2 * h + 2) * LANE]
        qr = t2 * cs_ref[0] + pltpu.roll(t2, MLA_ROPE, 1) * cs_ref[1]
        qr_ref[:, h * LANE:(h + 1) * LANE] = qr.astype(BF16)


def mla_q(lay, q, cs):
    te = lay.te
    M = lay.M
    return pl.pallas_call(
        _mla_q_kernel,
        out_shape=(jax.ShapeDtypeStruct((M, MLA_H * LANE), BF16),
                   jax.ShapeDtypeStruct((M, MLA_H * LANE), BF16)),
        grid=(lay.nblocks,),
        in_specs=[pl.BlockSpec((te, MLA_H * 2 * LANE), lambda i: (i, 0)),
                  pl.BlockSpec((2, te, LANE), lambda i: (0, i, 0))],
        out_specs=(pl.BlockSpec((te, MLA_H * LANE), lambda i: (i, 0)),
                   pl.BlockSpec((te, MLA_H * LANE), lambda i: (i, 0))),
        compiler_params=_cparams(("parallel",)),
        name="mla_q",
    )(q, cs)


def _flash_kernel(qn_ref, qr_ref, kn_ref, kr_ref, v_ref, o_ref, *, tq):
    qi = pl.program_id(2)
    q = jnp.concatenate([qn_ref[...], qr_ref[...]], axis=1)
    nt = (((1,), (1,)), ((), ()))
    c = MLA_SCALE * math.log2(math.e)

    def update(start, carry, diagonal):
        m, l, acc = carry
        k = jnp.concatenate([kn_ref[pl.ds(start, tq), :], kr_ref[pl.ds(start, tq), :]], axis=1)
        s = lax.dot_general(q, k, nt, preferred_element_type=F32) * c
        if diagonal:
            s = jnp.where(lax.broadcasted_iota(jnp.int32, (tq, tq), 1)
                          <= lax.broadcasted_iota(jnp.int32, (tq, tq), 0), s, -1e30)
        m_new = jnp.maximum(m, jnp.max(s, axis=1, keepdims=True))
        p = jnp.exp2(s - m_new)
        alpha = jnp.exp2(m - m_new)
        l = alpha * l + jnp.sum(p, axis=1, keepdims=True)
        acc = alpha * acc + jnp.dot(p.astype(BF16), v_ref[pl.ds(start, tq), :],
                                    preferred_element_type=F32)
        return m_new, l, acc

    def body(ki, carry):
        return update(pl.multiple_of(ki * tq, tq), carry, False)

    init = (jnp.full((tq, 1), -1e30, F32), jnp.zeros((tq, 1), F32), jnp.zeros((tq, MLA_V), F32))
    carry = lax.fori_loop(0, qi, body, init)
    m, l, acc = update(pl.multiple_of(qi * tq, tq), carry, True)
    o_ref[...] = acc / l


def mla_flash(qn, qr, knv, krb, B, T):
    tq = _tile(T, 512)
    nq = T // tq
    return pl.pallas_call(
        functools.partial(_flash_kernel, tq=tq),
        out_shape=jax.ShapeDtypeStruct((B * T, MLA_H * MLA_V), F32),
        grid=(B, MLA_H, nq),
        in_specs=[pl.BlockSpec((tq, LANE), lambda b, h, i: (b * nq + i, h)),
                  pl.BlockSpec((tq, LANE), lambda b, h, i: (b * nq + i, h)),
                  pl.BlockSpec((T, LANE), lambda b, h, i: (b, h)),
                  pl.BlockSpec((T, LANE), lambda b, h, i: (b, 0)),
                  pl.BlockSpec((T, LANE), lambda b, h, i: (b, MLA_H + h))],
        out_specs=pl.BlockSpec((tq, MLA_V), lambda b, h, i: (b * nq + i, h)),
        compiler_params=_cparams(("parallel", "parallel", "arbitrary")),
        name="mla_flash",
    )(qn, qr, knv, krb, knv)


def _paged_kernel(pt_ref, ql_ref, qr_ref, cn_ref, kn_ref, *rest, pg, ts):
    ck_refs = rest[:pg]
    kr_refs = rest[pg:2 * pg]
    o_ref, m_ref, l_ref, acc_ref = rest[2 * pg:]
    p = pl.program_id(1)
    nt = (((1,), (1,)), ((), ()))

    @pl.when(p == 0)
    def _():
        m_ref[...] = jnp.full(m_ref.shape, -1e30, F32)
        l_ref[...] = jnp.zeros(l_ref.shape, F32)
        acc_ref[...] = jnp.zeros(acc_ref.shape, F32)

    rows = ts * MLA_H
    ql = ql_ref[...]
    qr = qr_ref[...][:, :MLA_ROPE]
    cks = [r[...].astype(BF16) for r in ck_refs]
    ck = jnp.concatenate(cks, axis=0)
    kr_t = jnp.concatenate([r[...].astype(BF16) for r in kr_refs], axis=1)
    s = jnp.concatenate([lax.dot_general(ql, c, nt, preferred_element_type=F32) for c in cks], axis=1)
    s = (s + jnp.dot(qr, kr_t, preferred_element_type=F32)) * MLA_SCALE
    m = m_ref[...]
    m_new = jnp.maximum(m, jnp.max(s, axis=1, keepdims=True))
    pr = jnp.exp(s - m_new)
    alpha = jnp.exp(m - m_new)
    l = alpha * l_ref[...] + jnp.sum(pr, axis=1, keepdims=True)
    acc = alpha * acc_ref[...] + jnp.dot(pr.astype(BF16), ck, preferred_element_type=F32)
    m = m_new
    m_ref[...] = m
    l_ref[...] = l
    acc_ref[...] = acc

    @pl.when(p == pl.num_programs(1) - 1)
    def _():
        qlf = ql.astype(F32)
        qrf = qr.astype(F32)
        tok = lax.broadcasted_iota(jnp.int32, (ts * MLA_H, 1), 0) // MLA_H
        cols = []
        for j in range(ts):
            cn = cn_ref[j:j + 1, :].astype(BF16).astype(F32)
            kn = kn_ref[j:j + 1, :MLA_ROPE].astype(F32)
            sj = (jnp.sum(qlf * cn, axis=1, keepdims=True)
                  + jnp.sum(qrf * kn, axis=1, keepdims=True)) * MLA_SCALE
            cols.append(jnp.where(tok >= j, sj, -1e30))
        m2 = m
        for sj in cols:
            m2 = jnp.maximum(m2, sj)
        alpha = jnp.exp(m - m2)
        l2 = alpha * l
        acc2 = alpha * acc
        for j, sj in enumerate(cols):
            pj = jnp.exp(sj - m2)
            l2 = l2 + pj
            acc2 = acc2 + pj.astype(BF16).astype(F32) * cn_ref[j:j + 1, :].astype(BF16).astype(F32)
        o_ref[...] = (acc2 / l2).astype(o_ref.dtype)


def mla_paged(page_table, ql, qr, ckv_new, krb_new, cache_ckv, cache_kr, layer, DB, TS):
    n_pages = page_table.shape[1]
    pg = _tile(n_pages, 32) if n_pages >= 8 else n_pages
    rows = TS * MLA_H
    pt = page_table.reshape(-1)

    def page_spec(i, shape):
        return pl.BlockSpec((None, None) + shape,
                            lambda b, p, pt_ref: (layer, pt_ref[b * n_pages + p * pg + i], 0, 0))

    cache_kr_t = jnp.swapaxes(cache_kr, 2, 3)
    per_b = lambda width: pl.BlockSpec((None, rows, width), lambda b, p, pt_ref: (b, 0, 0))
    new_b = lambda width: pl.BlockSpec((None, TS, width), lambda b, p, pt_ref: (b, 0, 0))
    grid_spec = pltpu.PrefetchScalarGridSpec(
        num_scalar_prefetch=1,
        grid=(DB, n_pages // pg),
        in_specs=[per_b(MLA_KVL), per_b(LANE), new_b(MLA_KVL), new_b(LANE)]
                 + [page_spec(i, (PAGE, MLA_KVL)) for i in range(pg)]
                 + [page_spec(i, (MLA_ROPE, PAGE)) for i in range(pg)],
        out_specs=per_b(MLA_KVL),
        scratch_shapes=[pltpu.VMEM((rows, 1), F32), pltpu.VMEM((rows, 1), F32),
                        pltpu.VMEM((rows, MLA_KVL), F32)],
    )
    return pl.pallas_call(
        functools.partial(_paged_kernel, pg=pg, ts=TS),
        out_shape=jax.ShapeDtypeStruct((DB, rows, MLA_KVL), BF16),
        grid_spec=grid_spec,
        compiler_params=_cparams(("parallel", "arbitrary")),
        name="mla_paged",
    )(pt, ql, qr, ckv_new, krb_new, *([cache_ckv] * pg), *([cache_kr_t] * pg))


def _gate_kernel(o_ref, z_ref, y_ref):
    y_ref[...] = (o_ref[...] * _silu(z_ref[...])).astype(BF16)


def gate_mul(lay, o, zsrc, zcol):
    te = lay.te
    return pl.pallas_call(
        _gate_kernel,
        out_shape=jax.ShapeDtypeStruct((lay.M, D), BF16),
        grid=(lay.nblocks,),
        in_specs=[pl.BlockSpec((te, D), lambda i: (i, 0)),
                  pl.BlockSpec((te, D), lambda i: (i, zcol))],
        out_specs=pl.BlockSpec((te, D), lambda i: (i, 0)),
        compiler_params=_cparams(("parallel",)),
        name="gate_mul",
    )(o, zsrc)


def mla_layer(lay, h, cache_ckv, cache_kr, layer, page_table, w):
    B, T, DB, TS, Mp, M = lay.B, lay.T, lay.DB, lay.TS, lay.Mp, lay.M
    w_in = w["w_in"]
    w_kr = w_in[:, 2 * MLA_QL:2 * MLA_QL + MLA_ROPE]
    w1 = jnp.concatenate([w_in[:, 2 * MLA_QL + MLA_ROPE:], w_in[:, :2 * MLA_QL],
                          w_kr, _rot_cols(w_kr)], axis=1)
    proj = matmul2d(h, w1.astype(BF16), tn=640, name="mla_proj")
    pos = jnp.concatenate([jnp.tile(jnp.arange(T, dtype=F32), B),
                           jnp.tile(page_table.shape[1] * PAGE + jnp.arange(TS, dtype=F32), DB)])
    half = MLA_ROPE // 2
    inv_freq = ROPE_THETA ** (-jnp.arange(half, dtype=F32) / half)
    ang = pos[:, None] * inv_freq[None, :]
    zeros = jnp.zeros((M, MLA_ROPE), F32)
    cs = jnp.stack([jnp.concatenate([jnp.cos(ang), jnp.cos(ang), zeros], axis=1),
                    jnp.concatenate([jnp.sin(ang), jnp.sin(ang), zeros], axis=1)])
    cq, ckv, kr, krb = mla_prep(lay, proj, w["q_norm"], w["kv_norm"], cs)
    w_uq = w["w_uq"]
    wq = jnp.concatenate([w_uq, _rot_cols(w_uq[..., MLA_NOPE:])], axis=-1)
    q = matmul2d(cq, wq.reshape(MLA_QL, MLA_H * 2 * LANE).astype(BF16), name="mla_qproj")
    qn, qr = mla_q(lay, q, cs)
    w_kv = jnp.concatenate([w["w_uk"].reshape(MLA_KVL, -1), w["w_uv"].reshape(MLA_KVL, -1)], axis=1)
    knv = matmul2d(ckv[:Mp], w_kv.astype(BF16), out_dtype=BF16, name="mla_kv")
    o_p = mla_flash(qn, qr, knv, krb, B, T)
    w_ukT = w["w_uk"].transpose(1, 2, 0).astype(BF16)
    ql = head_matmul(qn[Mp:], w_ukT, BF16, "mla_qlat")
    o_lat = mla_paged(page_table, ql.reshape(DB, TS * MLA_H, MLA_KVL),
                      qr[Mp:].reshape(DB, TS * MLA_H, LANE),
                      ckv[Mp:].reshape(DB, TS, MLA_KVL), krb[Mp:].reshape(DB, TS, LANE),
                      cache_ckv, cache_kr, layer, DB, TS)
    o_s = head_matmul(o_lat.reshape(lay.Ms, MLA_H * MLA_KVL),
                      w["w_uv"].transpose(1, 0, 2).astype(BF16), F32, "mla_ouv")
    o = jnp.concatenate([o_p, o_s], axis=0)
    y = gate_mul(lay, o, proj, 0)
    o_proj = matmul2d(y, w["w_out"].astype(BF16), name="mla_out")
    return (o_proj, ckv[:Mp].reshape(B, T, MLA_KVL), kr[:Mp].reshape(B, T, MLA_ROPE),
            ckv[Mp:].reshape(DB, TS, MLA_KVL), kr[Mp:].reshape(DB, TS, MLA_ROPE))


def _gla_kernel(q_ref, k_ref, v_ref, g_ref, s0_ref, o_ref, sT_ref, st_ref, *, L):
    c = pl.program_id(2)
    C = GLA_SUB
    assert L % C == 0

    @pl.when(c == 0)
    def _():
        st_ref[...] = s0_ref[...].T

    q = q_ref[...] * (GLA_DK ** -0.5)
    k = k_ref[...]
    v = v_ref[...].astype(BF16)
    trow = lax.broadcasted_iota(jnp.int32, (L, 1), 0)
    b = g_ref[...]
    sh = 1
    while sh < L:
        b = b + jnp.where(trow >= sh, pltpu.roll(b, sh, 0), 0.0)
        sh *= 2
    st = st_ref[...]
    nt = (((1,), (1,)), ((), ()))
    tn = (((0,), (0,)), ((), ()))
    o = lax.dot_general((q * jnp.exp(b)).astype(BF16), st.astype(BF16), nt, preferred_element_type=F32)
    crow = lax.broadcasted_iota(jnp.int32, (C, 1), 0)
    col = lax.broadcasted_iota(jnp.int32, (C, L), 1)
    att_rows = []
    for i in range(L // C):
        lo = i * C
        qi, ki, bi = q[lo:lo + C], k[lo:lo + C], b[lo:lo + C]
        if i == 0:
            att_i = jnp.zeros((C, L), F32)
        else:
            beta = b[lo - 1:lo, :]
            q_in = (qi * jnp.exp(bi - beta)).astype(BF16)
            k_out = (k * jnp.exp(jnp.where(trow < lo, beta - b, -jnp.inf))).astype(BF16)
            att_i = lax.dot_general(q_in, k_out, nt, preferred_element_type=F32)
        for s in range(C):
            e = jnp.where(crow >= s, bi - bi[s:s + 1, :], -jnp.inf)
            a_s = jnp.sum(qi * ki[s:s + 1, :] * jnp.exp(e), axis=1, keepdims=True)
            att_i = jnp.where(col == lo + s, a_s, att_i)
        att_rows.append(att_i)
    att = att_rows[0] if len(att_rows) == 1 else jnp.concatenate(att_rows, axis=0)
    o = o + jnp.dot(att.astype(BF16), v, preferred_element_type=F32)
    o_ref[...] = o
    b_end = b[L - 1:L, :]
    kd = (k * jnp.exp(b_end - b)).astype(BF16)
    st = st * jnp.exp(b_end) + lax.dot_general(v, kd, tn, preferred_element_type=F32)
    st_ref[...] = st

    @pl.when(c == pl.num_programs(2) - 1)
    def _():
        sT_ref[...] = st.T


def gla_scan(q, k, v, g, s0, *, nseq, nchunk, L, row0, qcol, kcol, vcol):
    rows = lambda n, h, c: row0 + n * nchunk + c
    return pl.pallas_call(
        functools.partial(_gla_kernel, L=L),
        out_shape=(jax.ShapeDtypeStruct((q.shape[0], GLA_H * GLA_DV), F32),
                   jax.ShapeDtypeStruct((nseq, GLA_H, GLA_DK, GLA_DV), F32)),
        grid=(nseq, GLA_H, nchunk),
        in_specs=[pl.BlockSpec((L, GLA_DK), lambda n, h, c: (rows(n, h, c), qcol + h)),
                  pl.BlockSpec((L, GLA_DK), lambda n, h, c: (rows(n, h, c), kcol + h)),
                  pl.BlockSpec((L, GLA_DV), lambda n, h, c: (rows(n, h, c), vcol + h)),
                  pl.BlockSpec((L, GLA_DK), lambda n, h, c: (rows(n, h, c), h)),
                  pl.BlockSpec((None, None, GLA_DK, GLA_DV), lambda n, h, c: (n, h, 0, 0))],
        out_specs=(pl.BlockSpec((L, GLA_DV), lambda n, h, c: (rows(n, h, c), h)),
                   pl.BlockSpec((None, None, GLA_DK, GLA_DV), lambda n, h, c: (n, h, 0, 0))),
        scratch_shapes=[pltpu.VMEM((GLA_DV, GLA_DK), F32)],
        compiler_params=_cparams(("parallel", "parallel", "arbitrary")),
        name="gla_scan",
    )(q, k, v, g, s0)


def _gla_post_kernel(o_ref, z_ref, g_ref, y_ref):
    for h in range(GLA_H):
        hs = slice(h * GLA_DV, (h + 1) * GLA_DV)
        y_ref[:, hs] = (_rms(o_ref[:, hs], g_ref[...]) * _silu(z_ref[:, hs])).astype(BF16)


def gla_post(lay, o, proj, g_norm):
    te = lay.te
    return pl.pallas_call(
        _gla_post_kernel,
        out_shape=jax.ShapeDtypeStruct((lay.M, D), BF16),
        grid=(lay.nblocks,),
        in_specs=[pl.BlockSpec((te, D), lambda i: (i, 0)),
                  pl.BlockSpec((te, D), lambda i: (i, 2)),
                  pl.BlockSpec((1, GLA_DV), lambda i: (0, 0))],
        out_specs=pl.BlockSpec((te, D), lambda i: (i, 0)),
        compiler_params=_cparams(("parallel",)),
        name="gla_post",
    )(o, proj, g_norm[None])


def gla_layer(lay, h, state, w):
    B, T, DB, TS, Mp, M = lay.B, lay.T, lay.DB, lay.TS, lay.Mp, lay.M
    w_in = _pad_cols(w["w_in"], 2 * GLA_QK + 2 * D + GLA_LORA_PAD)
    proj = matmul2d(h, w_in.astype(BF16), tn=640, name="gla_proj")
    gl = proj[:, 2 * GLA_QK + 2 * D:]
    g = matmul2d(gl, _pad_rows(w["w_g2"], GLA_LORA_PAD).astype(BF16), w["b_g"],
                 act_out="logsig_tau", name="gla_gate")
    L = math.gcd(T, GLA_CHUNK)
    nk = GLA_QK // GLA_DK
    o_p, p_state = gla_scan(proj, proj, proj, g, jnp.zeros((B, GLA_H, GLA_DK, GLA_DV), F32),
                            nseq=B, nchunk=T // L, L=L, row0=0,
                            qcol=0, kcol=nk, vcol=2 * GLA_QK // GLA_DV)
    LS = GLA_SUB
    pad = lambda a: jnp.pad(a.reshape(DB, TS, -1), ((0, 0), (0, LS - TS), (0, 0))).reshape(DB * LS, -1)
    ps = pad(proj[Mp:, :2 * GLA_QK + D])
    o_s, s_state = gla_scan(ps, ps, ps, pad(g[Mp:]), state, nseq=DB, nchunk=1, L=LS, row0=0,
                            qcol=0, kcol=nk, vcol=2 * GLA_QK // GLA_DV)
    o_s = o_s.reshape(DB, LS, D)[:, :TS].reshape(lay.Ms, D)
    o = jnp.concatenate([o_p[:Mp], o_s], axis=0)
    y = gla_post(lay, o, proj, w["norm"])
    o_proj = matmul2d(y, w["w_out"].astype(BF16), name="gla_out")
    return o_proj, p_state, s_state


def kernel(x_prompt, x_sample, c_prompt, c_sample, state_rwkv_wkv, state_rwkv_shift, cache_mla_ckv, cache_mla_krope, page_table, state_gla, norm_pre, norm_post, ada_w, ada_b, rw_mu, rw_w_in, rw_w0, rw_w1, rw_w2, rw_a0, rw_a1, rw_a2, rw_v0, rw_v1, rw_v2, rw_k_k, rw_k_a, rw_r_k, rw_ln_w, rw_ln_b, rw_w_out, mla_w_in, mla_q_norm, mla_kv_norm, mla_w_uq, mla_w_uk, mla_w_uv, mla_w_out, gla_w_in, gla_w_g2, gla_b_g, gla_norm, gla_w_out):
    B, T, _ = x_prompt.shape
    DB, TS, _ = x_sample.shape
    depth = norm_pre.shape[0]
    lay = RowLayout(B, T, DB, TS)
    Mp = lay.Mp
    x = jnp.concatenate([x_prompt.reshape(Mp, D), x_sample.reshape(lay.Ms, D)], axis=0)
    c = jnp.concatenate([c_prompt, c_sample], axis=0)[None]
    c = jnp.broadcast_to(c, (depth,) + c.shape[1:])
    mods = matmul(c, ada_w, ada_b[:, None, :], act_in="silu", name="ada")

    p_wkv, p_shift, p_ckv, p_kr, p_gla = [], [], [], [], []
    s_shift, s_ckv, s_kr, s_gla = [], [], [], []
    v_first = None
    state_t = jnp.transpose(state_rwkv_wkv, (0, 2, 3, 4, 1))
    s_wkv_t = None
    for i in range(depth):
        kind, j = i % 3, i // 3
        mod = lay.expand_mod(mods[i])
        h = prenorm(lay, x, norm_pre[i], mod, F32 if kind == 0 else BF16)
        if kind == 0:
            w = dict(mu=rw_mu[j], w_in=rw_w_in[j], w0=rw_w0[j], w1=rw_w1[j], w2=rw_w2[j],
                     a0=rw_a0[j], a1=rw_a1[j], a2=rw_a2[j], k_k=rw_k_k[j], k_a=rw_k_a[j],
                     r_k=rw_r_k[j], ln_w=rw_ln_w[j], ln_b=rw_ln_b[j], w_out=rw_w_out[j])
            if j > 0:
                w.update(v0=rw_v0[j - 1], v1=rw_v1[j - 1], v2=rw_v2[j - 1])
            o, pst, psh, s_wkv_t, ssh, v_first = rwkv_layer(
                lay, h, state_rwkv_shift[j], state_t, j, s_wkv_t, v_first if j > 0 else None, w)
            p_wkv.append(pst)
            p_shift.append(psh)
            s_shift.append(ssh)
        elif kind == 1:
            w = dict(w_in=mla_w_in[j], q_norm=mla_q_norm[j], kv_norm=mla_kv_norm[j],
                     w_uq=mla_w_uq[j], w_uk=mla_w_uk[j], w_uv=mla_w_uv[j], w_out=mla_w_out[j])
            o, pc, pk, sc, sk = mla_layer(lay, h, cache_mla_ckv, cache_mla_krope, j, page_table, w)
            p_ckv.append(pc)
            p_kr.append(pk)
            s_ckv.append(sc)
            s_kr.append(sk)
        else:
            w = dict(w_in=gla_w_in[j], w_g2=gla_w_g2[j], b_g=gla_b_g[j], norm=gla_norm[j],
                     w_out=gla_w_out[j])
            o, pg, sg = gla_layer(lay, h, state_gla[j], w)
            p_gla.append(pg)
            s_gla.append(sg)
        x = residual(lay, x, o, norm_post[i], mod)
    yp = x[:Mp].reshape(B, T, D)
    ys = x[Mp:].reshape(DB, TS, D)
    return (yp, ys,
            jnp.stack(p_wkv), jnp.stack(p_shift), jnp.stack(p_ckv), jnp.stack(p_kr), jnp.stack(p_gla),
            jnp.transpose(s_wkv_t, (0, 4, 1, 2, 3)), jnp.stack(s_shift), jnp.stack(s_ckv),
            jnp.stack(s_kr), jnp.stack(s_gla))
```

```python
import functools
import math

import jax
import jax.numpy as jnp
import numpy as np
from jax import lax
from jax.experimental import pallas as pl
from jax.experimental.pallas import tpu as pltpu

F32 = jnp.float32
BF16 = jnp.bfloat16

D = 2048
NORM_EPS = 1e-6
RW_N = 64
RW_H = D // RW_N
RW_LORA_PAD = 128
RW_GN_EPS = 64e-5
RW_TILES = D // 128
RW_FOLD = 128 // RW_H
EXP_M05 = math.exp(-0.5)
MLA_H = 16
MLA_NOPE = 128
MLA_ROPE = 64
MLA_V = 128
MLA_QL = 512
MLA_KVL = 512
MLA_SCALE = (MLA_NOPE + MLA_ROPE) ** -0.5
ROPE_THETA = 10000.0
PAGE = 128
GLA_H = 4
GLA_DK = 256
GLA_DV = 512
GLA_QK = GLA_H * GLA_DK
GLA_LORA_PAD = 256
GLA_TAU = 16.0
GLA_CHUNK = 64
GLA_SUB = 16

LANE = 128
ROW_TILE = 128
VMEM_LIMIT = 48 * 1024 * 1024


def _cparams(sem):
    return pltpu.CompilerParams(dimension_semantics=sem, vmem_limit_bytes=VMEM_LIMIT)


def _tile(n, pref):
    for t in (1024, 640, 512, 384, 256, 128, 64, 32, 16, 8):
        if t <= pref and n % t == 0:
            return t
    return n


def _sigmoid(x):
    return 1.0 / (1.0 + jnp.exp(-x))


def _silu(x):
    return x * _sigmoid(x)


def _mm_kernel(*refs, has_bias, act_in, act_out, tanh_group):
    if has_bias:
        x_ref, w_ref, b_ref, o_ref = refs
    else:
        x_ref, w_ref, o_ref = refs
    x = x_ref[...]
    if act_in == "silu":
        x = _silu(x.astype(F32))
    acc = jnp.dot(x.astype(BF16), w_ref[...].astype(BF16), preferred_element_type=F32)
    if has_bias:
        acc = acc + b_ref[...]
    if act_out == "tanh_group":
        acc = jnp.where(pl.program_id(0) == tanh_group, jnp.tanh(acc), acc)
    elif act_out == "logsig_tau":
        acc = (jnp.minimum(acc, 0.0) - jnp.log(1.0 + jnp.exp(-jnp.abs(acc)))) * (1.0 / GLA_TAU)
    o_ref[...] = acc.astype(o_ref.dtype)


MM_X_TILE_BYTES = 9 * 1024 * 1024
MM_MAX_ROWS = 2304


def _row_tile(M, row_bytes):
    for parts in range(1, M // 8 + 1):
        rows = M // parts
        if M % parts == 0 and rows % 8 == 0 and rows <= MM_MAX_ROWS and rows * row_bytes <= MM_X_TILE_BYTES:
            return rows
    return M


def matmul(x, w, bias=None, *, x_off=0, act_in=None, act_out=None, tanh_group=0,
           out_dtype=F32, tn=512, name="mm"):
    G, K, N = w.shape
    M = x.shape[1]
    tm = _row_tile(M, K * x.dtype.itemsize)
    tn = _tile(N, tn)
    in_specs = [
        pl.BlockSpec((None, tm, K), lambda g, i, j: (g + x_off, i, 0)),
        pl.BlockSpec((None, K, tn), lambda g, i, j: (g, 0, j)),
    ]
    args = [x, w]
    if bias is not None:
        in_specs.append(pl.BlockSpec((None, 1, tn), lambda g, i, j: (g, 0, j)))
        args.append(bias)
    return pl.pallas_call(
        functools.partial(_mm_kernel, has_bias=bias is not None, act_in=act_in,
                          act_out=act_out, tanh_group=tanh_group),
        out_shape=jax.ShapeDtypeStruct((G, M, N), out_dtype),
        grid=(G, M // tm, N // tn),
        in_specs=in_specs,
        out_specs=pl.BlockSpec((None, tm, tn), lambda g, i, j: (g, i, j)),
        compiler_params=_cparams(("parallel", "parallel", "arbitrary")),
        name=name,
    )(*args)


def matmul2d(x, w, bias=None, **kw):
    b3 = None if bias is None else bias[None, None, :]
    return matmul(x[None], w[None], b3, **kw)[0]


def _hmm_kernel(x_ref, w_ref, o_ref):
    o_ref[...] = jnp.dot(x_ref[...].astype(BF16), w_ref[...],
                         preferred_element_type=F32).astype(o_ref.dtype)


def head_matmul(x, w, out_dtype, name):
    H, Kh, Nh = w.shape
    M = x.shape[0]
    tm = _tile(M, 512)
    return pl.pallas_call(
        _hmm_kernel,
        out_shape=jax.ShapeDtypeStruct((M, H * Nh), out_dtype),
        grid=(H, M // tm),
        in_specs=[pl.BlockSpec((tm, Kh), lambda h, i: (i, h)),
                  pl.BlockSpec((None, Kh, Nh), lambda h, i: (h, 0, 0))],
        out_specs=pl.BlockSpec((tm, Nh), lambda h, i: (i, h)),
        compiler_params=_cparams(("parallel", "arbitrary")),
        name=name,
    )(x, w)


class RowLayout:
    def __init__(self, B, T, DB, TS):
        self.B, self.T, self.DB, self.TS = B, T, DB, TS
        self.Mp = B * T
        self.Ms = DB * TS
        self.M = self.Mp + self.Ms
        self.te = math.gcd(math.gcd(ROW_TILE, T), self.Ms)
        self.n_p_blocks = self.Mp // self.te
        self.blocks_per_seq = T // self.te
        self.nblocks = self.M // self.te

    def mod_index(self, i):
        return jnp.where(i < self.n_p_blocks, i // self.blocks_per_seq,
                         self.B + i - self.n_p_blocks)

    def expand_mod(self, m):
        mp = jnp.broadcast_to(m[:self.B, None, :], (self.B, self.te, m.shape[-1]))
        ms = jnp.repeat(m[self.B:], self.TS, axis=0).reshape(self.Ms // self.te, self.te, -1)
        return jnp.concatenate([mp, ms], axis=0)


def _rms(x, g):
    ms = jnp.mean(x * x, axis=-1, keepdims=True)
    return x * lax.rsqrt(ms + NORM_EPS) * g


def _prenorm_kernel(x_ref, g_ref, shift_ref, scale_ref, h_ref):
    h = _rms(x_ref[...], g_ref[...]) * (1.0 + scale_ref[...]) + shift_ref[...]
    h_ref[...] = h.astype(h_ref.dtype)


def prenorm(lay, x, g_pre, mod, out_dtype):
    te = lay.te
    row = pl.BlockSpec((te, D), lambda i: (i, 0))
    return pl.pallas_call(
        _prenorm_kernel,
        out_shape=jax.ShapeDtypeStruct((lay.M, D), out_dtype),
        grid=(lay.nblocks,),
        in_specs=[row,
                  pl.BlockSpec((1, D), lambda i: (0, 0)),
                  pl.BlockSpec((None, te, D), lambda i: (lay.mod_index(i), 0, 0)),
                  pl.BlockSpec((None, te, D), lambda i: (lay.mod_index(i), 0, 1))],
        out_specs=row,
        compiler_params=_cparams(("parallel",)),
        name="prenorm",
    )(x, g_pre[None], mod, mod)


def _residual_kernel(x_ref, o_ref, g_ref, gate_ref, y_ref):
    y_ref[...] = x_ref[...] + gate_ref[...] * _rms(o_ref[...], g_ref[...])


def residual(lay, x, o, g_post, mod):
    te = lay.te
    row = pl.BlockSpec((te, D), lambda i: (i, 0))
    return pl.pallas_call(
        _residual_kernel,
        out_shape=jax.ShapeDtypeStruct((lay.M, D), F32),
        grid=(lay.nblocks,),
        in_specs=[row, row,
                  pl.BlockSpec((1, D), lambda i: (0, 0)),
                  pl.BlockSpec((None, te, D), lambda i: (lay.mod_index(i), 0, 2))],
        out_specs=row,
        compiler_params=_cparams(("parallel",)),
        name="residual",
    )(x, o, g_post[None], mod)


RW_MIX_ORDER = (0, 1, 3, 2, 4, 5)


SUBLANES = 8


def _rw_mix_kernel(h_ref, tail_ref, first_ref, mu_ref, x_ref, *, blocks_per_seq, n_p_blocks, ts):
    i = pl.program_id(0)
    h = h_ref[...]
    rows = h.shape[0]
    ridx = lax.broadcasted_iota(jnp.int32, (rows, 1), 0)
    prev = pltpu.roll(h, 1, 0)
    tail = tail_ref[SUBLANES - 1:SUBLANES, :]
    opens = jnp.logical_and(i < n_p_blocks, i % blocks_per_seq == 0)
    prev = jnp.where(ridx == 0, jnp.where(opens, 0.0, tail), prev)
    is_sample = i >= n_p_blocks
    prev = jnp.where(jnp.logical_and(is_sample, ridx % ts == 0), first_ref[...], prev)
    dx = prev - h
    for o, c in enumerate(RW_MIX_ORDER):
        x_ref[o] = (h + dx * mu_ref[c:c + 1, :]).astype(BF16)


def rw_mix(lay, h, shift_rows, mu):
    te = lay.te
    per = te // SUBLANES
    return pl.pallas_call(
        functools.partial(_rw_mix_kernel, blocks_per_seq=lay.blocks_per_seq,
                          n_p_blocks=lay.n_p_blocks, ts=lay.TS),
        out_shape=jax.ShapeDtypeStruct((6, lay.M, D), BF16),
        grid=(lay.nblocks,),
        in_specs=[pl.BlockSpec((te, D), lambda i: (i, 0)),
                  pl.BlockSpec((SUBLANES, D), lambda i: (jnp.maximum(i * per - 1, 0), 0)),
                  pl.BlockSpec((te, D), lambda i: (jnp.maximum(i - lay.n_p_blocks, 0), 0)),
                  pl.BlockSpec((6, D), lambda i: (0, 0))],
        out_specs=pl.BlockSpec((6, te, D), lambda i: (0, i, 0)),
        compiler_params=_cparams(("parallel",)),
        name="rw_mix",
    )(h, h, shift_rows, mu)


def _head_fold(s):
    s = s + pltpu.roll(s, RW_H, 1)
    return s + pltpu.roll(s, 2 * RW_H, 1)


def _rw_prep_kernel(*refs, vres):
    if vres:
        p_ref, l_ref, par_ref, vf_ref, out_ref, v_ref, rk_ref = refs
        iw, ia = 1, 2
    else:
        p_ref, l_ref, par_ref, out_ref, v_ref, rk_ref = refs
        iw, ia = 0, 1
    rows = out_ref.shape[1]
    n2 = jnp.zeros((rows, LANE), F32)
    rk = jnp.zeros((rows, LANE), F32)
    for j in range(RW_TILES):
        js = slice(j * LANE, (j + 1) * LANE)
        par = lambda i: par_ref[i:i + 1, js]
        r = p_ref[0, :, js]
        k = p_ref[1, :, js]
        v = p_ref[3, :, js]
        decay = jnp.exp(-EXP_M05 * _sigmoid(par(0) + l_ref[iw, :, js]))
        a = _sigmoid(par(1) + l_ref[ia, :, js])
        kk = k * par(2)
        n2 = n2 + kk * kk
        kmod = k * (1.0 + (a - 1.0) * par(3))
        rk = rk + r * kmod * par(4)
        if vres:
            v = v + (vf_ref[:, js] - v) * _sigmoid(par(5) + l_ref[0, :, js])
        out_ref[0, :, js] = decay
        out_ref[1, :, js] = kmod
        out_ref[2, :, js] = kk
        out_ref[3, :, js] = a
        v_ref[:, js] = v
    inv = 1.0 / jnp.maximum(jnp.sqrt(_head_fold(n2)), 1e-12)
    rk_ref[...] = _head_fold(rk)
    for j in range(RW_TILES):
        js = slice(j * LANE, (j + 1) * LANE)
        kkn = out_ref[2, :, js] * inv
        out_ref[2, :, js] = kkn
        out_ref[3, :, js] = kkn * out_ref[3, :, js]


def rw_prep(lay, proj, lora, params, v_first):
    te = lay.te
    vres = v_first is not None
    nl = lora.shape[0]
    in_specs = [pl.BlockSpec((4, te, D), lambda i: (0, i, 0)),
                pl.BlockSpec((nl, te, D), lambda i: (0, i, 0)),
                pl.BlockSpec(params.shape, lambda i: (0, 0))]
    args = [proj, lora, params]
    if vres:
        in_specs.append(pl.BlockSpec((te, D), lambda i: (i, 0)))
        args.append(v_first)
    return pl.pallas_call(
        functools.partial(_rw_prep_kernel, vres=vres),
        out_shape=(jax.ShapeDtypeStruct((4, lay.M, D), F32),
                   jax.ShapeDtypeStruct((lay.M, D), F32),
                   jax.ShapeDtypeStruct((lay.M, LANE), F32)),
        grid=(lay.nblocks,),
        in_specs=in_specs,
        out_specs=(pl.BlockSpec((4, te, D), lambda i: (0, i, 0)),
                   pl.BlockSpec((te, D), lambda i: (i, 0)),
                   pl.BlockSpec((te, LANE), lambda i: (i, 0))),
        compiler_params=_cparams(("parallel",)),
        name="rw_prep",
    )(*args)


def _wkv_kernel(r_a, r_b, s_a, s_b, v_a, v_b, rep_ref, o_a, o_b, st_ref, vrep_a, vrep_b, *, steps):
    @pl.when(pl.program_id(0) == 0)
    def _():
        st_ref[...] = jnp.zeros(st_ref.shape, F32)

    seqs = ((r_a, s_a, v_a, o_a, vrep_a), (r_b, s_b, v_b, o_b, vrep_b))
    for (_, _, v_ref, _, vrep) in seqs:
        y = v_ref[...].reshape(steps * RW_TILES, LANE)
        hi = y.astype(BF16)
        lo = (y - hi.astype(F32)).astype(BF16)
        for g in range(RW_FOLD):
            yr = (jnp.dot(hi, rep_ref[g], preferred_element_type=F32)
                  + jnp.dot(lo, rep_ref[g], preferred_element_type=F32))
            vrep[:, g * RW_TILES:(g + 1) * RW_TILES, :] = yr.reshape(steps, RW_TILES, LANE)

    lane_group = lax.broadcasted_iota(jnp.int32, (RW_TILES, LANE), 1) // RW_H

    def kk_dot(q, s_ref):
        p = jnp.zeros((RW_N, LANE), F32)
        for j in range(RW_TILES):
            p = p + st_ref[q, j] * s_ref[2, 0:1, j * LANE:(j + 1) * LANE]
        return p

    def step(t, carry):
        t_next = jnp.minimum(t + 1, steps - 1)
        out, raw = [], []
        for q, (r_ref, s_ref, _, o_ref, vrep) in enumerate(seqs):
            sa = -(carry[q] if q == 0 else _head_fold(carry[q]))
            vr = vrep[t]
            w_row = s_ref[0, pl.ds(t, 1), :]
            k_row = s_ref[1, pl.ds(t, 1), :]
            b_row = s_ref[3, pl.ds(t, 1), :]
            r_row = r_ref[pl.ds(t, 1), :]
            kk_row = s_ref[2, pl.ds(t_next, 1), :]
            o = jnp.zeros((RW_N, LANE), F32)
            p = jnp.zeros((RW_N, LANE), F32)
            for j in range(RW_TILES):
                js = slice(j * LANE, (j + 1) * LANE)
                sn = st_ref[q, j] * w_row[:, js] + sa * b_row[:, js] + vr * k_row[:, js]
                st_ref[q, j] = sn
                o = o + sn * r_row[:, js]
                p = p + sn * kk_row[:, js]
            out.append(_head_fold(p) if q == 0 else p)
            emit(q, jnp.maximum(t - 1, 0), carry[2 + q])
            raw.append(o)
        return tuple(out + raw)

    def emit(q, t, o):
        o_ref = seqs[q][3]
        o = _head_fold(o)
        res = o[0:RW_TILES]
        for g in range(1, RW_FOLD):
            res = jnp.where(lane_group == g, o[g * RW_TILES:(g + 1) * RW_TILES], res)
        o_ref[t] = res

    init = tuple(kk_dot(q, s[1]) for q, s in enumerate(seqs))
    zero = jnp.zeros((RW_N, LANE), F32)
    last = lax.fori_loop(0, steps, step, (_head_fold(init[0]), init[1], zero, zero))
    for q in range(2):
        emit(q, steps - 1, last[2 + q])


def wkv_prompt(proj4, stack4, v, T):
    M = v.shape[0]
    tb = _tile(T, 128)
    nc = T // tb
    v3 = v.reshape(M, RW_TILES, LANE)
    lane = np.arange(LANE)
    rep = jnp.asarray(np.stack([(lane[:, None] // RW_H == g) & (lane[:, None] % RW_H == lane[None, :] % RW_H)
                                for g in range(RW_FOLD)]), BF16)
    rspec = lambda s: pl.BlockSpec((None, tb, D), lambda c: (0, s * nc + c, 0))
    sspec = lambda s: pl.BlockSpec((4, tb, D), lambda c: (0, s * nc + c, 0))
    vspec = lambda s: pl.BlockSpec((tb, RW_TILES, LANE), lambda c: (s * nc + c, 0, 0))
    ospec = pl.BlockSpec((tb, RW_TILES, LANE), lambda c: (c, 0, 0))
    return pl.pallas_call(
        functools.partial(_wkv_kernel, steps=tb),
        out_shape=(jax.ShapeDtypeStruct((T, RW_TILES, LANE), F32),
                   jax.ShapeDtypeStruct((T, RW_TILES, LANE), F32),
                   jax.ShapeDtypeStruct((2, RW_TILES, RW_N, LANE), F32)),
        grid=(nc,),
        in_specs=[rspec(0), rspec(1), sspec(0), sspec(1), vspec(0), vspec(1),
                  pl.BlockSpec((RW_FOLD, LANE, LANE), lambda c: (0, 0, 0))],
        out_specs=(ospec, ospec, pl.BlockSpec((2, RW_TILES, RW_N, LANE), lambda c: (0, 0, 0, 0))),
        scratch_shapes=[pltpu.VMEM((tb, RW_N, LANE), F32),
                        pltpu.VMEM((tb, RW_N, LANE), F32)],
        compiler_params=_cparams(("arbitrary",)),
        name="wkv",
    )(proj4, proj4, stack4, stack4, v3, v3, rep)


def _wkv_sample_kernel(x_ref, s0_ref, *rest, steps):
    o_ref, st_ref = rest[-2:]

    def row(v, carry):
        s = s0_ref[v]
        for t in range(steps):
            p = jnp.sum(s * x_ref[3, t], axis=0, keepdims=True)
            vt = x_ref[5, t, pl.ds(v, 1), :]
            s = s * x_ref[1, t] - p * x_ref[4, t] + vt * x_ref[2, t]
            o_ref[t, pl.ds(v, 1), :] = jnp.sum(s * x_ref[0, t], axis=0, keepdims=True)
        st_ref[v] = s
        return carry

    lax.fori_loop(0, RW_N, row, 0, unroll=4)


def wkv_sample(x6, state_t, layer, into=None):
    _, TS, H, _, DB = x6.shape
    L = state_t.shape[0]
    st_spec = pl.BlockSpec((None, None, RW_N, RW_N, DB), lambda h: (layer, h, 0, 0, 0))
    in_specs = [pl.BlockSpec((6, TS, None, RW_N, DB), lambda h: (0, 0, h, 0, 0)), st_spec]
    args = [x6, state_t]
    aliases = {}
    if into is not None:
        aliases = {2: 1}
        in_specs.append(pl.BlockSpec(memory_space=pl.ANY))
        args.append(into)
    return pl.pallas_call(
        functools.partial(_wkv_sample_kernel, steps=TS),
        out_shape=(jax.ShapeDtypeStruct((TS, H, RW_N, DB), F32),
                   jax.ShapeDtypeStruct((L, H, RW_N, RW_N, DB), F32)),
        grid=(H,),
        in_specs=in_specs,
        out_specs=(pl.BlockSpec((TS, None, RW_N, DB), lambda h: (0, h, 0, 0)), st_spec),
        input_output_aliases=aliases,
        compiler_params=_cparams(("parallel",)),
        name="wkv_sample",
    )(*args)


def _rw_post_kernel(oa_ref, ob_ref, os_ref, v_ref, rk_ref, z_ref, ln_ref, y_ref, o_scr, *, n_seq_blocks):
    i = pl.program_id(0)
    rows = y_ref.shape[0]
    in_a = i < n_seq_blocks
    in_b = jnp.logical_and(i >= n_seq_blocks, i < 2 * n_seq_blocks)
    s = jnp.zeros((rows, LANE), F32)
    for j in range(RW_TILES):
        js = slice(j * LANE, (j + 1) * LANE)
        oj = jnp.where(in_a, oa_ref[:, j, :], jnp.where(in_b, ob_ref[:, j, :], os_ref[:, js]))
        o_scr[:, js] = oj
        s = s + oj
    mean = _head_fold(s) * (1.0 / RW_N)
    s2 = jnp.zeros((rows, LANE), F32)
    for j in range(RW_TILES):
        d = o_scr[:, j * LANE:(j + 1) * LANE] - mean
        s2 = s2 + d * d
    rstd = lax.rsqrt(_head_fold(s2) * (1.0 / RW_N) + RW_GN_EPS)
    rk = rk_ref[...]
    for j in range(RW_TILES):
        js = slice(j * LANE, (j + 1) * LANE)
        o = (o_scr[:, js] - mean) * rstd * ln_ref[0:1, js] + ln_ref[1:2, js]
        o = o + rk * v_ref[:, js]
        y_ref[:, js] = (o * _silu(z_ref[:, js])).astype(BF16)


def rw_post(lay, o_a, o_b, o_s, v, rk, proj4, ln):
    te = lay.te
    row = pl.BlockSpec((te, D), lambda i: (i, 0))
    nsb = lay.blocks_per_seq
    npb = lay.n_p_blocks
    a_idx = lambda i: jnp.where(i < nsb, i, 0)
    b_idx = lambda i: jnp.where(jnp.logical_and(i >= nsb, i < 2 * nsb), i - nsb, 0)
    s_idx = lambda i: jnp.maximum(i - npb, 0)
    return pl.pallas_call(
        functools.partial(_rw_post_kernel, n_seq_blocks=nsb),
        out_shape=jax.ShapeDtypeStruct((lay.M, D), BF16),
        grid=(lay.nblocks,),
        in_specs=[pl.BlockSpec((te, RW_TILES, LANE), lambda i: (a_idx(i), 0, 0)),
                  pl.BlockSpec((te, RW_TILES, LANE), lambda i: (b_idx(i), 0, 0)),
                  pl.BlockSpec((te, D), lambda i: (s_idx(i), 0)),
                  row,
                  pl.BlockSpec((te, LANE), lambda i: (i, 0)),
                  pl.BlockSpec((None, te, D), lambda i: (2, i, 0)),
                  pl.BlockSpec((2, D), lambda i: (0, 0))],
        out_specs=row,
        scratch_shapes=[pltpu.VMEM((te, D), F32)],
        compiler_params=_cparams(("parallel",)),
        name="rw_post",
    )(o_a, o_b, o_s, v, rk, proj4, ln)


def _head_minor(a):
    lead = a.shape[:-1]
    return a.reshape(lead + (RW_H, RW_N)).swapaxes(-1, -2).reshape(lead + (D,))


def _tiles_to_state(s):
    n = s.shape[0]
    s = s.reshape(n, RW_TILES, RW_FOLD, RW_TILES, RW_FOLD, RW_H)
    return s.transpose(0, 5, 3, 2, 1, 4).reshape(n, RW_H, RW_N, RW_N)


def _pad_cols(w, n):
    return jnp.pad(w, ((0, 0), (0, n - w.shape[1])))


def _pad_rows(w, n):
    return jnp.pad(w, ((0, n - w.shape[0]), (0, 0)))


def rwkv_layer(lay, h, shift_state, state_t, layer, s_states, v_first, w):
    B, T, DB, TS, Mp, M = lay.B, lay.T, lay.DB, lay.TS, lay.Mp, lay.M
    x6 = rw_mix(lay, h, jnp.repeat(shift_state, TS, axis=0), w["mu"])
    w_in = jnp.stack([_head_minor(w["w_in"][c]).astype(BF16) for c in (0, 1, 3, 2)])
    proj4 = matmul(x6, w_in, name="rw_proj")
    vres = v_first is not None
    l1 = [w["v1"]] if vres else []
    l1 += [w["w1"], w["a1"]]
    l2 = [w["v2"]] if vres else []
    l2 += [w["w2"], w["a2"]]
    l1 = jnp.stack([_pad_cols(a, RW_LORA_PAD) for a in l1]).astype(BF16)
    l2 = jnp.stack([_head_minor(_pad_rows(a, RW_LORA_PAD)) for a in l2]).astype(BF16)
    lo1 = matmul(x6, l1, x_off=3 if vres else 4, act_out="tanh_group",
                 tanh_group=1 if vres else 0, out_dtype=BF16, name="rw_lora1")
    lora = matmul(lo1, l2, name="rw_lora2")
    plist = [w["w0"], w["a0"], w["k_k"], w["k_a"], w["r_k"].reshape(-1)]
    if vres:
        plist.append(w["v0"])
    plist += [jnp.zeros((D,), F32)] * (8 - len(plist))
    params = _head_minor(jnp.stack(plist))
    stack4, v, rk = rw_prep(lay, proj4, lora, params, v_first)
    v_first_out = v_first if vres else v

    assert B == 2
    o_a, o_b, p_st = wkv_prompt(proj4, stack4, v, T)
    xs = jnp.concatenate([proj4[0:1, Mp:], stack4[:, Mp:], v[None, Mp:]], axis=0)
    xs = xs.reshape(6, DB, TS, RW_N, RW_H).transpose(0, 2, 4, 3, 1)
    o_s, s_states = wkv_sample(xs, state_t, layer, s_states)
    o_s = o_s.transpose(3, 0, 2, 1).reshape(lay.Ms, D)

    ln = _head_minor(jnp.stack([w["ln_w"], w["ln_b"]]))
    y = rw_post(lay, o_a, o_b, o_s, v, rk, proj4, ln)
    w_out = w["w_out"].reshape(RW_H, RW_N, D).swapaxes(0, 1).reshape(D, D)
    o_proj = matmul2d(y, w_out.astype(BF16), name="rw_out")
    p_shift = h[T - 1:Mp:T]
    s_shift = h[Mp:].reshape(DB, TS, D)[:, -1]
    return o_proj, _tiles_to_state(p_st), p_shift, s_states, s_shift, v_first_out


def _rot_cols(w):
    half = MLA_ROPE // 2
    return jnp.concatenate([-w[..., half:], w[..., :half]], axis=-1)


def _mla_prep_kernel(pq_ref, pkv_ref, pkr_ref, gq_ref, gkv_ref, cs_ref,
                     cq_ref, ckv_ref, kr_ref, krb_ref):
    cq_ref[...] = _rms(pq_ref[...], gq_ref[...]).astype(BF16)
    ckv_ref[...] = _rms(pkv_ref[...], gkv_ref[...])
    t2 = pkr_ref[...]
    kr = t2 * cs_ref[0] + pltpu.roll(t2, MLA_ROPE, 1) * cs_ref[1]
    kr_ref[...] = kr[:, :MLA_ROPE]
    krb_ref[...] = kr.astype(BF16)


def mla_prep(lay, proj, g_q, g_kv, cs):
    te = lay.te
    M = lay.M
    return pl.pallas_call(
        _mla_prep_kernel,
        out_shape=(jax.ShapeDtypeStruct((M, MLA_QL), BF16),
                   jax.ShapeDtypeStruct((M, MLA_KVL), F32),
                   jax.ShapeDtypeStruct((M, MLA_ROPE), F32),
                   jax.ShapeDtypeStruct((M, LANE), BF16)),
        grid=(lay.nblocks,),
        in_specs=[pl.BlockSpec((te, MLA_QL), lambda i: (i, D // MLA_QL)),
                  pl.BlockSpec((te, MLA_KVL), lambda i: (i, D // MLA_KVL + 1)),
                  pl.BlockSpec((te, LANE), lambda i: (i, (D + MLA_QL + MLA_KVL) // LANE)),
                  pl.BlockSpec((1, MLA_QL), lambda i: (0, 0)),
                  pl.BlockSpec((1, MLA_KVL), lambda i: (0, 0)),
                  pl.BlockSpec((2, te, LANE), lambda i: (0, i, 0))],
        out_specs=(pl.BlockSpec((te, MLA_QL), lambda i: (i, 0)),
                   pl.BlockSpec((te, MLA_KVL), lambda i: (i, 0)),
                   pl.BlockSpec((te, MLA_ROPE), lambda i: (i, 0)),
                   pl.BlockSpec((te, LANE), lambda i: (i, 0))),
        compiler_params=_cparams(("parallel",)),
        name="mla_prep",
    )(proj, proj, proj, g_q[None], g_kv[None], cs)


def _mla_q_kernel(q_ref, cs_ref, qn_ref, qr_ref):
    for h in range(MLA_H):
        qn_ref[:, h * LANE:(h + 1) * LANE] = q_ref[:, 2 * h * LANE:(2 * h + 1) * LANE].astype(BF16)
        t2 = q_ref[:, (2 * h + 1) * LANE:(2 * h + 2) * LANE]
        qr = t2 * cs_ref[0] + pltpu.roll(t2, MLA_ROPE, 1) * cs_ref[1]
        qr_ref[:, h * LANE:(h + 1) * LANE] = qr.astype(BF16)


def mla_q(lay, q, cs):
    te = lay.te
    M = lay.M
    return pl.pallas_call(
        _mla_q_kernel,
        out_shape=(jax.ShapeDtypeStruct((M, MLA_H * LANE), BF16),
                   jax.ShapeDtypeStruct((M, MLA_H * LANE), BF16)),
        grid=(lay.nblocks,),
        in_specs=[pl.BlockSpec((te, MLA_H * 2 * LANE), lambda i: (i, 0)),
                  pl.BlockSpec((2, te, LANE), lambda i: (0, i, 0))],
        out_specs=(pl.BlockSpec((te, MLA_H * LANE), lambda i: (i, 0)),
                   pl.BlockSpec((te, MLA_H * LANE), lambda i: (i, 0))),
        compiler_params=_cparams(("parallel",)),
        name="mla_q",
    )(q, cs)


FLASH_HEADS = 2


def _flash_kernel(qn_ref, qr_ref, kn_ref, kr_ref, v_ref, o_ref, *, tq, tk):
    qi = pl.program_id(2)
    nt = (((1,), (1,)), ((), ()))
    hd = lambda u: slice(u * LANE, (u + 1) * LANE)
    qs = [jnp.concatenate([qn_ref[:, hd(u)], qr_ref[:, hd(u)]], axis=1) for u in range(FLASH_HEADS)]
    c = MLA_SCALE * math.log2(math.e)
    per_q = tq // tk

    def update(start, carry, diag):
        kr = kr_ref[pl.ds(start, tk), :]
        ss = [lax.dot_general(qs[u], jnp.concatenate([kn_ref[pl.ds(start, tk), hd(u)], kr], axis=1),
                              nt, preferred_element_type=F32) * c for u in range(FLASH_HEADS)]
        out = []
        for u in range(FLASH_HEADS):
            m, l, acc = carry[u]
            s = ss[u]
            if diag is not None:
                s = jnp.where(lax.broadcasted_iota(jnp.int32, (tq, tk), 1) + diag * tk
                              <= lax.broadcasted_iota(jnp.int32, (tq, tk), 0), s, -1e30)
            m_new = jnp.maximum(m, jnp.max(s, axis=1, keepdims=True))
            p = jnp.exp2(s - m_new)
            alpha = jnp.exp2(m - m_new)
            l = alpha * l + jnp.sum(p, axis=1, keepdims=True)
            acc = alpha * acc + jnp.dot(p.astype(BF16), v_ref[pl.ds(start, tk), hd(u)],
                                        preferred_element_type=F32)
            out.append((m_new, l, acc))
        return tuple(out)

    def body(ki, carry):
        return update(pl.multiple_of(ki * tk, tk), carry, None)

    init = (jnp.full((tq, 1), -1e30, F32), jnp.zeros((tq, 1), F32), jnp.zeros((tq, MLA_V), F32))
    carry = lax.fori_loop(0, qi * per_q, body, (init,) * FLASH_HEADS)
    for d in range(per_q):
        carry = update(pl.multiple_of(qi * tq + d * tk, tk), carry, d)
    for u in range(FLASH_HEADS):
        _, l, acc = carry[u]
        o_ref[:, hd(u)] = acc / l


def mla_flash(qn, qr, knv, krb, B, T):
    tq = _tile(T, 512)
    tk = tq
    nq = T // tq
    hw = FLASH_HEADS * LANE
    groups = MLA_H // FLASH_HEADS
    return pl.pallas_call(
        functools.partial(_flash_kernel, tq=tq, tk=tk),
        out_shape=jax.ShapeDtypeStruct((B * T, MLA_H * MLA_V), F32),
        grid=(B, groups, nq),
        in_specs=[pl.BlockSpec((tq, hw), lambda b, h, i: (b * nq + i, h)),
                  pl.BlockSpec((tq, hw), lambda b, h, i: (b * nq + i, h)),
                  pl.BlockSpec((T, hw), lambda b, h, i: (b, h)),
                  pl.BlockSpec((T, LANE), lambda b, h, i: (b, 0)),
                  pl.BlockSpec((T, hw), lambda b, h, i: (b, groups + h))],
        out_specs=pl.BlockSpec((tq, hw), lambda b, h, i: (b * nq + i, h)),
        compiler_params=_cparams(("parallel", "parallel", "arbitrary")),
        name="mla_flash",
    )(qn, qr, knv, krb, knv)


def _paged_kernel(pt_ref, ql_ref, qr_ref, cn_ref, kn_ref, *rest, pg, ts):
    ck_refs = rest[:pg]
    kr_refs = rest[pg:2 * pg]
    o_ref, m_ref, l_ref, acc_ref = rest[2 * pg:]
    p = pl.program_id(1)
    nt = (((1,), (1,)), ((), ()))

    @pl.when(p == 0)
    def _():
        m_ref[...] = jnp.full(m_ref.shape, -1e30, F32)
        l_ref[...] = jnp.zeros(l_ref.shape, F32)
        acc_ref[...] = jnp.zeros(acc_ref.shape, F32)

    rows = ts * MLA_H
    ql = ql_ref[...]
    qr = qr_ref[...][:, :MLA_ROPE]
    cks = [r[...].astype(BF16) for r in ck_refs]
    ck = jnp.concatenate(cks, axis=0)
    kr_t = jnp.concatenate([r[...].astype(BF16) for r in kr_refs], axis=1)
    s = jnp.concatenate([lax.dot_general(ql, c, nt, preferred_element_type=F32) for c in cks], axis=1)
    s = (s + jnp.dot(qr, kr_t, preferred_element_type=F32)) * MLA_SCALE
    m = m_ref[...]
    m_new = jnp.maximum(m, jnp.max(s, axis=1, keepdims=True))
    pr = jnp.exp(s - m_new)
    alpha = jnp.exp(m - m_new)
    l = alpha * l_ref[...] + jnp.sum(pr, axis=1, keepdims=True)
    acc = alpha * acc_ref[...] + jnp.dot(pr.astype(BF16), ck, preferred_element_type=F32)
    m = m_new
    m_ref[...] = m
    l_ref[...] = l
    acc_ref[...] = acc

    @pl.when(p == pl.num_programs(1) - 1)
    def _():
        qlf = ql.astype(F32)
        qrf = qr.astype(F32)
        tok = lax.broadcasted_iota(jnp.int32, (ts * MLA_H, 1), 0) // MLA_H
        cols = []
        for j in range(ts):
            cn = cn_ref[j:j + 1, :].astype(BF16).astype(F32)
            kn = kn_ref[j:j + 1, :MLA_ROPE].astype(F32)
            sj = (jnp.sum(qlf * cn, axis=1, keepdims=True)
                  + jnp.sum(qrf * kn, axis=1, keepdims=True)) * MLA_SCALE
            cols.append(jnp.where(tok >= j, sj, -1e30))
        m2 = m
        for sj in cols:
            m2 = jnp.maximum(m2, sj)
        alpha = jnp.exp(m - m2)
        l2 = alpha * l
        acc2 = alpha * acc
        for j, sj in enumerate(cols):
            pj = jnp.exp(sj - m2)
            l2 = l2 + pj
            acc2 = acc2 + pj.astype(BF16).astype(F32) * cn_ref[j:j + 1, :].astype(BF16).astype(F32)
        o_ref[...] = (acc2 / l2).astype(o_ref.dtype)


def mla_paged(page_table, ql, qr, ckv_new, krb_new, cache_ckv, cache_kr, layer, DB, TS):
    n_pages = page_table.shape[1]
    pg = _tile(n_pages, 32) if n_pages >= 8 else n_pages
    rows = TS * MLA_H
    pt = page_table.reshape(-1)

    def page_spec(i, shape):
        return pl.BlockSpec((None, None) + shape,
                            lambda b, p, pt_ref: (layer, pt_ref[b * n_pages + p * pg + i], 0, 0))

    cache_kr_t = jnp.swapaxes(cache_kr, 2, 3)
    per_b = lambda width: pl.BlockSpec((None, rows, width), lambda b, p, pt_ref: (b, 0, 0))
    new_b = lambda width: pl.BlockSpec((None, TS, width), lambda b, p, pt_ref: (b, 0, 0))
    grid_spec = pltpu.PrefetchScalarGridSpec(
        num_scalar_prefetch=1,
        grid=(DB, n_pages // pg),
        in_specs=[per_b(MLA_KVL), per_b(LANE), new_b(MLA_KVL), new_b(LANE)]
                 + [page_spec(i, (PAGE, MLA_KVL)) for i in range(pg)]
                 + [page_spec(i, (MLA_ROPE, PAGE)) for i in range(pg)],
        out_specs=per_b(MLA_KVL),
        scratch_shapes=[pltpu.VMEM((rows, 1), F32), pltpu.VMEM((rows, 1), F32),
                        pltpu.VMEM((rows, MLA_KVL), F32)],
    )
    return pl.pallas_call(
        functools.partial(_paged_kernel, pg=pg, ts=TS),
        out_shape=jax.ShapeDtypeStruct((DB, rows, MLA_KVL), BF16),
        grid_spec=grid_spec,
        compiler_params=_cparams(("parallel", "arbitrary")),
        name="mla_paged",
    )(pt, ql, qr, ckv_new, krb_new, *([cache_ckv] * pg), *([cache_kr_t] * pg))


def _gate_kernel(o_ref, z_ref, y_ref):
    y_ref[...] = (o_ref[...] * _silu(z_ref[...])).astype(BF16)


def gate_mul(lay, o, zsrc, zcol):
    te = lay.te
    return pl.pallas_call(
        _gate_kernel,
        out_shape=jax.ShapeDtypeStruct((lay.M, D), BF16),
        grid=(lay.nblocks,),
        in_specs=[pl.BlockSpec((te, D), lambda i: (i, 0)),
                  pl.BlockSpec((te, D), lambda i: (i, zcol))],
        out_specs=pl.BlockSpec((te, D), lambda i: (i, 0)),
        compiler_params=_cparams(("parallel",)),
        name="gate_mul",
    )(o, zsrc)


def mla_layer(lay, h, cache_ckv, cache_kr, layer, page_table, w):
    B, T, DB, TS, Mp, M = lay.B, lay.T, lay.DB, lay.TS, lay.Mp, lay.M
    w_in = w["w_in"]
    w_kr = w_in[:, 2 * MLA_QL:2 * MLA_QL + MLA_ROPE]
    w1 = jnp.concatenate([w_in[:, 2 * MLA_QL + MLA_ROPE:], w_in[:, :2 * MLA_QL],
                          w_kr, _rot_cols(w_kr)], axis=1)
    proj = matmul2d(h, w1.astype(BF16), tn=640, name="mla_proj")
    pos = jnp.concatenate([jnp.tile(jnp.arange(T, dtype=F32), B),
                           jnp.tile(page_table.shape[1] * PAGE + jnp.arange(TS, dtype=F32), DB)])
    half = MLA_ROPE // 2
    inv_freq = ROPE_THETA ** (-jnp.arange(half, dtype=F32) / half)
    ang = pos[:, None] * inv_freq[None, :]
    zeros = jnp.zeros((M, MLA_ROPE), F32)
    cs = jnp.stack([jnp.concatenate([jnp.cos(ang), jnp.cos(ang), zeros], axis=1),
                    jnp.concatenate([jnp.sin(ang), jnp.sin(ang), zeros], axis=1)])
    cq, ckv, kr, krb = mla_prep(lay, proj, w["q_norm"], w["kv_norm"], cs)
    w_uq = w["w_uq"]
    wq = jnp.concatenate([w_uq, _rot_cols(w_uq[..., MLA_NOPE:])], axis=-1)
    q = matmul2d(cq, wq.reshape(MLA_QL, MLA_H * 2 * LANE).astype(BF16), name="mla_qproj")
    qn, qr = mla_q(lay, q, cs)
    w_kv = jnp.concatenate([w["w_uk"].reshape(MLA_KVL, -1), w["w_uv"].reshape(MLA_KVL, -1)], axis=1)
    knv = matmul2d(ckv[:Mp], w_kv.astype(BF16), out_dtype=BF16, name="mla_kv")
    o_p = mla_flash(qn, qr, knv, krb, B, T)
    w_ukT = w["w_uk"].transpose(1, 2, 0).astype(BF16)
    ql = head_matmul(qn[Mp:], w_ukT, BF16, "mla_qlat")
    o_lat = mla_paged(page_table, ql.reshape(DB, TS * MLA_H, MLA_KVL),
                      qr[Mp:].reshape(DB, TS * MLA_H, LANE),
                      ckv[Mp:].reshape(DB, TS, MLA_KVL), krb[Mp:].reshape(DB, TS, LANE),
                      cache_ckv, cache_kr, layer, DB, TS)
    o_s = head_matmul(o_lat.reshape(lay.Ms, MLA_H * MLA_KVL),
                      w["w_uv"].transpose(1, 0, 2).astype(BF16), F32, "mla_ouv")
    o = jnp.concatenate([o_p, o_s], axis=0)
    y = gate_mul(lay, o, proj, 0)
    o_proj = matmul2d(y, w["w_out"].astype(BF16), name="mla_out")
    return (o_proj, ckv[:Mp].reshape(B, T, MLA_KVL), kr[:Mp].reshape(B, T, MLA_ROPE),
            ckv[Mp:].reshape(DB, TS, MLA_KVL), kr[Mp:].reshape(DB, TS, MLA_ROPE))


def _gla_kernel(q_ref, k_ref, v_ref, g_ref, s0_ref, o_ref, sT_ref, st_ref, *, L):
    c = pl.program_id(1)
    C = GLA_SUB
    assert L % C == 0
    heads = range(GLA_H)
    dk = lambda h: slice(h * GLA_DK, (h + 1) * GLA_DK)
    dv = lambda h: slice(h * GLA_DV, (h + 1) * GLA_DV)

    @pl.when(c == 0)
    def _():
        for h in heads:
            st_ref[h] = s0_ref[h].T

    nt = (((1,), (1,)), ((), ()))
    tn = (((0,), (0,)), ((), ()))
    trow = lax.broadcasted_iota(jnp.int32, (L, 1), 0)
    crow = lax.broadcasted_iota(jnp.int32, (C, 1), 0)
    col = lax.broadcasted_iota(jnp.int32, (C, L), 1)
    q = [q_ref[:, dk(h)] * (GLA_DK ** -0.5) for h in heads]
    k = [k_ref[:, dk(h)] for h in heads]
    v = [v_ref[:, dv(h)].astype(BF16) for h in heads]
    b = []
    for h in heads:
        bh = g_ref[:, dk(h)]
        sh = 1
        while sh < L:
            bh = bh + jnp.where(trow >= sh, pltpu.roll(bh, sh, 0), 0.0)
            sh *= 2
        b.append(bh)
    st = [st_ref[h] for h in heads]
    o = [lax.dot_general((q[h] * jnp.exp(b[h])).astype(BF16), st[h].astype(BF16), nt,
                         preferred_element_type=F32) for h in heads]
    att_rows = [[] for _ in heads]
    for i in range(L // C):
        lo = i * C
        if i == 0:
            att_i = [jnp.zeros((C, L), F32) for _ in heads]
        else:
            att_i = []
            for h in heads:
                beta = b[h][lo - 1:lo, :]
                q_in = (q[h][lo:lo + C] * jnp.exp(b[h][lo:lo + C] - beta)).astype(BF16)
                k_out = (k[h] * jnp.exp(jnp.where(trow < lo, beta - b[h], -jnp.inf))).astype(BF16)
                att_i.append(lax.dot_general(q_in, k_out, nt, preferred_element_type=F32))
        for s in range(C):
            for h in heads:
                bi = b[h][lo:lo + C]
                e = jnp.where(crow >= s, bi - bi[s:s + 1, :], -jnp.inf)
                a_s = jnp.sum(q[h][lo:lo + C] * k[h][lo + s:lo + s + 1, :] * jnp.exp(e),
                              axis=1, keepdims=True)
                att_i[h] = jnp.where(col == lo + s, a_s, att_i[h])
        for h in heads:
            att_rows[h].append(att_i[h])
    for h in heads:
        att = att_rows[h][0] if len(att_rows[h]) == 1 else jnp.concatenate(att_rows[h], axis=0)
        o_ref[:, dv(h)] = o[h] + jnp.dot(att.astype(BF16), v[h], preferred_element_type=F32)
    for h in heads:
        b_end = b[h][L - 1:L, :]
        kd = (k[h] * jnp.exp(b_end - b[h])).astype(BF16)
        st[h] = st[h] * jnp.exp(b_end) + lax.dot_general(v[h], kd, tn, preferred_element_type=F32)
        st_ref[h] = st[h]

    @pl.when(c == pl.num_programs(1) - 1)
    def _():
        for h in heads:
            sT_ref[h] = st[h].T


def gla_scan(q, k, v, g, s0, *, nseq, nchunk, L, kcol, vcol):
    rows = lambda n, c: n * nchunk + c
    st_spec = pl.BlockSpec((None, GLA_H, GLA_DK, GLA_DV), lambda n, c: (n, 0, 0, 0))
    return pl.pallas_call(
        functools.partial(_gla_kernel, L=L),
        out_shape=(jax.ShapeDtypeStruct((q.shape[0], GLA_H * GLA_DV), F32),
                   jax.ShapeDtypeStruct((nseq, GLA_H, GLA_DK, GLA_DV), F32)),
        grid=(nseq, nchunk),
        in_specs=[pl.BlockSpec((L, GLA_QK), lambda n, c: (rows(n, c), 0)),
                  pl.BlockSpec((L, GLA_QK), lambda n, c: (rows(n, c), kcol)),
                  pl.BlockSpec((L, GLA_H * GLA_DV), lambda n, c: (rows(n, c), vcol)),
                  pl.BlockSpec((L, GLA_QK), lambda n, c: (rows(n, c), 0)),
                  st_spec],
        out_specs=(pl.BlockSpec((L, GLA_H * GLA_DV), lambda n, c: (rows(n, c), 0)), st_spec),
        scratch_shapes=[pltpu.VMEM((GLA_H, GLA_DV, GLA_DK), F32)],
        compiler_params=_cparams(("parallel", "arbitrary")),
        name="gla_scan",
    )(q, k, v, g, s0)


def _gla_post_kernel(o_ref, z_ref, g_ref, y_ref):
    for h in range(GLA_H):
        hs = slice(h * GLA_DV, (h + 1) * GLA_DV)
        y_ref[:, hs] = (_rms(o_ref[:, hs], g_ref[...]) * _silu(z_ref[:, hs])).astype(BF16)


def gla_post(lay, o, proj, g_norm):
    te = lay.te
    return pl.pallas_call(
        _gla_post_kernel,
        out_shape=jax.ShapeDtypeStruct((lay.M, D), BF16),
        grid=(lay.nblocks,),
        in_specs=[pl.BlockSpec((te, D), lambda i: (i, 0)),
                  pl.BlockSpec((te, D), lambda i: (i, 2)),
                  pl.BlockSpec((1, GLA_DV), lambda i: (0, 0))],
        out_specs=pl.BlockSpec((te, D), lambda i: (i, 0)),
        compiler_params=_cparams(("parallel",)),
        name="gla_post",
    )(o, proj, g_norm[None])


def gla_layer(lay, h, state, w):
    B, T, DB, TS, Mp, M = lay.B, lay.T, lay.DB, lay.TS, lay.Mp, lay.M
    w_in = _pad_cols(w["w_in"], 2 * GLA_QK + 2 * D + GLA_LORA_PAD)
    proj = matmul2d(h, w_in.astype(BF16), tn=640, name="gla_proj")
    gl = proj[:, 2 * GLA_QK + 2 * D:]
    g = matmul2d(gl, _pad_rows(w["w_g2"], GLA_LORA_PAD).astype(BF16), w["b_g"],
                 act_out="logsig_tau", name="gla_gate")
    L = math.gcd(T, GLA_CHUNK)
    cols = dict(kcol=1, vcol=2 * GLA_QK // (GLA_H * GLA_DV))
    o_p, p_state = gla_scan(proj, proj, proj, g, jnp.zeros((B, GLA_H, GLA_DK, GLA_DV), F32),
                            nseq=B, nchunk=T // L, L=L, **cols)
    LS = GLA_SUB
    pad = lambda a: jnp.pad(a.reshape(DB, TS, -1), ((0, 0), (0, LS - TS), (0, 0))).reshape(DB * LS, -1)
    ps = pad(proj[Mp:, :2 * GLA_QK + D])
    o_s, s_state = gla_scan(ps, ps, ps, pad(g[Mp:]), state, nseq=DB, nchunk=1, L=LS, **cols)
    o_s = o_s.reshape(DB, LS, D)[:, :TS].reshape(lay.Ms, D)
    o = jnp.concatenate([o_p[:Mp], o_s], axis=0)
    y = gla_post(lay, o, proj, w["norm"])
    o_proj = matmul2d(y, w["w_out"].astype(BF16), name="gla_out")
    return o_proj, p_state, s_state


def kernel(x_prompt, x_sample, c_prompt, c_sample, state_rwkv_wkv, state_rwkv_shift, cache_mla_ckv, cache_mla_krope, page_table, state_gla, norm_pre, norm_post, ada_w, ada_b, rw_mu, rw_w_in, rw_w0, rw_w1, rw_w2, rw_a0, rw_a1, rw_a2, rw_v0, rw_v1, rw_v2, rw_k_k, rw_k_a, rw_r_k, rw_ln_w, rw_ln_b, rw_w_out, mla_w_in, mla_q_norm, mla_kv_norm, mla_w_uq, mla_w_uk, mla_w_uv, mla_w_out, gla_w_in, gla_w_g2, gla_b_g, gla_norm, gla_w_out):
    B, T, _ = x_prompt.shape
    DB, TS, _ = x_sample.shape
    depth = norm_pre.shape[0]
    lay = RowLayout(B, T, DB, TS)
    Mp = lay.Mp
    x = jnp.concatenate([x_prompt.reshape(Mp, D), x_sample.reshape(lay.Ms, D)], axis=0)
    c = jnp.concatenate([c_prompt, c_sample], axis=0)[None]
    c = jnp.broadcast_to(c, (depth,) + c.shape[1:])
    mods = matmul(c, ada_w, ada_b[:, None, :], act_in="silu", name="ada")

    p_wkv, p_shift, p_ckv, p_kr, p_gla = [], [], [], [], []
    s_shift, s_ckv, s_kr, s_gla = [], [], [], []
    v_first = None
    state_t = jnp.transpose(state_rwkv_wkv, (0, 2, 3, 4, 1))
    s_wkv_t = None
    for i in range(depth):
        kind, j = i % 3, i // 3
        mod = lay.expand_mod(mods[i])
        h = prenorm(lay, x, norm_pre[i], mod, F32 if kind == 0 else BF16)
        if kind == 0:
            w = dict(mu=rw_mu[j], w_in=rw_w_in[j], w0=rw_w0[j], w1=rw_w1[j], w2=rw_w2[j],
                     a0=rw_a0[j], a1=rw_a1[j], a2=rw_a2[j], k_k=rw_k_k[j], k_a=rw_k_a[j],
                     r_k=rw_r_k[j], ln_w=rw_ln_w[j], ln_b=rw_ln_b[j], w_out=rw_w_out[j])
            if j > 0:
                w.update(v0=rw_v0[j - 1], v1=rw_v1[j - 1], v2=rw_v2[j - 1])
            o, pst, psh, s_wkv_t, ssh, v_first = rwkv_layer(
                lay, h, state_rwkv_shift[j], state_t, j, s_wkv_t, v_first if j > 0 else None, w)
            p_wkv.append(pst)
            p_shift.append(psh)
            s_shift.append(ssh)
        elif kind == 1:
            w = dict(w_in=mla_w_in[j], q_norm=mla_q_norm[j], kv_norm=mla_kv_norm[j],
                     w_uq=mla_w_uq[j], w_uk=mla_w_uk[j], w_uv=mla_w_uv[j], w_out=mla_w_out[j])
            o, pc, pk, sc, sk = mla_layer(lay, h, cache_mla_ckv, cache_mla_krope, j, page_table, w)
            p_ckv.append(pc)
            p_kr.append(pk)
            s_ckv.append(sc)
            s_kr.append(sk)
        else:
            w = dict(w_in=gla_w_in[j], w_g2=gla_w_g2[j], b_g=gla_b_g[j], norm=gla_norm[j],
                     w_out=gla_w_out[j])
            o, pg, sg = gla_layer(lay, h, state_gla[j], w)
            p_gla.append(pg)
            s_gla.append(sg)
        x = residual(lay, x, o, norm_post[i], mod)
    yp = x[:Mp].reshape(B, T, D)
    ys = x[Mp:].reshape(DB, TS, D)
    return (yp, ys,
            jnp.stack(p_wkv), jnp.stack(p_shift), jnp.stack(p_ckv), jnp.stack(p_kr), jnp.stack(p_gla),
            jnp.transpose(s_wkv_t, (0, 4, 1, 2, 3)), jnp.stack(s_shift), jnp.stack(s_ckv),
            jnp.stack(s_kr), jnp.stack(s_gla))
```

```python
import functools
import math

import jax
import jax.numpy as jnp
import numpy as np
from jax import lax
from jax.experimental import pallas as pl
from jax.experimental.pallas import tpu as pltpu

F32 = jnp.float32
BF16 = jnp.bfloat16

D = 2048
NORM_EPS = 1e-6
RW_N = 64
RW_H = D // RW_N
RW_LORA_PAD = 128
RW_GN_EPS = 64e-5
RW_TILES = D // 128
RW_FOLD = 128 // RW_H
EXP_M05 = math.exp(-0.5)
MLA_H = 16
MLA_NOPE = 128
MLA_ROPE = 64
MLA_V = 128
MLA_QL = 512
MLA_KVL = 512
MLA_SCALE = (MLA_NOPE + MLA_ROPE) ** -0.5
ROPE_THETA = 10000.0
PAGE = 128
GLA_H = 4
GLA_DK = 256
GLA_DV = 512
GLA_QK = GLA_H * GLA_DK
GLA_LORA_PAD = 256
GLA_TAU = 16.0
GLA_CHUNK = 64
GLA_SUB = 16

LANE = 128
ROW_TILE = 128
VMEM_LIMIT = 48 * 1024 * 1024


def _cparams(sem):
    return pltpu.CompilerParams(dimension_semantics=sem, vmem_limit_bytes=VMEM_LIMIT)


def _tile(n, pref):
    for t in (1024, 640, 512, 384, 256, 128, 64, 32, 16, 8):
        if t <= pref and n % t == 0:
            return t
    return n


def _sigmoid(x):
    return 1.0 / (1.0 + jnp.exp(-x))


def _silu(x):
    return x * _sigmoid(x)


def _mm_kernel(*refs, has_bias, act_in, act_out, tanh_group):
    if has_bias:
        x_ref, w_ref, b_ref, o_ref = refs
    else:
        x_ref, w_ref, o_ref = refs
    x = x_ref[...]
    if act_in == "silu":
        x = _silu(x.astype(F32))
    acc = jnp.dot(x.astype(BF16), w_ref[...].astype(BF16), preferred_element_type=F32)
    if has_bias:
        acc = acc + b_ref[...]
    if act_out == "tanh_group":
        acc = jnp.where(pl.program_id(0) == tanh_group, jnp.tanh(acc), acc)
    elif act_out == "logsig_tau":
        acc = (jnp.minimum(acc, 0.0) - jnp.log(1.0 + jnp.exp(-jnp.abs(acc)))) * (1.0 / GLA_TAU)
    o_ref[...] = acc.astype(o_ref.dtype)


MM_X_TILE_BYTES = 9 * 1024 * 1024
MM_MAX_ROWS = 2304


def _row_tile(M, row_bytes):
    for parts in range(1, M // 8 + 1):
        rows = M // parts
        if M % parts == 0 and rows % 8 == 0 and rows <= MM_MAX_ROWS and rows * row_bytes <= MM_X_TILE_BYTES:
            return rows
    return M


def matmul(x, w, bias=None, *, x_off=0, act_in=None, act_out=None, tanh_group=0,
           out_dtype=F32, tn=512, name="mm"):
    G, K, N = w.shape
    M = x.shape[1]
    tm = _row_tile(M, K * x.dtype.itemsize)
    tn = _tile(N, tn)
    in_specs = [
        pl.BlockSpec((None, tm, K), lambda g, i, j: (g + x_off, i, 0)),
        pl.BlockSpec((None, K, tn), lambda g, i, j: (g, 0, j)),
    ]
    args = [x, w]
    if bias is not None:
        in_specs.append(pl.BlockSpec((None, 1, tn), lambda g, i, j: (g, 0, j)))
        args.append(bias)
    return pl.pallas_call(
        functools.partial(_mm_kernel, has_bias=bias is not None, act_in=act_in,
                          act_out=act_out, tanh_group=tanh_group),
        out_shape=jax.ShapeDtypeStruct((G, M, N), out_dtype),
        grid=(G, M // tm, N // tn),
        in_specs=in_specs,
        out_specs=pl.BlockSpec((None, tm, tn), lambda g, i, j: (g, i, j)),
        compiler_params=_cparams(("parallel", "parallel", "arbitrary")),
        name=name,
    )(*args)


def matmul2d(x, w, bias=None, **kw):
    b3 = None if bias is None else bias[None, None, :]
    return matmul(x[None], w[None], b3, **kw)[0]


def _hmm_kernel(x_ref, w_ref, o_ref):
    o_ref[...] = jnp.dot(x_ref[...].astype(BF16), w_ref[...],
                         preferred_element_type=F32).astype(o_ref.dtype)


def head_matmul(x, w, out_dtype, name):
    H, Kh, Nh = w.shape
    M = x.shape[0]
    tm = _tile(M, 512)
    return pl.pallas_call(
        _hmm_kernel,
        out_shape=jax.ShapeDtypeStruct((M, H * Nh), out_dtype),
        grid=(H, M // tm),
        in_specs=[pl.BlockSpec((tm, Kh), lambda h, i: (i, h)),
                  pl.BlockSpec((None, Kh, Nh), lambda h, i: (h, 0, 0))],
        out_specs=pl.BlockSpec((tm, Nh), lambda h, i: (i, h)),
        compiler_params=_cparams(("parallel", "arbitrary")),
        name=name,
    )(x, w)


class RowLayout:
    def __init__(self, B, T, DB, TS):
        self.B, self.T, self.DB, self.TS = B, T, DB, TS
        self.Mp = B * T
        self.Ms = DB * TS
        self.M = self.Mp + self.Ms
        self.te = math.gcd(math.gcd(ROW_TILE, T), self.Ms)
        self.n_p_blocks = self.Mp // self.te
        self.blocks_per_seq = T // self.te
        self.nblocks = self.M // self.te

    def mod_index(self, i):
        return jnp.where(i < self.n_p_blocks, i // self.blocks_per_seq,
                         self.B + i - self.n_p_blocks)

    def expand_mod(self, m):
        mp = jnp.broadcast_to(m[:self.B, None, :], (self.B, self.te, m.shape[-1]))
        ms = jnp.repeat(m[self.B:], self.TS, axis=0).reshape(self.Ms // self.te, self.te, -1)
        return jnp.concatenate([mp, ms], axis=0)


def _rms(x, g):
    ms = jnp.mean(x * x, axis=-1, keepdims=True)
    return x * lax.rsqrt(ms + NORM_EPS) * g


def _read_rows(refs, n_p_blocks):
    if len(refs) == 1:
        return refs[0][...]
    return jnp.where(pl.program_id(0) < n_p_blocks, refs[0][...], refs[1][...])


def _norm_step_kernel(*refs, nx, residual, prenorm, split_out, n_p_blocks):
    refs = list(refs)
    x = _read_rows(refs[:nx], n_p_blocks)
    pos = nx
    if residual:
        o_ref, gpost_ref, gate_ref = refs[pos:pos + 3]
        pos += 3
        x = x + gate_ref[...] * _rms(o_ref[...], gpost_ref[...])
    if prenorm:
        gpre_ref, shift_ref, scale_ref = refs[pos:pos + 3]
        pos += 3
    outs = refs[pos:]
    if residual:
        if split_out:
            @pl.when(pl.program_id(0) < n_p_blocks)
            def _():
                outs[0][...] = x

            @pl.when(pl.program_id(0) >= n_p_blocks)
            def _():
                outs[1][...] = x
            outs = outs[2:]
        else:
            outs[0][...] = x
            outs = outs[1:]
    if prenorm:
        h = _rms(x, gpre_ref[...]) * (1.0 + scale_ref[...]) + shift_ref[...]
        outs[0][...] = h.astype(outs[0].dtype)


def norm_step(lay, x, *, res=None, pre=None, split_out=False):
    te, npb = lay.te, lay.n_p_blocks
    row = pl.BlockSpec((te, D), lambda i: (i, 0))
    vec = pl.BlockSpec((1, D), lambda i: (0, 0))
    modcol = lambda c: pl.BlockSpec((None, te, D), lambda i: (lay.mod_index(i), 0, c))
    p_row = pl.BlockSpec((te, D), lambda i: (jnp.minimum(i, npb - 1), 0))
    s_row = pl.BlockSpec((te, D), lambda i: (jnp.maximum(i - npb, 0), 0))
    xs = list(x) if isinstance(x, tuple) else [x]
    nx = len(xs)
    in_specs = [p_row, s_row] if nx == 2 else [row]
    args = list(xs)
    out_shape, out_specs = [], []
    if res is not None:
        o, g_post, mod = res
        in_specs += [row, vec, modcol(2)]
        args += [o, g_post[None], mod]
        if split_out:
            out_shape += [jax.ShapeDtypeStruct((lay.Mp, D), F32), jax.ShapeDtypeStruct((lay.Ms, D), F32)]
            out_specs += [p_row, s_row]
        else:
            out_shape.append(jax.ShapeDtypeStruct((lay.M, D), F32))
            out_specs.append(row)
    if pre is not None:
        g_pre, mod, dtype = pre
        in_specs += [vec, modcol(0), modcol(1)]
        args += [g_pre[None], mod, mod]
        out_shape.append(jax.ShapeDtypeStruct((lay.M, D), dtype))
        out_specs.append(row)
    return pl.pallas_call(
        functools.partial(_norm_step_kernel, nx=nx, residual=res is not None,
                          prenorm=pre is not None, split_out=split_out, n_p_blocks=npb),
        out_shape=tuple(out_shape),
        grid=(lay.nblocks,),
        in_specs=in_specs,
        out_specs=tuple(out_specs),
        compiler_params=_cparams(("arbitrary",)),
        name="norm_step",
    )(*args)


RW_MIX_ORDER = (0, 1, 3, 2, 4, 5)


SUBLANES = 8


def _rw_mix_kernel(h_ref, tail_ref, first_ref, mu_ref, x_ref, *, blocks_per_seq, n_p_blocks, ts):
    i = pl.program_id(0)
    h = h_ref[...]
    rows = h.shape[0]
    ridx = lax.broadcasted_iota(jnp.int32, (rows, 1), 0)
    prev = pltpu.roll(h, 1, 0)
    tail = tail_ref[SUBLANES - 1:SUBLANES, :]
    opens = jnp.logical_and(i < n_p_blocks, i % blocks_per_seq == 0)
    prev = jnp.where(ridx == 0, jnp.where(opens, 0.0, tail), prev)
    is_sample = i >= n_p_blocks
    prev = jnp.where(jnp.logical_and(is_sample, ridx % ts == 0), first_ref[...], prev)
    dx = prev - h
    for o, c in enumerate(RW_MIX_ORDER):
        x_ref[o] = (h + dx * mu_ref[c:c + 1, :]).astype(BF16)


def rw_mix(lay, h, shift_rows, mu):
    te = lay.te
    per = te // SUBLANES
    return pl.pallas_call(
        functools.partial(_rw_mix_kernel, blocks_per_seq=lay.blocks_per_seq,
                          n_p_blocks=lay.n_p_blocks, ts=lay.TS),
        out_shape=jax.ShapeDtypeStruct((6, lay.M, D), BF16),
        grid=(lay.nblocks,),
        in_specs=[pl.BlockSpec((te, D), lambda i: (i, 0)),
                  pl.BlockSpec((SUBLANES, D), lambda i: (jnp.maximum(i * per - 1, 0), 0)),
                  pl.BlockSpec((te, D), lambda i: (jnp.maximum(i - lay.n_p_blocks, 0), 0)),
                  pl.BlockSpec((6, D), lambda i: (0, 0))],
        out_specs=pl.BlockSpec((6, te, D), lambda i: (0, i, 0)),
        compiler_params=_cparams(("parallel",)),
        name="rw_mix",
    )(h, h, shift_rows, mu)


def _head_fold(s):
    s = s + pltpu.roll(s, RW_H, 1)
    return s + pltpu.roll(s, 2 * RW_H, 1)


def _rw_prep_kernel(*refs, vres):
    if vres:
        p_ref, l_ref, par_ref, vf_ref, out_ref, v_ref, rk_ref = refs
        iw, ia = 1, 2
    else:
        p_ref, l_ref, par_ref, out_ref, v_ref, rk_ref = refs
        iw, ia = 0, 1
    rows = out_ref.shape[1]
    n2 = jnp.zeros((rows, LANE), F32)
    rk = jnp.zeros((rows, LANE), F32)
    for j in range(RW_TILES):
        js = slice(j * LANE, (j + 1) * LANE)
        par = lambda i: par_ref[i:i + 1, js]
        r = p_ref[0, :, js]
        k = p_ref[1, :, js]
        v = p_ref[3, :, js]
        decay = jnp.exp(-EXP_M05 * _sigmoid(par(0) + l_ref[iw, :, js]))
        a = _sigmoid(par(1) + l_ref[ia, :, js])
        kk = k * par(2)
        n2 = n2 + kk * kk
        kmod = k * (1.0 + (a - 1.0) * par(3))
        rk = rk + r * kmod * par(4)
        if vres:
            v = v + (vf_ref[:, js] - v) * _sigmoid(par(5) + l_ref[0, :, js])
        out_ref[0, :, js] = decay
        out_ref[1, :, js] = kmod
        out_ref[2, :, js] = kk
        out_ref[3, :, js] = a
        v_ref[:, js] = v
    inv = 1.0 / jnp.maximum(jnp.sqrt(_head_fold(n2)), 1e-12)
    rk_ref[...] = _head_fold(rk)
    for j in range(RW_TILES):
        js = slice(j * LANE, (j + 1) * LANE)
        kkn = out_ref[2, :, js] * inv
        out_ref[2, :, js] = kkn
        out_ref[3, :, js] = kkn * out_ref[3, :, js]


def rw_prep(lay, proj, lora, params, v_first):
    te = lay.te
    vres = v_first is not None
    nl = lora.shape[0]
    in_specs = [pl.BlockSpec((4, te, D), lambda i: (0, i, 0)),
                pl.BlockSpec((nl, te, D), lambda i: (0, i, 0)),
                pl.BlockSpec(params.shape, lambda i: (0, 0))]
    args = [proj, lora, params]
    if vres:
        in_specs.append(pl.BlockSpec((te, D), lambda i: (i, 0)))
        args.append(v_first)
    return pl.pallas_call(
        functools.partial(_rw_prep_kernel, vres=vres),
        out_shape=(jax.ShapeDtypeStruct((4, lay.M, D), F32),
                   jax.ShapeDtypeStruct((lay.M, D), F32),
                   jax.ShapeDtypeStruct((lay.M, LANE), F32)),
        grid=(lay.nblocks,),
        in_specs=in_specs,
        out_specs=(pl.BlockSpec((4, te, D), lambda i: (0, i, 0)),
                   pl.BlockSpec((te, D), lambda i: (i, 0)),
                   pl.BlockSpec((te, LANE), lambda i: (i, 0))),
        compiler_params=_cparams(("parallel",)),
        name="rw_prep",
    )(*args)


def _wkv_kernel(r_a, r_b, s_a, s_b, v_a, v_b, rep_ref, o_a, o_b, st_ref, vrep_a, vrep_b, *, steps):
    @pl.when(pl.program_id(0) == 0)
    def _():
        st_ref[...] = jnp.zeros(st_ref.shape, F32)

    seqs = ((r_a, s_a, v_a, o_a, vrep_a), (r_b, s_b, v_b, o_b, vrep_b))
    for (_, _, v_ref, _, vrep) in seqs:
        y = v_ref[...].reshape(steps * RW_TILES, LANE)
        hi = y.astype(BF16)
        lo = (y - hi.astype(F32)).astype(BF16)
        for g in range(RW_FOLD):
            yr = (jnp.dot(hi, rep_ref[g], preferred_element_type=F32)
                  + jnp.dot(lo, rep_ref[g], preferred_element_type=F32))
            vrep[:, g * RW_TILES:(g + 1) * RW_TILES, :] = yr.reshape(steps, RW_TILES, LANE)

    lane_group = lax.broadcasted_iota(jnp.int32, (RW_TILES, LANE), 1) // RW_H

    def kk_dot(q, s_ref):
        p = jnp.zeros((RW_N, LANE), F32)
        for j in range(RW_TILES):
            p = p + st_ref[q, j] * s_ref[2, 0:1, j * LANE:(j + 1) * LANE]
        return p

    def step(t, carry):
        t_next = jnp.minimum(t + 1, steps - 1)
        out, raw = [], []
        for q, (r_ref, s_ref, _, o_ref, vrep) in enumerate(seqs):
            sa = -(carry[q] if q == 0 else _head_fold(carry[q]))
            vr = vrep[t]
            w_row = s_ref[0, pl.ds(t, 1), :]
            k_row = s_ref[1, pl.ds(t, 1), :]
            b_row = s_ref[3, pl.ds(t, 1), :]
            r_row = r_ref[pl.ds(t, 1), :]
            kk_row = s_ref[2, pl.ds(t_next, 1), :]
            o = jnp.zeros((RW_N, LANE), F32)
            p = jnp.zeros((RW_N, LANE), F32)
            for j in range(RW_TILES):
                js = slice(j * LANE, (j + 1) * LANE)
                sn = st_ref[q, j] * w_row[:, js] + sa * b_row[:, js] + vr * k_row[:, js]
                st_ref[q, j] = sn
                o = o + sn * r_row[:, js]
                p = p + sn * kk_row[:, js]
            out.append(_head_fold(p) if q == 0 else p)
            emit(q, jnp.maximum(t - 1, 0), carry[2 + q])
            raw.append(o)
        return tuple(out + raw)

    def emit(q, t, o):
        o_ref = seqs[q][3]
        o = _head_fold(o)
        res = o[0:RW_TILES]
        for g in range(1, RW_FOLD):
            res = jnp.where(lane_group == g, o[g * RW_TILES:(g + 1) * RW_TILES], res)
        o_ref[t] = res

    init = tuple(kk_dot(q, s[1]) for q, s in enumerate(seqs))
    zero = jnp.zeros((RW_N, LANE), F32)
    last = lax.fori_loop(0, steps, step, (_head_fold(init[0]), init[1], zero, zero))
    for q in range(2):
        emit(q, steps - 1, last[2 + q])


def wkv_prompt(proj4, stack4, v, T):
    M = v.shape[0]
    tb = _tile(T, 128)
    nc = T // tb
    v3 = v.reshape(M, RW_TILES, LANE)
    lane = np.arange(LANE)
    rep = jnp.asarray(np.stack([(lane[:, None] // RW_H == g) & (lane[:, None] % RW_H == lane[None, :] % RW_H)
                                for g in range(RW_FOLD)]), BF16)
    rspec = lambda s: pl.BlockSpec((None, tb, D), lambda c: (0, s * nc + c, 0))
    sspec = lambda s: pl.BlockSpec((4, tb, D), lambda c: (0, s * nc + c, 0))
    vspec = lambda s: pl.BlockSpec((tb, RW_TILES, LANE), lambda c: (s * nc + c, 0, 0))
    ospec = pl.BlockSpec((tb, RW_TILES, LANE), lambda c: (c, 0, 0))
    return pl.pallas_call(
        functools.partial(_wkv_kernel, steps=tb),
        out_shape=(jax.ShapeDtypeStruct((T, RW_TILES, LANE), F32),
                   jax.ShapeDtypeStruct((T, RW_TILES, LANE), F32),
                   jax.ShapeDtypeStruct((2, RW_TILES, RW_N, LANE), F32)),
        grid=(nc,),
        in_specs=[rspec(0), rspec(1), sspec(0), sspec(1), vspec(0), vspec(1),
                  pl.BlockSpec((RW_FOLD, LANE, LANE), lambda c: (0, 0, 0))],
        out_specs=(ospec, ospec, pl.BlockSpec((2, RW_TILES, RW_N, LANE), lambda c: (0, 0, 0, 0))),
        scratch_shapes=[pltpu.VMEM((tb, RW_N, LANE), F32),
                        pltpu.VMEM((tb, RW_N, LANE), F32)],
        compiler_params=_cparams(("arbitrary",)),
        name="wkv",
    )(proj4, proj4, stack4, stack4, v3, v3, rep)


def _wkv_sample_kernel(x_ref, s0_ref, *rest, steps):
    o_ref, st_ref = rest[-2:]

    def row(v, carry):
        s = s0_ref[v]
        for t in range(steps):
            p = jnp.sum(s * x_ref[3, t], axis=0, keepdims=True)
            vt = x_ref[5, t, pl.ds(v, 1), :]
            s = s * x_ref[1, t] - p * x_ref[4, t] + vt * x_ref[2, t]
            o_ref[t, pl.ds(v, 1), :] = jnp.sum(s * x_ref[0, t], axis=0, keepdims=True)
        st_ref[v] = s
        return carry

    lax.fori_loop(0, RW_N, row, 0, unroll=4)


def wkv_sample(x6, state_t, layer, into=None):
    _, TS, H, _, DB = x6.shape
    L = state_t.shape[0]
    st_spec = pl.BlockSpec((None, None, RW_N, RW_N, DB), lambda h: (layer, h, 0, 0, 0))
    in_specs = [pl.BlockSpec((6, TS, None, RW_N, DB), lambda h: (0, 0, h, 0, 0)), st_spec]
    args = [x6, state_t]
    aliases = {}
    if into is not None:
        aliases = {2: 1}
        in_specs.append(pl.BlockSpec(memory_space=pl.ANY))
        args.append(into)
    return pl.pallas_call(
        functools.partial(_wkv_sample_kernel, steps=TS),
        out_shape=(jax.ShapeDtypeStruct((TS, H, RW_N, DB), F32),
                   jax.ShapeDtypeStruct((L, H, RW_N, RW_N, DB), F32)),
        grid=(H,),
        in_specs=in_specs,
        out_specs=(pl.BlockSpec((TS, None, RW_N, DB), lambda h: (0, h, 0, 0)), st_spec),
        input_output_aliases=aliases,
        compiler_params=_cparams(("parallel",)),
        name="wkv_sample",
    )(*args)


def _rw_post_kernel(oa_ref, ob_ref, os_ref, v_ref, rk_ref, z_ref, ln_ref, y_ref, o_scr, *, n_seq_blocks):
    i = pl.program_id(0)
    rows = y_ref.shape[0]
    in_a = i < n_seq_blocks
    in_b = jnp.logical_and(i >= n_seq_blocks, i < 2 * n_seq_blocks)
    s = jnp.zeros((rows, LANE), F32)
    for j in range(RW_TILES):
        js = slice(j * LANE, (j + 1) * LANE)
        oj = jnp.where(in_a, oa_ref[:, j, :], jnp.where(in_b, ob_ref[:, j, :], os_ref[:, js]))
        o_scr[:, js] = oj
        s = s + oj
    mean = _head_fold(s) * (1.0 / RW_N)
    s2 = jnp.zeros((rows, LANE), F32)
    for j in range(RW_TILES):
        d = o_scr[:, j * LANE:(j + 1) * LANE] - mean
        s2 = s2 + d * d
    rstd = lax.rsqrt(_head_fold(s2) * (1.0 / RW_N) + RW_GN_EPS)
    rk = rk_ref[...]
    for j in range(RW_TILES):
        js = slice(j * LANE, (j + 1) * LANE)
        o = (o_scr[:, js] - mean) * rstd * ln_ref[0:1, js] + ln_ref[1:2, js]
        o = o + rk * v_ref[:, js]
        y_ref[:, js] = (o * _silu(z_ref[:, js])).astype(BF16)


def rw_post(lay, o_a, o_b, o_s, v, rk, proj4, ln):
    te = lay.te
    row = pl.BlockSpec((te, D), lambda i: (i, 0))
    nsb = lay.blocks_per_seq
    npb = lay.n_p_blocks
    a_idx = lambda i: jnp.where(i < nsb, i, 0)
    b_idx = lambda i: jnp.where(jnp.logical_and(i >= nsb, i < 2 * nsb), i - nsb, 0)
    s_idx = lambda i: jnp.maximum(i - npb, 0)
    return pl.pallas_call(
        functools.partial(_rw_post_kernel, n_seq_blocks=nsb),
        out_shape=jax.ShapeDtypeStruct((lay.M, D), BF16),
        grid=(lay.nblocks,),
        in_specs=[pl.BlockSpec((te, RW_TILES, LANE), lambda i: (a_idx(i), 0, 0)),
                  pl.BlockSpec((te, RW_TILES, LANE), lambda i: (b_idx(i), 0, 0)),
                  pl.BlockSpec((te, D), lambda i: (s_idx(i), 0)),
                  row,
                  pl.BlockSpec((te, LANE), lambda i: (i, 0)),
                  pl.BlockSpec((None, te, D), lambda i: (2, i, 0)),
                  pl.BlockSpec((2, D), lambda i: (0, 0))],
        out_specs=row,
        scratch_shapes=[pltpu.VMEM((te, D), F32)],
        compiler_params=_cparams(("parallel",)),
        name="rw_post",
    )(o_a, o_b, o_s, v, rk, proj4, ln)


def _head_minor(a):
    lead = a.shape[:-1]
    return a.reshape(lead + (RW_H, RW_N)).swapaxes(-1, -2).reshape(lead + (D,))


def _tiles_to_state(s):
    n = s.shape[0]
    s = s.reshape(n, RW_TILES, RW_FOLD, RW_TILES, RW_FOLD, RW_H)
    return s.transpose(0, 5, 3, 2, 1, 4).reshape(n, RW_H, RW_N, RW_N)


def _pad_cols(w, n):
    return jnp.pad(w, ((0, 0), (0, n - w.shape[1])))


def _pad_rows(w, n):
    return jnp.pad(w, ((0, n - w.shape[0]), (0, 0)))


def rwkv_layer(lay, h, shift_state, state_t, layer, s_states, v_first, w):
    B, T, DB, TS, Mp, M = lay.B, lay.T, lay.DB, lay.TS, lay.Mp, lay.M
    x6 = rw_mix(lay, h, jnp.repeat(shift_state, TS, axis=0), w["mu"])
    w_in = jnp.stack([_head_minor(w["w_in"][c]).astype(BF16) for c in (0, 1, 3, 2)])
    proj4 = matmul(x6, w_in, name="rw_proj")
    vres = v_first is not None
    l1 = [w["v1"]] if vres else []
    l1 += [w["w1"], w["a1"]]
    l2 = [w["v2"]] if vres else []
    l2 += [w["w2"], w["a2"]]
    l1 = jnp.stack([_pad_cols(a, RW_LORA_PAD) for a in l1]).astype(BF16)
    l2 = jnp.stack([_head_minor(_pad_rows(a, RW_LORA_PAD)) for a in l2]).astype(BF16)
    lo1 = matmul(x6, l1, x_off=3 if vres else 4, act_out="tanh_group",
                 tanh_group=1 if vres else 0, out_dtype=BF16, name="rw_lora1")
    lora = matmul(lo1, l2, name="rw_lora2")
    plist = [w["w0"], w["a0"], w["k_k"], w["k_a"], w["r_k"].reshape(-1)]
    if vres:
        plist.append(w["v0"])
    plist += [jnp.zeros((D,), F32)] * (8 - len(plist))
    params = _head_minor(jnp.stack(plist))
    stack4, v, rk = rw_prep(lay, proj4, lora, params, v_first)
    v_first_out = v_first if vres else v

    assert B == 2
    o_a, o_b, p_st = wkv_prompt(proj4, stack4, v, T)
    xs = jnp.concatenate([proj4[0:1, Mp:], stack4[:, Mp:], v[None, Mp:]], axis=0)
    xs = xs.reshape(6, DB, TS, RW_N, RW_H).transpose(0, 2, 4, 3, 1)
    o_s, s_states = wkv_sample(xs, state_t, layer, s_states)
    o_s = o_s.transpose(3, 0, 2, 1).reshape(lay.Ms, D)

    ln = _head_minor(jnp.stack([w["ln_w"], w["ln_b"]]))
    y = rw_post(lay, o_a, o_b, o_s, v, rk, proj4, ln)
    w_out = w["w_out"].reshape(RW_H, RW_N, D).swapaxes(0, 1).reshape(D, D)
    o_proj = matmul2d(y, w_out.astype(BF16), name="rw_out")
    p_shift = h[T - 1:Mp:T]
    s_shift = h[Mp:].reshape(DB, TS, D)[:, -1]
    return o_proj, _tiles_to_state(p_st), p_shift, s_states, s_shift, v_first_out


def _rot_cols(w):
    half = MLA_ROPE // 2
    return jnp.concatenate([-w[..., half:], w[..., :half]], axis=-1)


def _mla_prep_kernel(pq_ref, pkv_ref, pkr_ref, gq_ref, gkv_ref, cs_ref,
                     cq_ref, ckv_ref, kr_ref, krb_ref):
    cq_ref[...] = _rms(pq_ref[...], gq_ref[...]).astype(BF16)
    ckv_ref[...] = _rms(pkv_ref[...], gkv_ref[...])
    t2 = pkr_ref[...]
    kr = t2 * cs_ref[0] + pltpu.roll(t2, MLA_ROPE, 1) * cs_ref[1]
    kr_ref[...] = kr[:, :MLA_ROPE]
    krb_ref[...] = kr.astype(BF16)


def mla_prep(lay, proj, g_q, g_kv, cs):
    te = lay.te
    M = lay.M
    return pl.pallas_call(
        _mla_prep_kernel,
        out_shape=(jax.ShapeDtypeStruct((M, MLA_QL), BF16),
                   jax.ShapeDtypeStruct((M, MLA_KVL), F32),
                   jax.ShapeDtypeStruct((M, MLA_ROPE), F32),
                   jax.ShapeDtypeStruct((M, LANE), BF16)),
        grid=(lay.nblocks,),
        in_specs=[pl.BlockSpec((te, MLA_QL), lambda i: (i, D // MLA_QL)),
                  pl.BlockSpec((te, MLA_KVL), lambda i: (i, D // MLA_KVL + 1)),
                  pl.BlockSpec((te, LANE), lambda i: (i, (D + MLA_QL + MLA_KVL) // LANE)),
                  pl.BlockSpec((1, MLA_QL), lambda i: (0, 0)),
                  pl.BlockSpec((1, MLA_KVL), lambda i: (0, 0)),
                  pl.BlockSpec((2, te, LANE), lambda i: (0, i, 0))],
        out_specs=(pl.BlockSpec((te, MLA_QL), lambda i: (i, 0)),
                   pl.BlockSpec((te, MLA_KVL), lambda i: (i, 0)),
                   pl.BlockSpec((te, MLA_ROPE), lambda i: (i, 0)),
                   pl.BlockSpec((te, LANE), lambda i: (i, 0))),
        compiler_params=_cparams(("parallel",)),
        name="mla_prep",
    )(proj, proj, proj, g_q[None], g_kv[None], cs)


def _mla_q_kernel(q_ref, cs_ref, qn_ref, qr_ref):
    for h in range(MLA_H):
        qn_ref[:, h * LANE:(h + 1) * LANE] = q_ref[:, 2 * h * LANE:(2 * h + 1) * LANE].astype(BF16)
        t2 = q_ref[:, (2 * h + 1) * LANE:(2 * h + 2) * LANE]
        qr = t2 * cs_ref[0] + pltpu.roll(t2, MLA_ROPE, 1) * cs_ref[1]
        qr_ref[:, h * LANE:(h + 1) * LANE] = qr.astype(BF16)


def mla_q(lay, q, cs):
    te = lay.te
    M = lay.M
    return pl.pallas_call(
        _mla_q_kernel,
        out_shape=(jax.ShapeDtypeStruct((M, MLA_H * LANE), BF16),
                   jax.ShapeDtypeStruct((M, MLA_H * LANE), BF16)),
        grid=(lay.nblocks,),
        in_specs=[pl.BlockSpec((te, MLA_H * 2 * LANE), lambda i: (i, 0)),
                  pl.BlockSpec((2, te, LANE), lambda i: (0, i, 0))],
        out_specs=(pl.BlockSpec((te, MLA_H * LANE), lambda i: (i, 0)),
                   pl.BlockSpec((te, MLA_H * LANE), lambda i: (i, 0))),
        compiler_params=_cparams(("parallel",)),
        name="mla_q",
    )(q, cs)


FLASH_HEADS = 2


def _flash_kernel(qn_ref, qr_ref, kn_ref, kr_ref, v_ref, o_ref, *, tq, tk):
    qi = pl.program_id(2)
    nt = (((1,), (1,)), ((), ()))
    hd = lambda u: slice(u * LANE, (u + 1) * LANE)
    qs = [jnp.concatenate([qn_ref[:, hd(u)], qr_ref[:, hd(u)]], axis=1) for u in range(FLASH_HEADS)]
    c = MLA_SCALE * math.log2(math.e)
    per_q = tq // tk

    def update(start, carry, diag):
        kr = kr_ref[pl.ds(start, tk), :]
        ss = [lax.dot_general(qs[u], jnp.concatenate([kn_ref[pl.ds(start, tk), hd(u)], kr], axis=1),
                              nt, preferred_element_type=F32) * c for u in range(FLASH_HEADS)]
        out = []
        for u in range(FLASH_HEADS):
            m, l, acc = carry[u]
            s = ss[u]
            if diag is not None:
                s = jnp.where(lax.broadcasted_iota(jnp.int32, (tq, tk), 1) + diag * tk
                              <= lax.broadcasted_iota(jnp.int32, (tq, tk), 0), s, -1e30)
            m_new = jnp.maximum(m, jnp.max(s, axis=1, keepdims=True))
            p = jnp.exp2(s - m_new)
            alpha = jnp.exp2(m - m_new)
            l = alpha * l + jnp.sum(p, axis=1, keepdims=True)
            acc = alpha * acc + jnp.dot(p.astype(BF16), v_ref[pl.ds(start, tk), hd(u)],
                                        preferred_element_type=F32)
            out.append((m_new, l, acc))
        return tuple(out)

    def body(ki, carry):
        return update(pl.multiple_of(ki * tk, tk), carry, None)

    init = (jnp.full((tq, 1), -1e30, F32), jnp.zeros((tq, 1), F32), jnp.zeros((tq, MLA_V), F32))
    carry = lax.fori_loop(0, qi * per_q, body, (init,) * FLASH_HEADS)
    for d in range(per_q):
        carry = update(pl.multiple_of(qi * tq + d * tk, tk), carry, d)
    for u in range(FLASH_HEADS):
        _, l, acc = carry[u]
        o_ref[:, hd(u)] = acc / l


def mla_flash(qn, qr, knv, krb, B, T):
    tq = _tile(T, 512)
    tk = tq
    nq = T // tq
    hw = FLASH_HEADS * LANE
    groups = MLA_H // FLASH_HEADS
    return pl.pallas_call(
        functools.partial(_flash_kernel, tq=tq, tk=tk),
        out_shape=jax.ShapeDtypeStruct((B * T, MLA_H * MLA_V), F32),
        grid=(B, groups, nq),
        in_specs=[pl.BlockSpec((tq, hw), lambda b, h, i: (b * nq + i, h)),
                  pl.BlockSpec((tq, hw), lambda b, h, i: (b * nq + i, h)),
                  pl.BlockSpec((T, hw), lambda b, h, i: (b, h)),
                  pl.BlockSpec((T, LANE), lambda b, h, i: (b, 0)),
                  pl.BlockSpec((T, hw), lambda b, h, i: (b, groups + h))],
        out_specs=pl.BlockSpec((tq, hw), lambda b, h, i: (b * nq + i, h)),
        compiler_params=_cparams(("parallel", "parallel", "arbitrary")),
        name="mla_flash",
    )(qn, qr, knv, krb, knv)


def _paged_kernel(pt_ref, ql_ref, qr_ref, cn_ref, kn_ref, *rest, pg, ts):
    ck_refs = rest[:pg]
    kr_refs = rest[pg:2 * pg]
    o_ref, m_ref, l_ref, acc_ref = rest[2 * pg:]
    p = pl.program_id(1)
    nt = (((1,), (1,)), ((), ()))

    @pl.when(p == 0)
    def _():
        m_ref[...] = jnp.full(m_ref.shape, -1e30, F32)
        l_ref[...] = jnp.zeros(l_ref.shape, F32)
        acc_ref[...] = jnp.zeros(acc_ref.shape, F32)

    rows = ts * MLA_H
    ql = ql_ref[...]
    qr = qr_ref[...][:, :MLA_ROPE]
    cks = [r[...].astype(BF16) for r in ck_refs]
    ck = jnp.concatenate(cks, axis=0)
    kr_t = jnp.concatenate([r[...].astype(BF16) for r in kr_refs], axis=1)
    s = jnp.concatenate([lax.dot_general(ql, c, nt, preferred_element_type=F32) for c in cks], axis=1)
    s = (s + jnp.dot(qr, kr_t, preferred_element_type=F32)) * MLA_SCALE
    m = m_ref[...]
    m_new = jnp.maximum(m, jnp.max(s, axis=1, keepdims=True))
    pr = jnp.exp(s - m_new)
    alpha = jnp.exp(m - m_new)
    l = alpha * l_ref[...] + jnp.sum(pr, axis=1, keepdims=True)
    acc = alpha * acc_ref[...] + jnp.dot(pr.astype(BF16), ck, preferred_element_type=F32)
    m = m_new
    m_ref[...] = m
    l_ref[...] = l
    acc_ref[...] = acc

    @pl.when(p == pl.num_programs(1) - 1)
    def _():
        qlf = ql.astype(F32)
        qrf = qr.astype(F32)
        tok = lax.broadcasted_iota(jnp.int32, (ts * MLA_H, 1), 0) // MLA_H
        cols = []
        for j in range(ts):
            cn = cn_ref[j:j + 1, :].astype(BF16).astype(F32)
            kn = kn_ref[j:j + 1, :MLA_ROPE].astype(F32)
            sj = (jnp.sum(qlf * cn, axis=1, keepdims=True)
                  + jnp.sum(qrf * kn, axis=1, keepdims=True)) * MLA_SCALE
            cols.append(jnp.where(tok >= j, sj, -1e30))
        m2 = m
        for sj in cols:
            m2 = jnp.maximum(m2, sj)
        alpha = jnp.exp(m - m2)
        l2 = alpha * l
        acc2 = alpha * acc
        for j, sj in enumerate(cols):
            pj = jnp.exp(sj - m2)
            l2 = l2 + pj
            acc2 = acc2 + pj.astype(BF16).astype(F32) * cn_ref[j:j + 1, :].astype(BF16).astype(F32)
        o_ref[...] = (acc2 / l2).astype(o_ref.dtype)


def mla_paged(page_table, ql, qr, ckv_new, krb_new, cache_ckv, cache_kr, layer, DB, TS):
    n_pages = page_table.shape[1]
    pg = _tile(n_pages, 32) if n_pages >= 8 else n_pages
    rows = TS * MLA_H
    pt = page_table.reshape(-1)

    def page_spec(i, shape):
        return pl.BlockSpec((None, None) + shape,
                            lambda b, p, pt_ref: (layer, pt_ref[b * n_pages + p * pg + i], 0, 0))

    cache_kr_t = jnp.swapaxes(cache_kr, 2, 3)
    per_b = lambda width: pl.BlockSpec((None, rows, width), lambda b, p, pt_ref: (b, 0, 0))
    new_b = lambda width: pl.BlockSpec((None, TS, width), lambda b, p, pt_ref: (b, 0, 0))
    grid_spec = pltpu.PrefetchScalarGridSpec(
        num_scalar_prefetch=1,
        grid=(DB, n_pages // pg),
        in_specs=[per_b(MLA_KVL), per_b(LANE), new_b(MLA_KVL), new_b(LANE)]
                 + [page_spec(i, (PAGE, MLA_KVL)) for i in range(pg)]
                 + [page_spec(i, (MLA_ROPE, PAGE)) for i in range(pg)],
        out_specs=per_b(MLA_KVL),
        scratch_shapes=[pltpu.VMEM((rows, 1), F32), pltpu.VMEM((rows, 1), F32),
                        pltpu.VMEM((rows, MLA_KVL), F32)],
    )
    return pl.pallas_call(
        functools.partial(_paged_kernel, pg=pg, ts=TS),
        out_shape=jax.ShapeDtypeStruct((DB, rows, MLA_KVL), BF16),
        grid_spec=grid_spec,
        compiler_params=_cparams(("parallel", "arbitrary")),
        name="mla_paged",
    )(pt, ql, qr, ckv_new, krb_new, *([cache_ckv] * pg), *([cache_kr_t] * pg))


def _pair_specs(lay):
    te, npb = lay.te, lay.n_p_blocks
    return [pl.BlockSpec((te, D), lambda i: (jnp.minimum(i, npb - 1), 0)),
            pl.BlockSpec((te, D), lambda i: (jnp.maximum(i - npb, 0), 0))]


def _gate_kernel(op_ref, os_ref, z_ref, y_ref, *, n_p_blocks):
    o = _read_rows((op_ref, os_ref), n_p_blocks)
    y_ref[...] = (o * _silu(z_ref[...])).astype(BF16)


def gate_mul(lay, o_pair, zsrc, zcol):
    te = lay.te
    return pl.pallas_call(
        functools.partial(_gate_kernel, n_p_blocks=lay.n_p_blocks),
        out_shape=jax.ShapeDtypeStruct((lay.M, D), BF16),
        grid=(lay.nblocks,),
        in_specs=_pair_specs(lay) + [pl.BlockSpec((te, D), lambda i: (i, zcol))],
        out_specs=pl.BlockSpec((te, D), lambda i: (i, 0)),
        compiler_params=_cparams(("parallel",)),
        name="gate_mul",
    )(*o_pair, zsrc)


def mla_layer(lay, h, cache_ckv, cache_kr, layer, page_table, w):
    B, T, DB, TS, Mp, M = lay.B, lay.T, lay.DB, lay.TS, lay.Mp, lay.M
    w_in = w["w_in"]
    w_kr = w_in[:, 2 * MLA_QL:2 * MLA_QL + MLA_ROPE]
    w1 = jnp.concatenate([w_in[:, 2 * MLA_QL + MLA_ROPE:], w_in[:, :2 * MLA_QL],
                          w_kr, _rot_cols(w_kr)], axis=1)
    proj = matmul2d(h, w1.astype(BF16), tn=640, name="mla_proj")
    pos = jnp.concatenate([jnp.tile(jnp.arange(T, dtype=F32), B),
                           jnp.tile(page_table.shape[1] * PAGE + jnp.arange(TS, dtype=F32), DB)])
    half = MLA_ROPE // 2
    inv_freq = ROPE_THETA ** (-jnp.arange(half, dtype=F32) / half)
    ang = pos[:, None] * inv_freq[None, :]
    zeros = jnp.zeros((M, MLA_ROPE), F32)
    cs = jnp.stack([jnp.concatenate([jnp.cos(ang), jnp.cos(ang), zeros], axis=1),
                    jnp.concatenate([jnp.sin(ang), jnp.sin(ang), zeros], axis=1)])
    cq, ckv, kr, krb = mla_prep(lay, proj, w["q_norm"], w["kv_norm"], cs)
    w_uq = w["w_uq"]
    wq = jnp.concatenate([w_uq, _rot_cols(w_uq[..., MLA_NOPE:])], axis=-1)
    q = matmul2d(cq, wq.reshape(MLA_QL, MLA_H * 2 * LANE).astype(BF16), name="mla_qproj")
    qn, qr = mla_q(lay, q, cs)
    w_kv = jnp.concatenate([w["w_uk"].reshape(MLA_KVL, -1), w["w_uv"].reshape(MLA_KVL, -1)], axis=1)
    knv = matmul2d(ckv[:Mp], w_kv.astype(BF16), out_dtype=BF16, name="mla_kv")
    o_p = mla_flash(qn, qr, knv, krb, B, T)
    w_ukT = w["w_uk"].transpose(1, 2, 0).astype(BF16)
    ql = head_matmul(qn[Mp:], w_ukT, BF16, "mla_qlat")
    o_lat = mla_paged(page_table, ql.reshape(DB, TS * MLA_H, MLA_KVL),
                      qr[Mp:].reshape(DB, TS * MLA_H, LANE),
                      ckv[Mp:].reshape(DB, TS, MLA_KVL), krb[Mp:].reshape(DB, TS, LANE),
                      cache_ckv, cache_kr, layer, DB, TS)
    o_s = head_matmul(o_lat.reshape(lay.Ms, MLA_H * MLA_KVL),
                      w["w_uv"].transpose(1, 0, 2).astype(BF16), F32, "mla_ouv")
    y = gate_mul(lay, (o_p, o_s), proj, 0)
    o_proj = matmul2d(y, w["w_out"].astype(BF16), name="mla_out")
    return (o_proj, ckv[:Mp].reshape(B, T, MLA_KVL), kr[:Mp].reshape(B, T, MLA_ROPE),
            ckv[Mp:].reshape(DB, TS, MLA_KVL), kr[Mp:].reshape(DB, TS, MLA_ROPE))


def _gla_kernel(q_ref, k_ref, v_ref, g_ref, s0_ref, o_ref, sT_ref, st_ref, *, L):
    c = pl.program_id(1)
    C = GLA_SUB
    assert L % C == 0
    heads = range(GLA_H)
    dk = lambda h: slice(h * GLA_DK, (h + 1) * GLA_DK)
    dv = lambda h: slice(h * GLA_DV, (h + 1) * GLA_DV)

    @pl.when(c == 0)
    def _():
        for h in heads:
            st_ref[h] = s0_ref[h].T

    nt = (((1,), (1,)), ((), ()))
    tn = (((0,), (0,)), ((), ()))
    trow = lax.broadcasted_iota(jnp.int32, (L, 1), 0)
    crow = lax.broadcasted_iota(jnp.int32, (C, 1), 0)
    col = lax.broadcasted_iota(jnp.int32, (C, L), 1)
    q = [q_ref[:, dk(h)] * (GLA_DK ** -0.5) for h in heads]
    k = [k_ref[:, dk(h)] for h in heads]
    v = [v_ref[:, dv(h)].astype(BF16) for h in heads]
    b = []
    for h in heads:
        bh = g_ref[:, dk(h)]
        sh = 1
        while sh < L:
            bh = bh + jnp.where(trow >= sh, pltpu.roll(bh, sh, 0), 0.0)
            sh *= 2
        b.append(bh)
    st = [st_ref[h] for h in heads]
    o = [lax.dot_general((q[h] * jnp.exp(b[h])).astype(BF16), st[h].astype(BF16), nt,
                         preferred_element_type=F32) for h in heads]
    att_rows = [[] for _ in heads]
    for i in range(L // C):
        lo = i * C
        if i == 0:
            att_i = [jnp.zeros((C, L), F32) for _ in heads]
        else:
            att_i = []
            for h in heads:
                beta = b[h][lo - 1:lo, :]
                q_in = (q[h][lo:lo + C] * jnp.exp(b[h][lo:lo + C] - beta)).astype(BF16)
                k_out = (k[h] * jnp.exp(jnp.where(trow < lo, beta - b[h], -jnp.inf))).astype(BF16)
                att_i.append(lax.dot_general(q_in, k_out, nt, preferred_element_type=F32))
        for s in range(C):
            for h in heads:
                bi = b[h][lo:lo + C]
                e = jnp.where(crow >= s, bi - bi[s:s + 1, :], -jnp.inf)
                a_s = jnp.sum(q[h][lo:lo + C] * k[h][lo + s:lo + s + 1, :] * jnp.exp(e),
                              axis=1, keepdims=True)
                att_i[h] = jnp.where(col == lo + s, a_s, att_i[h])
        for h in heads:
            att_rows[h].append(att_i[h])
    for h in heads:
        att = att_rows[h][0] if len(att_rows[h]) == 1 else jnp.concatenate(att_rows[h], axis=0)
        o_ref[:, dv(h)] = o[h] + jnp.dot(att.astype(BF16), v[h], preferred_element_type=F32)
    for h in heads:
        b_end = b[h][L - 1:L, :]
        kd = (k[h] * jnp.exp(b_end - b[h])).astype(BF16)
        st[h] = st[h] * jnp.exp(b_end) + lax.dot_general(v[h], kd, tn, preferred_element_type=F32)
        st_ref[h] = st[h]

    @pl.when(c == pl.num_programs(1) - 1)
    def _():
        for h in heads:
            sT_ref[h] = st[h].T


def gla_scan(q, k, v, g, s0, *, nseq, nchunk, L, kcol, vcol):
    rows = lambda n, c: n * nchunk + c
    st_spec = pl.BlockSpec((None, GLA_H, GLA_DK, GLA_DV), lambda n, c: (n, 0, 0, 0))
    return pl.pallas_call(
        functools.partial(_gla_kernel, L=L),
        out_shape=(jax.ShapeDtypeStruct((q.shape[0], GLA_H * GLA_DV), F32),
                   jax.ShapeDtypeStruct((nseq, GLA_H, GLA_DK, GLA_DV), F32)),
        grid=(nseq, nchunk),
        in_specs=[pl.BlockSpec((L, GLA_QK), lambda n, c: (rows(n, c), 0)),
                  pl.BlockSpec((L, GLA_QK), lambda n, c: (rows(n, c), kcol)),
                  pl.BlockSpec((L, GLA_H * GLA_DV), lambda n, c: (rows(n, c), vcol)),
                  pl.BlockSpec((L, GLA_QK), lambda n, c: (rows(n, c), 0)),
                  st_spec],
        out_specs=(pl.BlockSpec((L, GLA_H * GLA_DV), lambda n, c: (rows(n, c), 0)), st_spec),
        scratch_shapes=[pltpu.VMEM((GLA_H, GLA_DV, GLA_DK), F32)],
        compiler_params=_cparams(("parallel", "arbitrary")),
        name="gla_scan",
    )(q, k, v, g, s0)


def _gla_post_kernel(op_ref, os_ref, z_ref, g_ref, y_ref, *, n_p_blocks):
    o = _read_rows((op_ref, os_ref), n_p_blocks)
    for h in range(GLA_H):
        hs = slice(h * GLA_DV, (h + 1) * GLA_DV)
        y_ref[:, hs] = (_rms(o[:, hs], g_ref[...]) * _silu(z_ref[:, hs])).astype(BF16)


def gla_post(lay, o_pair, proj, g_norm):
    te = lay.te
    return pl.pallas_call(
        functools.partial(_gla_post_kernel, n_p_blocks=lay.n_p_blocks),
        out_shape=jax.ShapeDtypeStruct((lay.M, D), BF16),
        grid=(lay.nblocks,),
        in_specs=_pair_specs(lay) + [pl.BlockSpec((te, D), lambda i: (i, 2)),
                                     pl.BlockSpec((1, GLA_DV), lambda i: (0, 0))],
        out_specs=pl.BlockSpec((te, D), lambda i: (i, 0)),
        compiler_params=_cparams(("parallel",)),
        name="gla_post",
    )(*o_pair, proj, g_norm[None])


def gla_layer(lay, h, state, w):
    B, T, DB, TS, Mp, M = lay.B, lay.T, lay.DB, lay.TS, lay.Mp, lay.M
    w_in = _pad_cols(w["w_in"], 2 * GLA_QK + 2 * D + GLA_LORA_PAD)
    proj = matmul2d(h, w_in.astype(BF16), tn=640, name="gla_proj")
    gl = proj[:, 2 * GLA_QK + 2 * D:]
    g = matmul2d(gl, _pad_rows(w["w_g2"], GLA_LORA_PAD).astype(BF16), w["b_g"],
                 act_out="logsig_tau", name="gla_gate")
    L = math.gcd(T, GLA_CHUNK)
    cols = dict(kcol=1, vcol=2 * GLA_QK // (GLA_H * GLA_DV))
    o_p, p_state = gla_scan(proj, proj, proj, g, jnp.zeros((B, GLA_H, GLA_DK, GLA_DV), F32),
                            nseq=B, nchunk=T // L, L=L, **cols)
    LS = GLA_SUB
    pad = lambda a: jnp.pad(a.reshape(DB, TS, -1), ((0, 0), (0, LS - TS), (0, 0))).reshape(DB * LS, -1)
    ps = pad(proj[Mp:, :2 * GLA_QK + D])
    o_s, s_state = gla_scan(ps, ps, ps, pad(g[Mp:]), state, nseq=DB, nchunk=1, L=LS, **cols)
    o_s = o_s.reshape(DB, LS, D)[:, :TS].reshape(lay.Ms, D)
    y = gla_post(lay, (o_p, o_s), proj, w["norm"])
    o_proj = matmul2d(y, w["w_out"].astype(BF16), name="gla_out")
    return o_proj, p_state, s_state


def kernel(x_prompt, x_sample, c_prompt, c_sample, state_rwkv_wkv, state_rwkv_shift, cache_mla_ckv, cache_mla_krope, page_table, state_gla, norm_pre, norm_post, ada_w, ada_b, rw_mu, rw_w_in, rw_w0, rw_w1, rw_w2, rw_a0, rw_a1, rw_a2, rw_v0, rw_v1, rw_v2, rw_k_k, rw_k_a, rw_r_k, rw_ln_w, rw_ln_b, rw_w_out, mla_w_in, mla_q_norm, mla_kv_norm, mla_w_uq, mla_w_uk, mla_w_uv, mla_w_out, gla_w_in, gla_w_g2, gla_b_g, gla_norm, gla_w_out):
    B, T, _ = x_prompt.shape
    DB, TS, _ = x_sample.shape
    depth = norm_pre.shape[0]
    lay = RowLayout(B, T, DB, TS)
    Mp = lay.Mp
    x = (x_prompt.reshape(Mp, D), x_sample.reshape(lay.Ms, D))
    c = jnp.concatenate([c_prompt, c_sample], axis=0)[None]
    c = jnp.broadcast_to(c, (depth,) + c.shape[1:])
    mods = matmul(c, ada_w, ada_b[:, None, :], act_in="silu", name="ada")

    p_wkv, p_shift, p_ckv, p_kr, p_gla = [], [], [], [], []
    s_shift, s_ckv, s_kr, s_gla = [], [], [], []
    v_first = None
    state_t = jnp.transpose(state_rwkv_wkv, (0, 2, 3, 4, 1))
    s_wkv_t = None
    mod_of = [lay.expand_mod(mods[i]) for i in range(depth)]
    h_dtype = lambda i: F32 if i % 3 == 0 else BF16
    (h,) = norm_step(lay, x, pre=(norm_pre[0], mod_of[0], h_dtype(0)))
    for i in range(depth):
        kind, j = i % 3, i // 3
        if kind == 0:
            w = dict(mu=rw_mu[j], w_in=rw_w_in[j], w0=rw_w0[j], w1=rw_w1[j], w2=rw_w2[j],
                     a0=rw_a0[j], a1=rw_a1[j], a2=rw_a2[j], k_k=rw_k_k[j], k_a=rw_k_a[j],
                     r_k=rw_r_k[j], ln_w=rw_ln_w[j], ln_b=rw_ln_b[j], w_out=rw_w_out[j])
            if j > 0:
                w.update(v0=rw_v0[j - 1], v1=rw_v1[j - 1], v2=rw_v2[j - 1])
            o, pst, psh, s_wkv_t, ssh, v_first = rwkv_layer(
                lay, h, state_rwkv_shift[j], state_t, j, s_wkv_t, v_first if j > 0 else None, w)
            p_wkv.append(pst)
            p_shift.append(psh)
            s_shift.append(ssh)
        elif kind == 1:
            w = dict(w_in=mla_w_in[j], q_norm=mla_q_norm[j], kv_norm=mla_kv_norm[j],
                     w_uq=mla_w_uq[j], w_uk=mla_w_uk[j], w_uv=mla_w_uv[j], w_out=mla_w_out[j])
            o, pc, pk, sc, sk = mla_layer(lay, h, cache_mla_ckv, cache_mla_krope, j, page_table, w)
            p_ckv.append(pc)
            p_kr.append(pk)
            s_ckv.append(sc)
            s_kr.append(sk)
        else:
            w = dict(w_in=gla_w_in[j], w_g2=gla_w_g2[j], b_g=gla_b_g[j], norm=gla_norm[j],
                     w_out=gla_w_out[j])
            o, pg, sg = gla_layer(lay, h, state_gla[j], w)
            p_gla.append(pg)
            s_gla.append(sg)
        res = (o, norm_post[i], mod_of[i])
        if i + 1 < depth:
            x, h = norm_step(lay, x, res=res, pre=(norm_pre[i + 1], mod_of[i + 1], h_dtype(i + 1)))
        else:
            yp, ys = norm_step(lay, x, res=res, split_out=True)
    yp = yp.reshape(B, T, D)
    ys = ys.reshape(DB, TS, D)
    return (yp, ys,
            jnp.stack(p_wkv), jnp.stack(p_shift), jnp.stack(p_ckv), jnp.stack(p_kr), jnp.stack(p_gla),
            jnp.transpose(s_wkv_t, (0, 4, 1, 2, 3)), jnp.stack(s_shift), jnp.stack(s_ckv),
            jnp.stack(s_kr), jnp.stack(s_gla))
```

```python
import functools
import math

import jax
import jax.numpy as jnp
import numpy as np
from jax import lax
from jax.experimental import pallas as pl
from jax.experimental.pallas import tpu as pltpu

F32 = jnp.float32
BF16 = jnp.bfloat16

D = 2048
NORM_EPS = 1e-6
RW_N = 64
RW_H = D // RW_N
RW_LORA_PAD = 128
RW_GN_EPS = 64e-5
RW_TILES = D // 128
RW_FOLD = 128 // RW_H
EXP_M05 = math.exp(-0.5)
MLA_H = 16
MLA_NOPE = 128
MLA_ROPE = 64
MLA_V = 128
MLA_QL = 512
MLA_KVL = 512
MLA_SCALE = (MLA_NOPE + MLA_ROPE) ** -0.5
ROPE_THETA = 10000.0
PAGE = 128
GLA_H = 4
GLA_DK = 256
GLA_DV = 512
GLA_QK = GLA_H * GLA_DK
GLA_LORA_PAD = 256
GLA_TAU = 16.0
GLA_CHUNK = 64
GLA_SUB = 16

LANE = 128
ROW_TILE = 128
VMEM_LIMIT = 48 * 1024 * 1024


def _cparams(sem):
    return pltpu.CompilerParams(dimension_semantics=sem, vmem_limit_bytes=VMEM_LIMIT)


def _tile(n, pref):
    for t in (1024, 640, 512, 384, 256, 128, 64, 32, 16, 8):
        if t <= pref and n % t == 0:
            return t
    return n


def _sigmoid(x):
    return 1.0 / (1.0 + jnp.exp(-x))


def _silu(x):
    return x * _sigmoid(x)


def _mm_kernel(*refs, has_bias, act_in, act_out, tanh_group):
    if has_bias:
        x_ref, w_ref, b_ref, o_ref = refs
    else:
        x_ref, w_ref, o_ref = refs
    x = x_ref[...]
    if act_in == "silu":
        x = _silu(x.astype(F32))
    acc = jnp.dot(x.astype(BF16), w_ref[...].astype(BF16), preferred_element_type=F32)
    if has_bias:
        acc = acc + b_ref[...]
    if act_out == "tanh_group":
        acc = jnp.where(pl.program_id(0) == tanh_group, jnp.tanh(acc), acc)
    elif act_out == "logsig_tau":
        acc = (jnp.minimum(acc, 0.0) - jnp.log(1.0 + jnp.exp(-jnp.abs(acc)))) * (1.0 / GLA_TAU)
    o_ref[...] = acc.astype(o_ref.dtype)


MM_X_TILE_BYTES = 9 * 1024 * 1024
MM_MAX_ROWS = 2304


def _row_tile(M, row_bytes):
    for parts in range(1, M // 8 + 1):
        rows = M // parts
        if M % parts == 0 and rows % 8 == 0 and rows <= MM_MAX_ROWS and rows * row_bytes <= MM_X_TILE_BYTES:
            return rows
    return M


def matmul(x, w, bias=None, *, x_off=0, act_in=None, act_out=None, tanh_group=0,
           out_dtype=F32, tn=512, name="mm"):
    G, K, N = w.shape
    M = x.shape[1]
    tm = _row_tile(M, K * x.dtype.itemsize)
    tn = _tile(N, tn)
    in_specs = [
        pl.BlockSpec((None, tm, K), lambda g, i, j: (g + x_off, i, 0)),
        pl.BlockSpec((None, K, tn), lambda g, i, j: (g, 0, j)),
    ]
    args = [x, w]
    if bias is not None:
        in_specs.append(pl.BlockSpec((None, 1, tn), lambda g, i, j: (g, 0, j)))
        args.append(bias)
    return pl.pallas_call(
        functools.partial(_mm_kernel, has_bias=bias is not None, act_in=act_in,
                          act_out=act_out, tanh_group=tanh_group),
        out_shape=jax.ShapeDtypeStruct((G, M, N), out_dtype),
        grid=(G, M // tm, N // tn),
        in_specs=in_specs,
        out_specs=pl.BlockSpec((None, tm, tn), lambda g, i, j: (g, i, j)),
        compiler_params=_cparams(("parallel", "parallel", "arbitrary")),
        name=name,
    )(*args)


def matmul2d(x, w, bias=None, **kw):
    b3 = None if bias is None else bias[None, None, :]
    return matmul(x[None], w[None], b3, **kw)[0]


def _hmm_kernel(x_ref, w_ref, o_ref):
    o_ref[...] = jnp.dot(x_ref[...].astype(BF16), w_ref[...],
                         preferred_element_type=F32).astype(o_ref.dtype)


def head_matmul(x, w, out_dtype, name):
    H, Kh, Nh = w.shape
    M = x.shape[0]
    tm = _tile(M, 512)
    return pl.pallas_call(
        _hmm_kernel,
        out_shape=jax.ShapeDtypeStruct((M, H * Nh), out_dtype),
        grid=(H, M // tm),
        in_specs=[pl.BlockSpec((tm, Kh), lambda h, i: (i, h)),
                  pl.BlockSpec((None, Kh, Nh), lambda h, i: (h, 0, 0))],
        out_specs=pl.BlockSpec((tm, Nh), lambda h, i: (i, h)),
        compiler_params=_cparams(("parallel", "arbitrary")),
        name=name,
    )(x, w)


class RowLayout:
    def __init__(self, B, T, DB, TS):
        self.B, self.T, self.DB, self.TS = B, T, DB, TS
        self.Mp = B * T
        self.Ms = DB * TS
        self.M = self.Mp + self.Ms
        self.te = math.gcd(math.gcd(ROW_TILE, T), self.Ms)
        self.n_p_blocks = self.Mp // self.te
        self.blocks_per_seq = T // self.te
        self.nblocks = self.M // self.te

    def mod_index(self, i):
        return jnp.where(i < self.n_p_blocks, i // self.blocks_per_seq,
                         self.B + i - self.n_p_blocks)

    def expand_mod(self, m):
        mp = jnp.broadcast_to(m[:self.B, None, :], (self.B, self.te, m.shape[-1]))
        ms = jnp.repeat(m[self.B:], self.TS, axis=0).reshape(self.Ms // self.te, self.te, -1)
        return jnp.concatenate([mp, ms], axis=0)


def _rms(x, g):
    ms = jnp.mean(x * x, axis=-1, keepdims=True)
    return x * lax.rsqrt(ms + NORM_EPS) * g


def _read_rows(refs, n_p_blocks):
    if len(refs) == 1:
        return refs[0][...]
    return jnp.where(pl.program_id(0) < n_p_blocks, refs[0][...], refs[1][...])


def _norm_step_kernel(*refs, nx, residual, prenorm, split_out, n_p_blocks):
    refs = list(refs)
    x = _read_rows(refs[:nx], n_p_blocks)
    pos = nx
    if residual:
        o_ref, gpost_ref, gate_ref = refs[pos:pos + 3]
        pos += 3
        x = x + gate_ref[...] * _rms(o_ref[...], gpost_ref[...])
    if prenorm:
        gpre_ref, shift_ref, scale_ref = refs[pos:pos + 3]
        pos += 3
    outs = refs[pos:]
    if residual:
        if split_out:
            @pl.when(pl.program_id(0) < n_p_blocks)
            def _():
                outs[0][...] = x

            @pl.when(pl.program_id(0) >= n_p_blocks)
            def _():
                outs[1][...] = x
            outs = outs[2:]
        else:
            outs[0][...] = x
            outs = outs[1:]
    if prenorm:
        h = _rms(x, gpre_ref[...]) * (1.0 + scale_ref[...]) + shift_ref[...]
        outs[0][...] = h.astype(outs[0].dtype)


def norm_step(lay, x, *, res=None, pre=None, split_out=False):
    te, npb = lay.te, lay.n_p_blocks
    row = pl.BlockSpec((te, D), lambda i: (i, 0))
    vec = pl.BlockSpec((1, D), lambda i: (0, 0))
    modcol = lambda c: pl.BlockSpec((None, te, D), lambda i: (lay.mod_index(i), 0, c))
    p_row = pl.BlockSpec((te, D), lambda i: (jnp.minimum(i, npb - 1), 0))
    s_row = pl.BlockSpec((te, D), lambda i: (jnp.maximum(i - npb, 0), 0))
    xs = list(x) if isinstance(x, tuple) else [x]
    nx = len(xs)
    in_specs = [p_row, s_row] if nx == 2 else [row]
    args = list(xs)
    out_shape, out_specs = [], []
    if res is not None:
        o, g_post, mod = res
        in_specs += [row, vec, modcol(2)]
        args += [o, g_post[None], mod]
        if split_out:
            out_shape += [jax.ShapeDtypeStruct((lay.Mp, D), F32), jax.ShapeDtypeStruct((lay.Ms, D), F32)]
            out_specs += [p_row, s_row]
        else:
            out_shape.append(jax.ShapeDtypeStruct((lay.M, D), F32))
            out_specs.append(row)
    if pre is not None:
        g_pre, mod, dtype = pre
        in_specs += [vec, modcol(0), modcol(1)]
        args += [g_pre[None], mod, mod]
        out_shape.append(jax.ShapeDtypeStruct((lay.M, D), dtype))
        out_specs.append(row)
    return pl.pallas_call(
        functools.partial(_norm_step_kernel, nx=nx, residual=res is not None,
                          prenorm=pre is not None, split_out=split_out, n_p_blocks=npb),
        out_shape=tuple(out_shape),
        grid=(lay.nblocks,),
        in_specs=in_specs,
        out_specs=tuple(out_specs),
        compiler_params=_cparams(("arbitrary",)),
        name="norm_step",
    )(*args)


RW_MIX_ORDER = (0, 1, 3, 2, 4, 5)


SUBLANES = 8


def _rw_mix_kernel(h_ref, tail_ref, first_ref, mu_ref, x_ref, *, blocks_per_seq, n_p_blocks, ts):
    i = pl.program_id(0)
    h = h_ref[...]
    rows = h.shape[0]
    ridx = lax.broadcasted_iota(jnp.int32, (rows, 1), 0)
    prev = pltpu.roll(h, 1, 0)
    tail = tail_ref[SUBLANES - 1:SUBLANES, :]
    opens = jnp.logical_and(i < n_p_blocks, i % blocks_per_seq == 0)
    prev = jnp.where(ridx == 0, jnp.where(opens, 0.0, tail), prev)
    is_sample = i >= n_p_blocks
    prev = jnp.where(jnp.logical_and(is_sample, ridx % ts == 0), first_ref[...], prev)
    dx = prev - h
    for o, c in enumerate(RW_MIX_ORDER):
        x_ref[o] = (h + dx * mu_ref[c:c + 1, :]).astype(BF16)


def rw_mix(lay, h, shift_rows, mu):
    te = lay.te
    per = te // SUBLANES
    return pl.pallas_call(
        functools.partial(_rw_mix_kernel, blocks_per_seq=lay.blocks_per_seq,
                          n_p_blocks=lay.n_p_blocks, ts=lay.TS),
        out_shape=jax.ShapeDtypeStruct((6, lay.M, D), BF16),
        grid=(lay.nblocks,),
        in_specs=[pl.BlockSpec((te, D), lambda i: (i, 0)),
                  pl.BlockSpec((SUBLANES, D), lambda i: (jnp.maximum(i * per - 1, 0), 0)),
                  pl.BlockSpec((te, D), lambda i: (jnp.maximum(i - lay.n_p_blocks, 0), 0)),
                  pl.BlockSpec((6, D), lambda i: (0, 0))],
        out_specs=pl.BlockSpec((6, te, D), lambda i: (0, i, 0)),
        compiler_params=_cparams(("parallel",)),
        name="rw_mix",
    )(h, h, shift_rows, mu)


def _head_fold(s):
    s = s + pltpu.roll(s, RW_H, 1)
    return s + pltpu.roll(s, 2 * RW_H, 1)


def _rw_prep_kernel(*refs, vres):
    if vres:
        p_ref, l_ref, par_ref, vf_ref, out_ref, v_ref, rk_ref = refs
    else:
        p_ref, l_ref, par_ref, out_ref, v_ref, rk_ref = refs
        vf_ref = None
    _rw_prep_rows(p_ref, l_ref, par_ref, vf_ref, out_ref, v_ref, rk_ref)


def _rw_prep_rows(p_ref, l_ref, par_ref, vf_ref, out_ref, v_ref, rk_ref):
    vres = vf_ref is not None
    iw, ia = (1, 2) if vres else (0, 1)
    rows = out_ref.shape[1]
    n2 = jnp.zeros((rows, LANE), F32)
    rk = jnp.zeros((rows, LANE), F32)
    for j in range(RW_TILES):
        js = slice(j * LANE, (j + 1) * LANE)
        par = lambda i: par_ref[i:i + 1, js]
        r = p_ref[0, :, js]
        k = p_ref[1, :, js]
        v = p_ref[3, :, js]
        decay = jnp.exp(-EXP_M05 * _sigmoid(par(0) + l_ref[iw, :, js]))
        a = _sigmoid(par(1) + l_ref[ia, :, js])
        kk = k * par(2)
        n2 = n2 + kk * kk
        kmod = k * (1.0 + (a - 1.0) * par(3))
        rk = rk + r * kmod * par(4)
        if vres:
            v = v + (vf_ref[:, js] - v) * _sigmoid(par(5) + l_ref[0, :, js])
        out_ref[0, :, js] = decay
        out_ref[1, :, js] = kmod
        out_ref[2, :, js] = kk
        out_ref[3, :, js] = a
        v_ref[:, js] = v
    inv = 1.0 / jnp.maximum(jnp.sqrt(_head_fold(n2)), 1e-12)
    rk_ref[...] = _head_fold(rk)
    for j in range(RW_TILES):
        js = slice(j * LANE, (j + 1) * LANE)
        kkn = out_ref[2, :, js] * inv
        out_ref[2, :, js] = kkn
        out_ref[3, :, js] = kkn * out_ref[3, :, js]


def rw_prep_sample(lay, proj, lora, params, v_first_s):
    te, npb = lay.te, lay.n_p_blocks
    vres = v_first_s is not None
    nl = lora.shape[0]
    in_specs = [pl.BlockSpec((4, te, D), lambda i: (0, npb + i, 0)),
                pl.BlockSpec((nl, te, D), lambda i: (0, npb + i, 0)),
                pl.BlockSpec(params.shape, lambda i: (0, 0))]
    args = [proj, lora, params]
    if vres:
        in_specs.append(pl.BlockSpec((te, D), lambda i: (i, 0)))
        args.append(v_first_s)
    return pl.pallas_call(
        functools.partial(_rw_prep_kernel, vres=vres),
        out_shape=(jax.ShapeDtypeStruct((4, lay.Ms, D), F32),
                   jax.ShapeDtypeStruct((lay.Ms, D), F32),
                   jax.ShapeDtypeStruct((lay.Ms, LANE), F32)),
        grid=(lay.Ms // te,),
        in_specs=in_specs,
        out_specs=(pl.BlockSpec((4, te, D), lambda i: (0, i, 0)),
                   pl.BlockSpec((te, D), lambda i: (i, 0)),
                   pl.BlockSpec((te, LANE), lambda i: (i, 0))),
        compiler_params=_cparams(("parallel",)),
        name="rw_prep",
    )(*args)


def _wkv_kernel(*refs, steps, vres):
    n_in = 4 + (2 if vres else 0)
    p_a, p_b, l_a, l_b = refs[:4]
    vf_a, vf_b = (refs[4], refs[5]) if vres else (None, None)
    par_ref, rep_ref = refs[n_in:n_in + 2]
    o_a, o_b, st_ref, vout_a, vout_b, rk_a, rk_b = refs[n_in + 2:n_in + 9]
    vrep_a, vrep_b, s_scr, v3_scr = refs[n_in + 9:]

    @pl.when(pl.program_id(0) == 0)
    def _():
        st_ref[...] = jnp.zeros(st_ref.shape, F32)

    _rw_prep_rows(p_a, l_a, par_ref, vf_a, s_scr.at[0], vout_a, rk_a)
    _rw_prep_rows(p_b, l_b, par_ref, vf_b, s_scr.at[1], vout_b, rk_b)
    for q, vout in enumerate((vout_a, vout_b)):
        for j in range(RW_TILES):
            v3_scr[q, :, j, :] = vout[:, j * LANE:(j + 1) * LANE]

    r_a, r_b = p_a.at[0], p_b.at[0]
    s_a, s_b = s_scr.at[0], s_scr.at[1]
    seqs = ((r_a, s_a, v3_scr.at[0], o_a, vrep_a), (r_b, s_b, v3_scr.at[1], o_b, vrep_b))
    for (_, _, v_ref, _, vrep) in seqs:
        y = v_ref[...].reshape(steps * RW_TILES, LANE)
        hi = y.astype(BF16)
        lo = (y - hi.astype(F32)).astype(BF16)
        for g in range(RW_FOLD):
            yr = (jnp.dot(hi, rep_ref[g], preferred_element_type=F32)
                  + jnp.dot(lo, rep_ref[g], preferred_element_type=F32))
            vrep[:, g * RW_TILES:(g + 1) * RW_TILES, :] = yr.reshape(steps, RW_TILES, LANE)

    lane_group = lax.broadcasted_iota(jnp.int32, (RW_TILES, LANE), 1) // RW_H

    def kk_dot(q, s_ref):
        p = jnp.zeros((RW_N, LANE), F32)
        for j in range(RW_TILES):
            p = p + st_ref[q, j] * s_ref[2, 0:1, j * LANE:(j + 1) * LANE]
        return p

    def step(t, carry):
        t_next = jnp.minimum(t + 1, steps - 1)
        out, raw = [], []
        for q, (r_ref, s_ref, _, o_ref, vrep) in enumerate(seqs):
            sa = -(carry[q] if q == 0 else _head_fold(carry[q]))
            vr = vrep[t]
            w_row = s_ref[0, pl.ds(t, 1), :]
            k_row = s_ref[1, pl.ds(t, 1), :]
            b_row = s_ref[3, pl.ds(t, 1), :]
            r_row = r_ref[pl.ds(t, 1), :]
            kk_row = s_ref[2, pl.ds(t_next, 1), :]
            o = jnp.zeros((RW_N, LANE), F32)
            p = jnp.zeros((RW_N, LANE), F32)
            for j in range(RW_TILES):
                js = slice(j * LANE, (j + 1) * LANE)
                sn = st_ref[q, j] * w_row[:, js] + sa * b_row[:, js] + vr * k_row[:, js]
                st_ref[q, j] = sn
                o = o + sn * r_row[:, js]
                p = p + sn * kk_row[:, js]
            out.append(_head_fold(p) if q == 0 else p)
            emit(q, jnp.maximum(t - 1, 0), carry[2 + q])
            raw.append(o)
        return tuple(out + raw)

    def emit(q, t, o):
        o_ref = seqs[q][3]
        o = _head_fold(o)
        res = o[0:RW_TILES]
        for g in range(1, RW_FOLD):
            res = jnp.where(lane_group == g, o[g * RW_TILES:(g + 1) * RW_TILES], res)
        o_ref[t] = res

    init = tuple(kk_dot(q, s[1]) for q, s in enumerate(seqs))
    zero = jnp.zeros((RW_N, LANE), F32)
    last = lax.fori_loop(0, steps, step, (_head_fold(init[0]), init[1], zero, zero))
    for q in range(2):
        emit(q, steps - 1, last[2 + q])


def wkv_prompt(proj4, lora, params, v_first_p, T):
    tb = _tile(T, 64)
    nc = T // tb
    nl = lora.shape[0]
    vres = v_first_p is not None
    lane = np.arange(LANE)
    rep = jnp.asarray(np.stack([(lane[:, None] // RW_H == g) & (lane[:, None] % RW_H == lane[None, :] % RW_H)
                                for g in range(RW_FOLD)]), BF16)
    pspec = lambda s: pl.BlockSpec((4, tb, D), lambda c: (0, s * nc + c, 0))
    lspec = lambda s: pl.BlockSpec((nl, tb, D), lambda c: (0, s * nc + c, 0))
    rows = lambda width: pl.BlockSpec((tb, width), lambda c: (c, 0))
    ospec = pl.BlockSpec((tb, RW_TILES, LANE), lambda c: (c, 0, 0))
    in_specs = [pspec(0), pspec(1), lspec(0), lspec(1)]
    args = [proj4, proj4, lora, lora]
    if vres:
        in_specs += [rows(D), rows(D)]
        args += list(v_first_p)
    in_specs += [pl.BlockSpec(params.shape, lambda c: (0, 0)),
                 pl.BlockSpec((RW_FOLD, LANE, LANE), lambda c: (0, 0, 0))]
    args += [params, rep]
    o_a, o_b, st, v_a, v_b, rk_a, rk_b = pl.pallas_call(
        functools.partial(_wkv_kernel, steps=tb, vres=vres),
        out_shape=(jax.ShapeDtypeStruct((T, RW_TILES, LANE), F32),
                   jax.ShapeDtypeStruct((T, RW_TILES, LANE), F32),
                   jax.ShapeDtypeStruct((2, RW_TILES, RW_N, LANE), F32),
                   jax.ShapeDtypeStruct((T, D), F32), jax.ShapeDtypeStruct((T, D), F32),
                   jax.ShapeDtypeStruct((T, LANE), F32), jax.ShapeDtypeStruct((T, LANE), F32)),
        grid=(nc,),
        in_specs=in_specs,
        out_specs=(ospec, ospec, pl.BlockSpec((2, RW_TILES, RW_N, LANE), lambda c: (0, 0, 0, 0)),
                   rows(D), rows(D), rows(LANE), rows(LANE)),
        scratch_shapes=[pltpu.VMEM((tb, RW_N, LANE), F32),
                        pltpu.VMEM((tb, RW_N, LANE), F32),
                        pltpu.VMEM((2, 4, tb, D), F32),
                        pltpu.VMEM((2, tb, RW_TILES, LANE), F32)],
        compiler_params=_cparams(("arbitrary",)),
        name="wkv",
    )(*args)
    return o_a, o_b, st, (v_a, v_b), (rk_a, rk_b)


def _wkv_sample_kernel(x_ref, s0_ref, *rest, steps):
    o_ref, st_ref = rest[-2:]

    def row(v, carry):
        s = s0_ref[v]
        for t in range(steps):
            p = jnp.sum(s * x_ref[3, t], axis=0, keepdims=True)
            vt = x_ref[5, t, pl.ds(v, 1), :]
            s = s * x_ref[1, t] - p * x_ref[4, t] + vt * x_ref[2, t]
            o_ref[t, pl.ds(v, 1), :] = jnp.sum(s * x_ref[0, t], axis=0, keepdims=True)
        st_ref[v] = s
        return carry

    lax.fori_loop(0, RW_N, row, 0, unroll=4)


def wkv_sample(x6, state_t, layer, into=None):
    _, TS, H, _, DB = x6.shape
    L = state_t.shape[0]
    st_spec = pl.BlockSpec((None, None, RW_N, RW_N, DB), lambda h: (layer, h, 0, 0, 0))
    in_specs = [pl.BlockSpec((6, TS, None, RW_N, DB), lambda h: (0, 0, h, 0, 0)), st_spec]
    args = [x6, state_t]
    aliases = {}
    if into is not None:
        aliases = {2: 1}
        in_specs.append(pl.BlockSpec(memory_space=pl.ANY))
        args.append(into)
    return pl.pallas_call(
        functools.partial(_wkv_sample_kernel, steps=TS),
        out_shape=(jax.ShapeDtypeStruct((TS, H, RW_N, DB), F32),
                   jax.ShapeDtypeStruct((L, H, RW_N, RW_N, DB), F32)),
        grid=(H,),
        in_specs=in_specs,
        out_specs=(pl.BlockSpec((TS, None, RW_N, DB), lambda h: (0, h, 0, 0)), st_spec),
        input_output_aliases=aliases,
        compiler_params=_cparams(("parallel",)),
        name="wkv_sample",
    )(*args)


def _rw_post_kernel(oa_ref, ob_ref, os_ref, va_ref, vb_ref, vs_ref, rka_ref, rkb_ref, rks_ref,
                    z_ref, ln_ref, y_ref, o_scr, *, n_seq_blocks):
    i = pl.program_id(0)
    rows = y_ref.shape[0]
    in_a = i < n_seq_blocks
    in_b = jnp.logical_and(i >= n_seq_blocks, i < 2 * n_seq_blocks)
    pick = lambda a, b, s: jnp.where(in_a, a, jnp.where(in_b, b, s))
    s = jnp.zeros((rows, LANE), F32)
    for j in range(RW_TILES):
        js = slice(j * LANE, (j + 1) * LANE)
        oj = pick(oa_ref[:, j, :], ob_ref[:, j, :], os_ref[:, js])
        o_scr[:, js] = oj
        s = s + oj
    mean = _head_fold(s) * (1.0 / RW_N)
    s2 = jnp.zeros((rows, LANE), F32)
    for j in range(RW_TILES):
        d = o_scr[:, j * LANE:(j + 1) * LANE] - mean
        s2 = s2 + d * d
    rstd = lax.rsqrt(_head_fold(s2) * (1.0 / RW_N) + RW_GN_EPS)
    rk = pick(rka_ref[...], rkb_ref[...], rks_ref[...])
    for j in range(RW_TILES):
        js = slice(j * LANE, (j + 1) * LANE)
        o = (o_scr[:, js] - mean) * rstd * ln_ref[0:1, js] + ln_ref[1:2, js]
        o = o + rk * pick(va_ref[:, js], vb_ref[:, js], vs_ref[:, js])
        y_ref[:, js] = (o * _silu(z_ref[:, js])).astype(BF16)


def rw_post(lay, o3, v3, rk3, proj4, ln):
    te = lay.te
    row = pl.BlockSpec((te, D), lambda i: (i, 0))
    nsb = lay.blocks_per_seq
    npb = lay.n_p_blocks
    a_idx = lambda i: jnp.where(i < nsb, i, 0)
    b_idx = lambda i: jnp.where(jnp.logical_and(i >= nsb, i < 2 * nsb), i - nsb, 0)
    s_idx = lambda i: jnp.maximum(i - npb, 0)
    triple = lambda width: [pl.BlockSpec((te, width), lambda i: (a_idx(i), 0)),
                            pl.BlockSpec((te, width), lambda i: (b_idx(i), 0)),
                            pl.BlockSpec((te, width), lambda i: (s_idx(i), 0))]
    return pl.pallas_call(
        functools.partial(_rw_post_kernel, n_seq_blocks=nsb),
        out_shape=jax.ShapeDtypeStruct((lay.M, D), BF16),
        grid=(lay.nblocks,),
        in_specs=[pl.BlockSpec((te, RW_TILES, LANE), lambda i: (a_idx(i), 0, 0)),
                  pl.BlockSpec((te, RW_TILES, LANE), lambda i: (b_idx(i), 0, 0)),
                  pl.BlockSpec((te, D), lambda i: (s_idx(i), 0))]
                 + triple(D) + triple(LANE)
                 + [pl.BlockSpec((None, te, D), lambda i: (2, i, 0)),
                    pl.BlockSpec((2, D), lambda i: (0, 0))],
        out_specs=row,
        scratch_shapes=[pltpu.VMEM((te, D), F32)],
        compiler_params=_cparams(("parallel",)),
        name="rw_post",
    )(*o3, *v3, *rk3, proj4, ln)


def _head_minor(a):
    lead = a.shape[:-1]
    return a.reshape(lead + (RW_H, RW_N)).swapaxes(-1, -2).reshape(lead + (D,))


def _tiles_to_state(s):
    n = s.shape[0]
    s = s.reshape(n, RW_TILES, RW_FOLD, RW_TILES, RW_FOLD, RW_H)
    return s.transpose(0, 5, 3, 2, 1, 4).reshape(n, RW_H, RW_N, RW_N)


def _pad_cols(w, n):
    return jnp.pad(w, ((0, 0), (0, n - w.shape[1])))


def _pad_rows(w, n):
    return jnp.pad(w, ((0, n - w.shape[0]), (0, 0)))


def rwkv_layer(lay, h, shift_state, state_t, layer, s_states, v_first, w):
    B, T, DB, TS, Mp, M = lay.B, lay.T, lay.DB, lay.TS, lay.Mp, lay.M
    x6 = rw_mix(lay, h, jnp.repeat(shift_state, TS, axis=0), w["mu"])
    w_in = jnp.stack([_head_minor(w["w_in"][c]).astype(BF16) for c in (0, 1, 3, 2)])
    proj4 = matmul(x6, w_in, name="rw_proj")
    vres = v_first is not None
    l1 = [w["v1"]] if vres else []
    l1 += [w["w1"], w["a1"]]
    l2 = [w["v2"]] if vres else []
    l2 += [w["w2"], w["a2"]]
    l1 = jnp.stack([_pad_cols(a, RW_LORA_PAD) for a in l1]).astype(BF16)
    l2 = jnp.stack([_head_minor(_pad_rows(a, RW_LORA_PAD)) for a in l2]).astype(BF16)
    lo1 = matmul(x6, l1, x_off=3 if vres else 4, act_out="tanh_group",
                 tanh_group=1 if vres else 0, out_dtype=BF16, name="rw_lora1")
    lora = matmul(lo1, l2, name="rw_lora2")
    plist = [w["w0"], w["a0"], w["k_k"], w["k_a"], w["r_k"].reshape(-1)]
    if vres:
        plist.append(w["v0"])
    plist += [jnp.zeros((D,), F32)] * (8 - len(plist))
    params = _head_minor(jnp.stack(plist))
    assert B == 2
    o_a, o_b, p_st, (v_a, v_b), (rk_a, rk_b) = wkv_prompt(
        proj4, lora, params, v_first[:2] if vres else None, T)
    stack4_s, v_s, rk_s = rw_prep_sample(lay, proj4, lora, params, v_first[2] if vres else None)
    xs = jnp.concatenate([proj4[0:1, Mp:], stack4_s, v_s[None]], axis=0)
    xs = xs.reshape(6, DB, TS, RW_N, RW_H).transpose(0, 2, 4, 3, 1)
    o_s, s_states = wkv_sample(xs, state_t, layer, s_states)
    o_s = o_s.transpose(3, 0, 2, 1).reshape(lay.Ms, D)
    v3 = (v_a, v_b, v_s)
    v_first_out = v_first if vres else v3

    ln = _head_minor(jnp.stack([w["ln_w"], w["ln_b"]]))
    y = rw_post(lay, (o_a, o_b, o_s), v3, (rk_a, rk_b, rk_s), proj4, ln)
    w_out = w["w_out"].reshape(RW_H, RW_N, D).swapaxes(0, 1).reshape(D, D)
    o_proj = matmul2d(y, w_out.astype(BF16), name="rw_out")
    p_shift = h[T - 1:Mp:T]
    s_shift = h[Mp:].reshape(DB, TS, D)[:, -1]
    return o_proj, _tiles_to_state(p_st), p_shift, s_states, s_shift, v_first_out


def _rot_cols(w):
    half = MLA_ROPE // 2
    return jnp.concatenate([-w[..., half:], w[..., :half]], axis=-1)


def _mla_prep_kernel(pq_ref, pkv_ref, pkr_ref, gq_ref, gkv_ref, cs_ref,
                     cq_ref, ckv_ref, kr_ref, krb_ref):
    cq_ref[...] = _rms(pq_ref[...], gq_ref[...]).astype(BF16)
    ckv_ref[...] = _rms(pkv_ref[...], gkv_ref[...])
    t2 = pkr_ref[...]
    kr = t2 * cs_ref[0] + pltpu.roll(t2, MLA_ROPE, 1) * cs_ref[1]
    kr_ref[...] = kr[:, :MLA_ROPE]
    krb_ref[...] = kr.astype(BF16)


def mla_prep(lay, proj, g_q, g_kv, cs):
    te = lay.te
    M = lay.M
    return pl.pallas_call(
        _mla_prep_kernel,
        out_shape=(jax.ShapeDtypeStruct((M, MLA_QL), BF16),
                   jax.ShapeDtypeStruct((M, MLA_KVL), F32),
                   jax.ShapeDtypeStruct((M, MLA_ROPE), F32),
                   jax.ShapeDtypeStruct((M, LANE), BF16)),
        grid=(lay.nblocks,),
        in_specs=[pl.BlockSpec((te, MLA_QL), lambda i: (i, D // MLA_QL)),
                  pl.BlockSpec((te, MLA_KVL), lambda i: (i, D // MLA_KVL + 1)),
                  pl.BlockSpec((te, LANE), lambda i: (i, (D + MLA_QL + MLA_KVL) // LANE)),
                  pl.BlockSpec((1, MLA_QL), lambda i: (0, 0)),
                  pl.BlockSpec((1, MLA_KVL), lambda i: (0, 0)),
                  pl.BlockSpec((2, te, LANE), lambda i: (0, i, 0))],
        out_specs=(pl.BlockSpec((te, MLA_QL), lambda i: (i, 0)),
                   pl.BlockSpec((te, MLA_KVL), lambda i: (i, 0)),
                   pl.BlockSpec((te, MLA_ROPE), lambda i: (i, 0)),
                   pl.BlockSpec((te, LANE), lambda i: (i, 0))),
        compiler_params=_cparams(("parallel",)),
        name="mla_prep",
    )(proj, proj, proj, g_q[None], g_kv[None], cs)


def _mla_q_kernel(q_ref, cs_ref, qn_ref, qr_ref):
    for h in range(MLA_H):
        qn_ref[:, h * LANE:(h + 1) * LANE] = q_ref[:, 2 * h * LANE:(2 * h + 1) * LANE].astype(BF16)
        t2 = q_ref[:, (2 * h + 1) * LANE:(2 * h + 2) * LANE]
        qr = t2 * cs_ref[0] + pltpu.roll(t2, MLA_ROPE, 1) * cs_ref[1]
        qr_ref[:, h * LANE:(h + 1) * LANE] = qr.astype(BF16)


def mla_q(lay, q, cs):
    te = lay.te
    M = lay.M
    return pl.pallas_call(
        _mla_q_kernel,
        out_shape=(jax.ShapeDtypeStruct((M, MLA_H * LANE), BF16),
                   jax.ShapeDtypeStruct((M, MLA_H * LANE), BF16)),
        grid=(lay.nblocks,),
        in_specs=[pl.BlockSpec((te, MLA_H * 2 * LANE), lambda i: (i, 0)),
                  pl.BlockSpec((2, te, LANE), lambda i: (0, i, 0))],
        out_specs=(pl.BlockSpec((te, MLA_H * LANE), lambda i: (i, 0)),
                   pl.BlockSpec((te, MLA_H * LANE), lambda i: (i, 0))),
        compiler_params=_cparams(("parallel",)),
        name="mla_q",
    )(q, cs)


FLASH_HEADS = 2


def _flash_kernel(qn_ref, qr_ref, kn_ref, kr_ref, v_ref, o_ref, *, tq, tk):
    qi = pl.program_id(2)
    nt = (((1,), (1,)), ((), ()))
    hd = lambda u: slice(u * LANE, (u + 1) * LANE)
    qs = [jnp.concatenate([qn_ref[:, hd(u)], qr_ref[:, hd(u)]], axis=1) for u in range(FLASH_HEADS)]
    c = MLA_SCALE * math.log2(math.e)
    per_q = tq // tk

    def update(start, carry, diag):
        kr = kr_ref[pl.ds(start, tk), :]
        ss = [lax.dot_general(qs[u], jnp.concatenate([kn_ref[pl.ds(start, tk), hd(u)], kr], axis=1),
                              nt, preferred_element_type=F32) * c for u in range(FLASH_HEADS)]
        out = []
        for u in range(FLASH_HEADS):
            m, l, acc = carry[u]
            s = ss[u]
            if diag is not None:
                s = jnp.where(lax.broadcasted_iota(jnp.int32, (tq, tk), 1) + diag * tk
                              <= lax.broadcasted_iota(jnp.int32, (tq, tk), 0), s, -1e30)
            m_new = jnp.maximum(m, jnp.max(s, axis=1, keepdims=True))
            p = jnp.exp2(s - m_new)
            alpha = jnp.exp2(m - m_new)
            l = alpha * l + jnp.sum(p, axis=1, keepdims=True)
            acc = alpha * acc + jnp.dot(p.astype(BF16), v_ref[pl.ds(start, tk), hd(u)],
                                        preferred_element_type=F32)
            out.append((m_new, l, acc))
        return tuple(out)

    def body(ki, carry):
        return update(pl.multiple_of(ki * tk, tk), carry, None)

    init = (jnp.full((tq, 1), -1e30, F32), jnp.zeros((tq, 1), F32), jnp.zeros((tq, MLA_V), F32))
    carry = lax.fori_loop(0, qi * per_q, body, (init,) * FLASH_HEADS)
    for d in range(per_q):
        carry = update(pl.multiple_of(qi * tq + d * tk, tk), carry, d)
    for u in range(FLASH_HEADS):
        _, l, acc = carry[u]
        o_ref[:, hd(u)] = acc / l


def mla_flash(qn, qr, knv, krb, B, T):
    tq = _tile(T, 512)
    tk = tq
    nq = T // tq
    hw = FLASH_HEADS * LANE
    groups = MLA_H // FLASH_HEADS
    return pl.pallas_call(
        functools.partial(_flash_kernel, tq=tq, tk=tk),
        out_shape=jax.ShapeDtypeStruct((B * T, MLA_H * MLA_V), F32),
        grid=(B, groups, nq),
        in_specs=[pl.BlockSpec((tq, hw), lambda b, h, i: (b * nq + i, h)),
                  pl.BlockSpec((tq, hw), lambda b, h, i: (b * nq + i, h)),
                  pl.BlockSpec((T, hw), lambda b, h, i: (b, h)),
                  pl.BlockSpec((T, LANE), lambda b, h, i: (b, 0)),
                  pl.BlockSpec((T, hw), lambda b, h, i: (b, groups + h))],
        out_specs=pl.BlockSpec((tq, hw), lambda b, h, i: (b * nq + i, h)),
        compiler_params=_cparams(("parallel", "parallel", "arbitrary")),
        name="mla_flash",
    )(qn, qr, knv, krb, knv)


def _paged_kernel(pt_ref, ql_ref, qr_ref, cn_ref, kn_ref, *rest, pg, ts):
    ck_refs = rest[:pg]
    kr_refs = rest[pg:2 * pg]
    o_ref, m_ref, l_ref, acc_ref = rest[2 * pg:]
    p = pl.program_id(1)
    nt = (((1,), (1,)), ((), ()))

    @pl.when(p == 0)
    def _():
        m_ref[...] = jnp.full(m_ref.shape, -1e30, F32)
        l_ref[...] = jnp.zeros(l_ref.shape, F32)
        acc_ref[...] = jnp.zeros(acc_ref.shape, F32)

    rows = ts * MLA_H
    ql = ql_ref[...]
    qr = qr_ref[...][:, :MLA_ROPE]
    cks = [r[...].astype(BF16) for r in ck_refs]
    ck = jnp.concatenate(cks, axis=0)
    kr_t = jnp.concatenate([r[...].astype(BF16) for r in kr_refs], axis=1)
    s = jnp.concatenate([lax.dot_general(ql, c, nt, preferred_element_type=F32) for c in cks], axis=1)
    s = (s + jnp.dot(qr, kr_t, preferred_element_type=F32)) * MLA_SCALE
    m = m_ref[...]
    m_new = jnp.maximum(m, jnp.max(s, axis=1, keepdims=True))
    pr = jnp.exp(s - m_new)
    alpha = jnp.exp(m - m_new)
    l = alpha * l_ref[...] + jnp.sum(pr, axis=1, keepdims=True)
    acc = alpha * acc_ref[...] + jnp.dot(pr.astype(BF16), ck, preferred_element_type=F32)
    m = m_new
    m_ref[...] = m
    l_ref[...] = l
    acc_ref[...] = acc

    @pl.when(p == pl.num_programs(1) - 1)
    def _():
        qlf = ql.astype(F32)
        qrf = qr.astype(F32)
        tok = lax.broadcasted_iota(jnp.int32, (ts * MLA_H, 1), 0) // MLA_H
        cols = []
        for j in range(ts):
            cn = cn_ref[j:j + 1, :].astype(BF16).astype(F32)
            kn = kn_ref[j:j + 1, :MLA_ROPE].astype(F32)
            sj = (jnp.sum(qlf * cn, axis=1, keepdims=True)
                  + jnp.sum(qrf * kn, axis=1, keepdims=True)) * MLA_SCALE
            cols.append(jnp.where(tok >= j, sj, -1e30))
        m2 = m
        for sj in cols:
            m2 = jnp.maximum(m2, sj)
        alpha = jnp.exp(m - m2)
        l2 = alpha * l
        acc2 = alpha * acc
        for j, sj in enumerate(cols):
            pj = jnp.exp(sj - m2)
            l2 = l2 + pj
            acc2 = acc2 + pj.astype(BF16).astype(F32) * cn_ref[j:j + 1, :].astype(BF16).astype(F32)
        o_ref[...] = (acc2 / l2).astype(o_ref.dtype)


def mla_paged(page_table, ql, qr, ckv_new, krb_new, cache_ckv, cache_kr, layer, DB, TS):
    n_pages = page_table.shape[1]
    pg = _tile(n_pages, 32) if n_pages >= 8 else n_pages
    rows = TS * MLA_H
    pt = page_table.reshape(-1)

    def page_spec(i, shape):
        return pl.BlockSpec((None, None) + shape,
                            lambda b, p, pt_ref: (layer, pt_ref[b * n_pages + p * pg + i], 0, 0))

    cache_kr_t = jnp.swapaxes(cache_kr, 2, 3)
    per_b = lambda width: pl.BlockSpec((None, rows, width), lambda b, p, pt_ref: (b, 0, 0))
    new_b = lambda width: pl.BlockSpec((None, TS, width), lambda b, p, pt_ref: (b, 0, 0))
    grid_spec = pltpu.PrefetchScalarGridSpec(
        num_scalar_prefetch=1,
        grid=(DB, n_pages // pg),
        in_specs=[per_b(MLA_KVL), per_b(LANE), new_b(MLA_KVL), new_b(LANE)]
                 + [page_spec(i, (PAGE, MLA_KVL)) for i in range(pg)]
                 + [page_spec(i, (MLA_ROPE, PAGE)) for i in range(pg)],
        out_specs=per_b(MLA_KVL),
        scratch_shapes=[pltpu.VMEM((rows, 1), F32), pltpu.VMEM((rows, 1), F32),
                        pltpu.VMEM((rows, MLA_KVL), F32)],
    )
    return pl.pallas_call(
        functools.partial(_paged_kernel, pg=pg, ts=TS),
        out_shape=jax.ShapeDtypeStruct((DB, rows, MLA_KVL), BF16),
        grid_spec=grid_spec,
        compiler_params=_cparams(("parallel", "arbitrary")),
        name="mla_paged",
    )(pt, ql, qr, ckv_new, krb_new, *([cache_ckv] * pg), *([cache_kr_t] * pg))


def _pair_specs(lay):
    te, npb = lay.te, lay.n_p_blocks
    return [pl.BlockSpec((te, D), lambda i: (jnp.minimum(i, npb - 1), 0)),
            pl.BlockSpec((te, D), lambda i: (jnp.maximum(i - npb, 0), 0))]


def _gate_kernel(op_ref, os_ref, z_ref, y_ref, *, n_p_blocks):
    o = _read_rows((op_ref, os_ref), n_p_blocks)
    y_ref[...] = (o * _silu(z_ref[...])).astype(BF16)


def gate_mul(lay, o_pair, zsrc, zcol):
    te = lay.te
    return pl.pallas_call(
        functools.partial(_gate_kernel, n_p_blocks=lay.n_p_blocks),
        out_shape=jax.ShapeDtypeStruct((lay.M, D), BF16),
        grid=(lay.nblocks,),
        in_specs=_pair_specs(lay) + [pl.BlockSpec((te, D), lambda i: (i, zcol))],
        out_specs=pl.BlockSpec((te, D), lambda i: (i, 0)),
        compiler_params=_cparams(("parallel",)),
        name="gate_mul",
    )(*o_pair, zsrc)


def mla_layer(lay, h, cache_ckv, cache_kr, layer, page_table, w):
    B, T, DB, TS, Mp, M = lay.B, lay.T, lay.DB, lay.TS, lay.Mp, lay.M
    w_in = w["w_in"]
    w_kr = w_in[:, 2 * MLA_QL:2 * MLA_QL + MLA_ROPE]
    w1 = jnp.concatenate([w_in[:, 2 * MLA_QL + MLA_ROPE:], w_in[:, :2 * MLA_QL],
                          w_kr, _rot_cols(w_kr)], axis=1)
    proj = matmul2d(h, w1.astype(BF16), tn=640, name="mla_proj")
    pos = jnp.concatenate([jnp.tile(jnp.arange(T, dtype=F32), B),
                           jnp.tile(page_table.shape[1] * PAGE + jnp.arange(TS, dtype=F32), DB)])
    half = MLA_ROPE // 2
    inv_freq = ROPE_THETA ** (-jnp.arange(half, dtype=F32) / half)
    ang = pos[:, None] * inv_freq[None, :]
    zeros = jnp.zeros((M, MLA_ROPE), F32)
    cs = jnp.stack([jnp.concatenate([jnp.cos(ang), jnp.cos(ang), zeros], axis=1),
                    jnp.concatenate([jnp.sin(ang), jnp.sin(ang), zeros], axis=1)])
    cq, ckv, kr, krb = mla_prep(lay, proj, w["q_norm"], w["kv_norm"], cs)
    w_uq = w["w_uq"]
    wq = jnp.concatenate([w_uq, _rot_cols(w_uq[..., MLA_NOPE:])], axis=-1)
    q = matmul2d(cq, wq.reshape(MLA_QL, MLA_H * 2 * LANE).astype(BF16), name="mla_qproj")
    qn, qr = mla_q(lay, q, cs)
    w_kv = jnp.concatenate([w["w_uk"].reshape(MLA_KVL, -1), w["w_uv"].reshape(MLA_KVL, -1)], axis=1)
    knv = matmul2d(ckv[:Mp], w_kv.astype(BF16), out_dtype=BF16, name="mla_kv")
    o_p = mla_flash(qn, qr, knv, krb, B, T)
    w_ukT = w["w_uk"].transpose(1, 2, 0).astype(BF16)
    ql = head_matmul(qn[Mp:], w_ukT, BF16, "mla_qlat")
    o_lat = mla_paged(page_table, ql.reshape(DB, TS * MLA_H, MLA_KVL),
                      qr[Mp:].reshape(DB, TS * MLA_H, LANE),
                      ckv[Mp:].reshape(DB, TS, MLA_KVL), krb[Mp:].reshape(DB, TS, LANE),
                      cache_ckv, cache_kr, layer, DB, TS)
    o_s = head_matmul(o_lat.reshape(lay.Ms, MLA_H * MLA_KVL),
                      w["w_uv"].transpose(1, 0, 2).astype(BF16), F32, "mla_ouv")
    y = gate_mul(lay, (o_p, o_s), proj, 0)
    o_proj = matmul2d(y, w["w_out"].astype(BF16), name="mla_out")
    return (o_proj, ckv[:Mp].reshape(B, T, MLA_KVL), kr[:Mp].reshape(B, T, MLA_ROPE),
            ckv[Mp:].reshape(DB, TS, MLA_KVL), kr[Mp:].reshape(DB, TS, MLA_ROPE))


def _gla_kernel(q_ref, k_ref, v_ref, g_ref, s0_ref, o_ref, sT_ref, st_ref, *, L):
    c = pl.program_id(1)
    C = GLA_SUB
    assert L % C == 0
    heads = range(GLA_H)
    dk = lambda h: slice(h * GLA_DK, (h + 1) * GLA_DK)
    dv = lambda h: slice(h * GLA_DV, (h + 1) * GLA_DV)

    @pl.when(c == 0)
    def _():
        for h in heads:
            st_ref[h] = s0_ref[h].T

    nt = (((1,), (1,)), ((), ()))
    tn = (((0,), (0,)), ((), ()))
    trow = lax.broadcasted_iota(jnp.int32, (L, 1), 0)
    crow = lax.broadcasted_iota(jnp.int32, (C, 1), 0)
    col = lax.broadcasted_iota(jnp.int32, (C, L), 1)
    q = [q_ref[:, dk(h)] * (GLA_DK ** -0.5) for h in heads]
    k = [k_ref[:, dk(h)] for h in heads]
    v = [v_ref[:, dv(h)].astype(BF16) for h in heads]
    b = []
    for h in heads:
        bh = g_ref[:, dk(h)]
        sh = 1
        while sh < L:
            bh = bh + jnp.where(trow >= sh, pltpu.roll(bh, sh, 0), 0.0)
            sh *= 2
        b.append(bh)
    st = [st_ref[h] for h in heads]
    o = [lax.dot_general((q[h] * jnp.exp(b[h])).astype(BF16), st[h].astype(BF16), nt,
                         preferred_element_type=F32) for h in heads]
    att_rows = [[] for _ in heads]
    for i in range(L // C):
        lo = i * C
        if i == 0:
            att_i = [jnp.zeros((C, L), F32) for _ in heads]
        else:
            att_i = []
            for h in heads:
                beta = b[h][lo - 1:lo, :]
                q_in = (q[h][lo:lo + C] * jnp.exp(b[h][lo:lo + C] - beta)).astype(BF16)
                k_out = (k[h] * jnp.exp(jnp.where(trow < lo, beta - b[h], -jnp.inf))).astype(BF16)
                att_i.append(lax.dot_general(q_in, k_out, nt, preferred_element_type=F32))
        for s in range(C):
            for h in heads:
                bi = b[h][lo:lo + C]
                e = jnp.where(crow >= s, bi - bi[s:s + 1, :], -jnp.inf)
                a_s = jnp.sum(q[h][lo:lo + C] * k[h][lo + s:lo + s + 1, :] * jnp.exp(e),
                              axis=1, keepdims=True)
                att_i[h] = jnp.where(col == lo + s, a_s, att_i[h])
        for h in heads:
            att_rows[h].append(att_i[h])
    for h in heads:
        att = att_rows[h][0] if len(att_rows[h]) == 1 else jnp.concatenate(att_rows[h], axis=0)
        o_ref[:, dv(h)] = o[h] + jnp.dot(att.astype(BF16), v[h], preferred_element_type=F32)
    for h in heads:
        b_end = b[h][L - 1:L, :]
        kd = (k[h] * jnp.exp(b_end - b[h])).astype(BF16)
        st[h] = st[h] * jnp.exp(b_end) + lax.dot_general(v[h], kd, tn, preferred_element_type=F32)
        st_ref[h] = st[h]

    @pl.when(c == pl.num_programs(1) - 1)
    def _():
        for h in heads:
            sT_ref[h] = st[h].T


def gla_scan(q, k, v, g, s0, *, nseq, nchunk, L, kcol, vcol):
    rows = lambda n, c: n * nchunk + c
    st_spec = pl.BlockSpec((None, GLA_H, GLA_DK, GLA_DV), lambda n, c: (n, 0, 0, 0))
    return pl.pallas_call(
        functools.partial(_gla_kernel, L=L),
        out_shape=(jax.ShapeDtypeStruct((q.shape[0], GLA_H * GLA_DV), F32),
                   jax.ShapeDtypeStruct((nseq, GLA_H, GLA_DK, GLA_DV), F32)),
        grid=(nseq, nchunk),
        in_specs=[pl.BlockSpec((L, GLA_QK), lambda n, c: (rows(n, c), 0)),
                  pl.BlockSpec((L, GLA_QK), lambda n, c: (rows(n, c), kcol)),
                  pl.BlockSpec((L, GLA_H * GLA_DV), lambda n, c: (rows(n, c), vcol)),
                  pl.BlockSpec((L, GLA_QK), lambda n, c: (rows(n, c), 0)),
                  st_spec],
        out_specs=(pl.BlockSpec((L, GLA_H * GLA_DV), lambda n, c: (rows(n, c), 0)), st_spec),
        scratch_shapes=[pltpu.VMEM((GLA_H, GLA_DV, GLA_DK), F32)],
        compiler_params=_cparams(("parallel", "arbitrary")),
        name="gla_scan",
    )(q, k, v, g, s0)


def _gla_post_kernel(op_ref, os_ref, z_ref, g_ref, y_ref, *, n_p_blocks):
    o = _read_rows((op_ref, os_ref), n_p_blocks)
    for h in range(GLA_H):
        hs = slice(h * GLA_DV, (h + 1) * GLA_DV)
        y_ref[:, hs] = (_rms(o[:, hs], g_ref[...]) * _silu(z_ref[:, hs])).astype(BF16)


def gla_post(lay, o_pair, proj, g_norm):
    te = lay.te
    return pl.pallas_call(
        functools.partial(_gla_post_kernel, n_p_blocks=lay.n_p_blocks),
        out_shape=jax.ShapeDtypeStruct((lay.M, D), BF16),
        grid=(lay.nblocks,),
        in_specs=_pair_specs(lay) + [pl.BlockSpec((te, D), lambda i: (i, 2)),
                                     pl.BlockSpec((1, GLA_DV), lambda i: (0, 0))],
        out_specs=pl.BlockSpec((te, D), lambda i: (i, 0)),
        compiler_params=_cparams(("parallel",)),
        name="gla_post",
    )(*o_pair, proj, g_norm[None])


def gla_layer(lay, h, state, w):
    B, T, DB, TS, Mp, M = lay.B, lay.T, lay.DB, lay.TS, lay.Mp, lay.M
    w_in = _pad_cols(w["w_in"], 2 * GLA_QK + 2 * D + GLA_LORA_PAD)
    proj = matmul2d(h, w_in.astype(BF16), tn=640, name="gla_proj")
    gl = proj[:, 2 * GLA_QK + 2 * D:]
    g = matmul2d(gl, _pad_rows(w["w_g2"], GLA_LORA_PAD).astype(BF16), w["b_g"],
                 act_out="logsig_tau", name="gla_gate")
    L = math.gcd(T, GLA_CHUNK)
    cols = dict(kcol=1, vcol=2 * GLA_QK // (GLA_H * GLA_DV))
    o_p, p_state = gla_scan(proj, proj, proj, g, jnp.zeros((B, GLA_H, GLA_DK, GLA_DV), F32),
                            nseq=B, nchunk=T // L, L=L, **cols)
    LS = GLA_SUB
    pad = lambda a: jnp.pad(a.reshape(DB, TS, -1), ((0, 0), (0, LS - TS), (0, 0))).reshape(DB * LS, -1)
    ps = pad(proj[Mp:, :2 * GLA_QK + D])
    o_s, s_state = gla_scan(ps, ps, ps, pad(g[Mp:]), state, nseq=DB, nchunk=1, L=LS, **cols)
    o_s = o_s.reshape(DB, LS, D)[:, :TS].reshape(lay.Ms, D)
    y = gla_post(lay, (o_p, o_s), proj, w["norm"])
    o_proj = matmul2d(y, w["w_out"].astype(BF16), name="gla_out")
    return o_proj, p_state, s_state


def kernel(x_prompt, x_sample, c_prompt, c_sample, state_rwkv_wkv, state_rwkv_shift, cache_mla_ckv, cache_mla_krope, page_table, state_gla, norm_pre, norm_post, ada_w, ada_b, rw_mu, rw_w_in, rw_w0, rw_w1, rw_w2, rw_a0, rw_a1, rw_a2, rw_v0, rw_v1, rw_v2, rw_k_k, rw_k_a, rw_r_k, rw_ln_w, rw_ln_b, rw_w_out, mla_w_in, mla_q_norm, mla_kv_norm, mla_w_uq, mla_w_uk, mla_w_uv, mla_w_out, gla_w_in, gla_w_g2, gla_b_g, gla_norm, gla_w_out):
    B, T, _ = x_prompt.shape
    DB, TS, _ = x_sample.shape
    depth = norm_pre.shape[0]
    lay = RowLayout(B, T, DB, TS)
    Mp = lay.Mp
    x = (x_prompt.reshape(Mp, D), x_sample.reshape(lay.Ms, D))
    c = jnp.concatenate([c_prompt, c_sample], axis=0)[None]
    c = jnp.broadcast_to(c, (depth,) + c.shape[1:])
    mods = matmul(c, ada_w, ada_b[:, None, :], act_in="silu", name="ada")

    p_wkv, p_shift, p_ckv, p_kr, p_gla = [], [], [], [], []
    s_shift, s_ckv, s_kr, s_gla = [], [], [], []
    v_first = None
    state_t = jnp.transpose(state_rwkv_wkv, (0, 2, 3, 4, 1))
    s_wkv_t = None
    mod_of = [lay.expand_mod(mods[i]) for i in range(depth)]
    h_dtype = lambda i: F32 if i % 3 == 0 else BF16
    (h,) = norm_step(lay, x, pre=(norm_pre[0], mod_of[0], h_dtype(0)))
    for i in range(depth):
        kind, j = i % 3, i // 3
        if kind == 0:
            w = dict(mu=rw_mu[j], w_in=rw_w_in[j], w0=rw_w0[j], w1=rw_w1[j], w2=rw_w2[j],
                     a0=rw_a0[j], a1=rw_a1[j], a2=rw_a2[j], k_k=rw_k_k[j], k_a=rw_k_a[j],
                     r_k=rw_r_k[j], ln_w=rw_ln_w[j], ln_b=rw_ln_b[j], w_out=rw_w_out[j])
            if j > 0:
                w.update(v0=rw_v0[j - 1], v1=rw_v1[j - 1], v2=rw_v2[j - 1])
            o, pst, psh, s_wkv_t, ssh, v_first = rwkv_layer(
                lay, h, state_rwkv_shift[j], state_t, j, s_wkv_t, v_first if j > 0 else None, w)
            p_wkv.append(pst)
            p_shift.append(psh)
            s_shift.append(ssh)
        elif kind == 1:
            w = dict(w_in=mla_w_in[j], q_norm=mla_q_norm[j], kv_norm=mla_kv_norm[j],
                     w_uq=mla_w_uq[j], w_uk=mla_w_uk[j], w_uv=mla_w_uv[j], w_out=mla_w_out[j])
            o, pc, pk, sc, sk = mla_layer(lay, h, cache_mla_ckv, cache_mla_krope, j, page_table, w)
            p_ckv.append(pc)
            p_kr.append(pk)
            s_ckv.append(sc)
            s_kr.append(sk)
        else:
            w = dict(w_in=gla_w_in[j], w_g2=gla_w_g2[j], b_g=gla_b_g[j], norm=gla_norm[j],
                     w_out=gla_w_out[j])
            o, pg, sg = gla_layer(lay, h, state_gla[j], w)
            p_gla.append(pg)
            s_gla.append(sg)
        res = (o, norm_post[i], mod_of[i])
        if i + 1 < depth:
            x, h = norm_step(lay, x, res=res, pre=(norm_pre[i + 1], mod_of[i + 1], h_dtype(i + 1)))
        else:
            yp, ys = norm_step(lay, x, res=res, split_out=True)
    yp = yp.reshape(B, T, D)
    ys = ys.reshape(DB, TS, D)
    return (yp, ys,
            jnp.stack(p_wkv), jnp.stack(p_shift), jnp.stack(p_ckv), jnp.stack(p_kr), jnp.stack(p_gla),
            jnp.transpose(s_wkv_t, (0, 4, 1, 2, 3)), jnp.stack(s_shift), jnp.stack(s_ckv),
            jnp.stack(s_kr), jnp.stack(s_gla))
```

```python
import functools
import math

import jax
import jax.numpy as jnp
import numpy as np
from jax import lax
from jax.experimental import pallas as pl
from jax.experimental.pallas import tpu as pltpu

F32 = jnp.float32
BF16 = jnp.bfloat16

D = 2048
NORM_EPS = 1e-6
RW_N = 64
RW_H = D // RW_N
RW_LORA_PAD = 128
RW_GN_EPS = 64e-5
RW_TILES = D // 128
RW_FOLD = 128 // RW_H
EXP_M05 = math.exp(-0.5)
MLA_H = 16
MLA_NOPE = 128
MLA_ROPE = 64
MLA_V = 128
MLA_QL = 512
MLA_KVL = 512
MLA_SCALE = (MLA_NOPE + MLA_ROPE) ** -0.5
ROPE_THETA = 10000.0
PAGE = 128
GLA_H = 4
GLA_DK = 256
GLA_DV = 512
GLA_QK = GLA_H * GLA_DK
GLA_LORA_PAD = 256
GLA_TAU = 16.0
GLA_CHUNK = 64
GLA_SUB = 16

LANE = 128
ROW_TILE = 128
VMEM_LIMIT = 48 * 1024 * 1024


def _cparams(sem):
    return pltpu.CompilerParams(dimension_semantics=sem, vmem_limit_bytes=VMEM_LIMIT)


def _tile(n, pref):
    for t in (1024, 640, 512, 384, 256, 128, 64, 32, 16, 8):
        if t <= pref and n % t == 0:
            return t
    return n


def _sigmoid(x):
    return 1.0 / (1.0 + jnp.exp(-x))


def _silu(x):
    return x * _sigmoid(x)


def _mm_kernel(*refs, has_bias, act_in, act_out, tanh_group):
    if has_bias:
        x_ref, w_ref, b_ref, o_ref = refs
    else:
        x_ref, w_ref, o_ref = refs
    x = x_ref[...]
    if act_in == "silu":
        x = _silu(x.astype(F32))
    acc = jnp.dot(x.astype(BF16), w_ref[...].astype(BF16), preferred_element_type=F32)
    if has_bias:
        acc = acc + b_ref[...]
    if act_out == "tanh_group":
        acc = jnp.where(pl.program_id(0) == tanh_group, jnp.tanh(acc), acc)
    elif act_out == "logsig_tau":
        acc = (jnp.minimum(acc, 0.0) - jnp.log(1.0 + jnp.exp(-jnp.abs(acc)))) * (1.0 / GLA_TAU)
    o_ref[...] = acc.astype(o_ref.dtype)


MM_X_TILE_BYTES = 9 * 1024 * 1024
MM_MAX_ROWS = 2304


def _row_tile(M, row_bytes):
    for parts in range(1, M // 8 + 1):
        rows = M // parts
        if M % parts == 0 and rows % 8 == 0 and rows <= MM_MAX_ROWS and rows * row_bytes <= MM_X_TILE_BYTES:
            return rows
    return M


def matmul(x, w, bias=None, *, x_off=0, act_in=None, act_out=None, tanh_group=0,
           out_dtype=F32, tn=512, name="mm"):
    G, K, N = w.shape
    M = x.shape[1]
    tm = _row_tile(M, K * x.dtype.itemsize)
    tn = _tile(N, tn)
    in_specs = [
        pl.BlockSpec((None, tm, K), lambda g, i, j: (g + x_off, i, 0)),
        pl.BlockSpec((None, K, tn), lambda g, i, j: (g, 0, j)),
    ]
    args = [x, w]
    if bias is not None:
        in_specs.append(pl.BlockSpec((None, 1, tn), lambda g, i, j: (g, 0, j)))
        args.append(bias)
    return pl.pallas_call(
        functools.partial(_mm_kernel, has_bias=bias is not None, act_in=act_in,
                          act_out=act_out, tanh_group=tanh_group),
        out_shape=jax.ShapeDtypeStruct((G, M, N), out_dtype),
        grid=(G, M // tm, N // tn),
        in_specs=in_specs,
        out_specs=pl.BlockSpec((None, tm, tn), lambda g, i, j: (g, i, j)),
        compiler_params=_cparams(("parallel", "parallel", "arbitrary")),
        name=name,
    )(*args)


def matmul2d(x, w, bias=None, **kw):
    b3 = None if bias is None else bias[None, None, :]
    return matmul(x[None], w[None], b3, **kw)[0]


def _hmm_kernel(x_ref, w_ref, o_ref):
    o_ref[...] = jnp.dot(x_ref[...].astype(BF16), w_ref[...],
                         preferred_element_type=F32).astype(o_ref.dtype)


def head_matmul(x, w, out_dtype, name):
    H, Kh, Nh = w.shape
    M = x.shape[0]
    tm = _tile(M, 512)
    return pl.pallas_call(
        _hmm_kernel,
        out_shape=jax.ShapeDtypeStruct((M, H * Nh), out_dtype),
        grid=(H, M // tm),
        in_specs=[pl.BlockSpec((tm, Kh), lambda h, i: (i, h)),
                  pl.BlockSpec((None, Kh, Nh), lambda h, i: (h, 0, 0))],
        out_specs=pl.BlockSpec((tm, Nh), lambda h, i: (i, h)),
        compiler_params=_cparams(("parallel", "arbitrary")),
        name=name,
    )(x, w)


class RowLayout:
    def __init__(self, B, T, DB, TS):
        self.B, self.T, self.DB, self.TS = B, T, DB, TS
        self.Mp = B * T
        self.Ms = DB * TS
        self.M = self.Mp + self.Ms
        self.te = math.gcd(math.gcd(ROW_TILE, T), self.Ms)
        self.n_p_blocks = self.Mp // self.te
        self.blocks_per_seq = T // self.te
        self.nblocks = self.M // self.te

    def mod_index(self, i):
        return jnp.where(i < self.n_p_blocks, i // self.blocks_per_seq,
                         self.B + i - self.n_p_blocks)

    def expand_mod(self, m):
        mp = jnp.broadcast_to(m[:self.B, None, :], (self.B, self.te, m.shape[-1]))
        ms = jnp.repeat(m[self.B:], self.TS, axis=0).reshape(self.Ms // self.te, self.te, -1)
        return jnp.concatenate([mp, ms], axis=0)


def _rms(x, g):
    ms = jnp.mean(x * x, axis=-1, keepdims=True)
    return x * lax.rsqrt(ms + NORM_EPS) * g


def _read_rows(refs, n_p_blocks):
    if len(refs) == 1:
        return refs[0][...]
    return jnp.where(pl.program_id(0) < n_p_blocks, refs[0][...], refs[1][...])


def _norm_step_kernel(*refs, nx, residual, prenorm, split_out, n_p_blocks):
    refs = list(refs)
    x = _read_rows(refs[:nx], n_p_blocks)
    pos = nx
    if residual:
        o_ref, gpost_ref, gate_ref = refs[pos:pos + 3]
        pos += 3
        x = x + gate_ref[...] * _rms(o_ref[...], gpost_ref[...])
    if prenorm:
        gpre_ref, shift_ref, scale_ref = refs[pos:pos + 3]
        pos += 3
    outs = refs[pos:]
    if residual:
        if split_out:
            @pl.when(pl.program_id(0) < n_p_blocks)
            def _():
                outs[0][...] = x

            @pl.when(pl.program_id(0) >= n_p_blocks)
            def _():
                outs[1][...] = x
            outs = outs[2:]
        else:
            outs[0][...] = x
            outs = outs[1:]
    if prenorm:
        h = _rms(x, gpre_ref[...]) * (1.0 + scale_ref[...]) + shift_ref[...]
        outs[0][...] = h.astype(outs[0].dtype)


def norm_step(lay, x, *, res=None, pre=None, split_out=False):
    te, npb = lay.te, lay.n_p_blocks
    row = pl.BlockSpec((te, D), lambda i: (i, 0))
    vec = pl.BlockSpec((1, D), lambda i: (0, 0))
    modcol = lambda c: pl.BlockSpec((None, te, D), lambda i: (lay.mod_index(i), 0, c))
    p_row = pl.BlockSpec((te, D), lambda i: (jnp.minimum(i, npb - 1), 0))
    s_row = pl.BlockSpec((te, D), lambda i: (jnp.maximum(i - npb, 0), 0))
    xs = list(x) if isinstance(x, tuple) else [x]
    nx = len(xs)
    in_specs = [p_row, s_row] if nx == 2 else [row]
    args = list(xs)
    out_shape, out_specs = [], []
    if res is not None:
        o, g_post, mod = res
        in_specs += [row, vec, modcol(2)]
        args += [o, g_post[None], mod]
        if split_out:
            out_shape += [jax.ShapeDtypeStruct((lay.Mp, D), F32), jax.ShapeDtypeStruct((lay.Ms, D), F32)]
            out_specs += [p_row, s_row]
        else:
            out_shape.append(jax.ShapeDtypeStruct((lay.M, D), F32))
            out_specs.append(row)
    if pre is not None:
        g_pre, mod, dtype = pre
        in_specs += [vec, modcol(0), modcol(1)]
        args += [g_pre[None], mod, mod]
        out_shape.append(jax.ShapeDtypeStruct((lay.M, D), dtype))
        out_specs.append(row)
    return pl.pallas_call(
        functools.partial(_norm_step_kernel, nx=nx, residual=res is not None,
                          prenorm=pre is not None, split_out=split_out, n_p_blocks=npb),
        out_shape=tuple(out_shape),
        grid=(lay.nblocks,),
        in_specs=in_specs,
        out_specs=tuple(out_specs),
        compiler_params=_cparams(("arbitrary",)),
        name="norm_step",
    )(*args)


RW_MIX_ORDER = (0, 1, 3, 2, 4, 5)


SUBLANES = 8


def _rw_mix_kernel(h_ref, tail_ref, first_ref, mu_ref, x_ref, *, blocks_per_seq, n_p_blocks, ts):
    i = pl.program_id(0)
    h = h_ref[...]
    rows = h.shape[0]
    ridx = lax.broadcasted_iota(jnp.int32, (rows, 1), 0)
    prev = pltpu.roll(h, 1, 0)
    tail = tail_ref[SUBLANES - 1:SUBLANES, :]
    opens = jnp.logical_and(i < n_p_blocks, i % blocks_per_seq == 0)
    prev = jnp.where(ridx == 0, jnp.where(opens, 0.0, tail), prev)
    is_sample = i >= n_p_blocks
    prev = jnp.where(jnp.logical_and(is_sample, ridx % ts == 0), first_ref[...], prev)
    dx = prev - h
    for o, c in enumerate(RW_MIX_ORDER):
        x_ref[o] = (h + dx * mu_ref[c:c + 1, :]).astype(BF16)


def rw_mix(lay, h, shift_rows, mu):
    te = lay.te
    per = te // SUBLANES
    return pl.pallas_call(
        functools.partial(_rw_mix_kernel, blocks_per_seq=lay.blocks_per_seq,
                          n_p_blocks=lay.n_p_blocks, ts=lay.TS),
        out_shape=jax.ShapeDtypeStruct((6, lay.M, D), BF16),
        grid=(lay.nblocks,),
        in_specs=[pl.BlockSpec((te, D), lambda i: (i, 0)),
                  pl.BlockSpec((SUBLANES, D), lambda i: (jnp.maximum(i * per - 1, 0), 0)),
                  pl.BlockSpec((te, D), lambda i: (jnp.maximum(i - lay.n_p_blocks, 0), 0)),
                  pl.BlockSpec((6, D), lambda i: (0, 0))],
        out_specs=pl.BlockSpec((6, te, D), lambda i: (0, i, 0)),
        compiler_params=_cparams(("parallel",)),
        name="rw_mix",
    )(h, h, shift_rows, mu)


def _head_fold(s):
    s = s + pltpu.roll(s, RW_H, 1)
    return s + pltpu.roll(s, 2 * RW_H, 1)


def _rw_prep_kernel(*refs, vres):
    if vres:
        p_ref, l_ref, par_ref, vf_ref, out_ref, v_ref, rk_ref = refs
    else:
        p_ref, l_ref, par_ref, out_ref, v_ref, rk_ref = refs
        vf_ref = None
    _rw_prep_rows(p_ref, l_ref, par_ref, vf_ref, out_ref, v_ref, rk_ref)


def _rw_prep_rows(p_ref, l_ref, par_ref, vf_ref, out_ref, v_ref, rk_ref):
    vres = vf_ref is not None
    iw, ia = (1, 2) if vres else (0, 1)
    rows = out_ref.shape[1]
    n2 = jnp.zeros((rows, LANE), F32)
    rk = jnp.zeros((rows, LANE), F32)
    for j in range(RW_TILES):
        js = slice(j * LANE, (j + 1) * LANE)
        par = lambda i: par_ref[i:i + 1, js]
        r = p_ref[0, :, js]
        k = p_ref[1, :, js]
        v = p_ref[3, :, js]
        decay = jnp.exp(-EXP_M05 * _sigmoid(par(0) + l_ref[iw, :, js]))
        a = _sigmoid(par(1) + l_ref[ia, :, js])
        kk = k * par(2)
        n2 = n2 + kk * kk
        kmod = k * (1.0 + (a - 1.0) * par(3))
        rk = rk + r * kmod * par(4)
        if vres:
            v = v + (vf_ref[:, js] - v) * _sigmoid(par(5) + l_ref[0, :, js])
        out_ref[0, :, js] = decay
        out_ref[1, :, js] = kmod
        out_ref[2, :, js] = kk
        out_ref[3, :, js] = a
        v_ref[:, js] = v
    inv = 1.0 / jnp.maximum(jnp.sqrt(_head_fold(n2)), 1e-12)
    rk_ref[...] = _head_fold(rk)
    for j in range(RW_TILES):
        js = slice(j * LANE, (j + 1) * LANE)
        kkn = out_ref[2, :, js] * inv
        out_ref[2, :, js] = kkn
        out_ref[3, :, js] = kkn * out_ref[3, :, js]


def rw_prep_sample(lay, proj, lora, params, v_first_s):
    te, npb = lay.te, lay.n_p_blocks
    vres = v_first_s is not None
    nl = lora.shape[0]
    in_specs = [pl.BlockSpec((4, te, D), lambda i: (0, npb + i, 0)),
                pl.BlockSpec((nl, te, D), lambda i: (0, npb + i, 0)),
                pl.BlockSpec(params.shape, lambda i: (0, 0))]
    args = [proj, lora, params]
    if vres:
        in_specs.append(pl.BlockSpec((te, D), lambda i: (i, 0)))
        args.append(v_first_s)
    return pl.pallas_call(
        functools.partial(_rw_prep_kernel, vres=vres),
        out_shape=(jax.ShapeDtypeStruct((4, lay.Ms, D), F32),
                   jax.ShapeDtypeStruct((lay.Ms, D), F32),
                   jax.ShapeDtypeStruct((lay.Ms, LANE), F32)),
        grid=(lay.Ms // te,),
        in_specs=in_specs,
        out_specs=(pl.BlockSpec((4, te, D), lambda i: (0, i, 0)),
                   pl.BlockSpec((te, D), lambda i: (i, 0)),
                   pl.BlockSpec((te, LANE), lambda i: (i, 0))),
        compiler_params=_cparams(("parallel",)),
        name="rw_prep",
    )(*args)


def _wkv_kernel(*refs, steps, vres):
    n_in = 4 + (2 if vres else 0)
    p_a, p_b, l_a, l_b = refs[:4]
    vf_a, vf_b = (refs[4], refs[5]) if vres else (None, None)
    par_ref, rep_ref = refs[n_in:n_in + 2]
    o_a, o_b, st_ref, vout_a, vout_b, rk_a, rk_b = refs[n_in + 2:n_in + 9]
    vrep_a, vrep_b, s_scr, v3_scr = refs[n_in + 9:]

    @pl.when(pl.program_id(0) == 0)
    def _():
        st_ref[...] = jnp.zeros(st_ref.shape, F32)

    _rw_prep_rows(p_a, l_a, par_ref, vf_a, s_scr.at[0], vout_a, rk_a)
    _rw_prep_rows(p_b, l_b, par_ref, vf_b, s_scr.at[1], vout_b, rk_b)
    for q, vout in enumerate((vout_a, vout_b)):
        for j in range(RW_TILES):
            v3_scr[q, :, j, :] = vout[:, j * LANE:(j + 1) * LANE]

    r_a, r_b = p_a.at[0], p_b.at[0]
    s_a, s_b = s_scr.at[0], s_scr.at[1]
    seqs = ((r_a, s_a, v3_scr.at[0], o_a, vrep_a), (r_b, s_b, v3_scr.at[1], o_b, vrep_b))
    for (_, _, v_ref, _, vrep) in seqs:
        y = v_ref[...].reshape(steps * RW_TILES, LANE)
        hi = y.astype(BF16)
        lo = (y - hi.astype(F32)).astype(BF16)
        for g in range(RW_FOLD):
            yr = (jnp.dot(hi, rep_ref[g], preferred_element_type=F32)
                  + jnp.dot(lo, rep_ref[g], preferred_element_type=F32))
            vrep[:, g * RW_TILES:(g + 1) * RW_TILES, :] = yr.reshape(steps, RW_TILES, LANE)

    lane_group = lax.broadcasted_iota(jnp.int32, (RW_TILES, LANE), 1) // RW_H

    def kk_dot(q, s_ref):
        p = jnp.zeros((RW_N, LANE), F32)
        for j in range(RW_TILES):
            p = p + st_ref[q, j] * s_ref[2, 0:1, j * LANE:(j + 1) * LANE]
        return p

    def step(t, carry):
        t_next = jnp.minimum(t + 1, steps - 1)
        out, raw = [], []
        for q, (r_ref, s_ref, _, o_ref, vrep) in enumerate(seqs):
            sa = -(carry[q] if q == 0 else _head_fold(carry[q]))
            vr = vrep[t]
            w_row = s_ref[0, pl.ds(t, 1), :]
            k_row = s_ref[1, pl.ds(t, 1), :]
            b_row = s_ref[3, pl.ds(t, 1), :]
            r_row = r_ref[pl.ds(t, 1), :]
            kk_row = s_ref[2, pl.ds(t_next, 1), :]
            o = jnp.zeros((RW_N, LANE), F32)
            p = jnp.zeros((RW_N, LANE), F32)
            for j in range(RW_TILES):
                js = slice(j * LANE, (j + 1) * LANE)
                sn = st_ref[q, j] * w_row[:, js] + sa * b_row[:, js] + vr * k_row[:, js]
                st_ref[q, j] = sn
                o = o + sn * r_row[:, js]
                p = p + sn * kk_row[:, js]
            out.append(_head_fold(p) if q == 0 else p)
            emit(q, jnp.maximum(t - 1, 0), carry[2 + q])
            raw.append(o)
        return tuple(out + raw)

    def emit(q, t, o):
        o_ref = seqs[q][3]
        o = _head_fold(o)
        res = o[0:RW_TILES]
        for g in range(1, RW_FOLD):
            res = jnp.where(lane_group == g, o[g * RW_TILES:(g + 1) * RW_TILES], res)
        o_ref[t] = res

    init = tuple(kk_dot(q, s[1]) for q, s in enumerate(seqs))
    zero = jnp.zeros((RW_N, LANE), F32)
    last = lax.fori_loop(0, steps, step, (_head_fold(init[0]), init[1], zero, zero))
    for q in range(2):
        emit(q, steps - 1, last[2 + q])


def wkv_prompt(proj4, lora, params, v_first_p, T):
    tb = _tile(T, 64)
    nc = T // tb
    nl = lora.shape[0]
    vres = v_first_p is not None
    lane = np.arange(LANE)
    rep = jnp.asarray(np.stack([(lane[:, None] // RW_H == g) & (lane[:, None] % RW_H == lane[None, :] % RW_H)
                                for g in range(RW_FOLD)]), BF16)
    pspec = lambda s: pl.BlockSpec((4, tb, D), lambda c: (0, s * nc + c, 0))
    lspec = lambda s: pl.BlockSpec((nl, tb, D), lambda c: (0, s * nc + c, 0))
    rows = lambda width: pl.BlockSpec((tb, width), lambda c: (c, 0))
    ospec = pl.BlockSpec((tb, RW_TILES, LANE), lambda c: (c, 0, 0))
    in_specs = [pspec(0), pspec(1), lspec(0), lspec(1)]
    args = [proj4, proj4, lora, lora]
    if vres:
        in_specs += [rows(D), rows(D)]
        args += list(v_first_p)
    in_specs += [pl.BlockSpec(params.shape, lambda c: (0, 0)),
                 pl.BlockSpec((RW_FOLD, LANE, LANE), lambda c: (0, 0, 0))]
    args += [params, rep]
    o_a, o_b, st, v_a, v_b, rk_a, rk_b = pl.pallas_call(
        functools.partial(_wkv_kernel, steps=tb, vres=vres),
        out_shape=(jax.ShapeDtypeStruct((T, RW_TILES, LANE), F32),
                   jax.ShapeDtypeStruct((T, RW_TILES, LANE), F32),
                   jax.ShapeDtypeStruct((2, RW_TILES, RW_N, LANE), F32),
                   jax.ShapeDtypeStruct((T, D), F32), jax.ShapeDtypeStruct((T, D), F32),
                   jax.ShapeDtypeStruct((T, LANE), F32), jax.ShapeDtypeStruct((T, LANE), F32)),
        grid=(nc,),
        in_specs=in_specs,
        out_specs=(ospec, ospec, pl.BlockSpec((2, RW_TILES, RW_N, LANE), lambda c: (0, 0, 0, 0)),
                   rows(D), rows(D), rows(LANE), rows(LANE)),
        scratch_shapes=[pltpu.VMEM((tb, RW_N, LANE), F32),
                        pltpu.VMEM((tb, RW_N, LANE), F32),
                        pltpu.VMEM((2, 4, tb, D), F32),
                        pltpu.VMEM((2, tb, RW_TILES, LANE), F32)],
        compiler_params=_cparams(("arbitrary",)),
        name="wkv",
    )(*args)
    return o_a, o_b, st, (v_a, v_b), (rk_a, rk_b)


def _wkv_sample_kernel(x_ref, s0_ref, *rest, steps):
    o_ref, st_ref = rest[-2:]

    def row(v, carry):
        s = s0_ref[v]
        for t in range(steps):
            p = jnp.sum(s * x_ref[3, t], axis=0, keepdims=True)
            vt = x_ref[5, t, pl.ds(v, 1), :]
            s = s * x_ref[1, t] - p * x_ref[4, t] + vt * x_ref[2, t]
            o_ref[t, pl.ds(v, 1), :] = jnp.sum(s * x_ref[0, t], axis=0, keepdims=True)
        st_ref[v] = s
        return carry

    lax.fori_loop(0, RW_N, row, 0, unroll=8)


def wkv_sample(x6, state_t, layer, into=None):
    _, TS, H, _, DB = x6.shape
    L = state_t.shape[0]
    st_spec = pl.BlockSpec((None, None, RW_N, RW_N, DB), lambda h: (layer, h, 0, 0, 0))
    in_specs = [pl.BlockSpec((6, TS, None, RW_N, DB), lambda h: (0, 0, h, 0, 0)), st_spec]
    args = [x6, state_t]
    aliases = {}
    if into is not None:
        aliases = {2: 1}
        in_specs.append(pl.BlockSpec(memory_space=pl.ANY))
        args.append(into)
    return pl.pallas_call(
        functools.partial(_wkv_sample_kernel, steps=TS),
        out_shape=(jax.ShapeDtypeStruct((TS, H, RW_N, DB), F32),
                   jax.ShapeDtypeStruct((L, H, RW_N, RW_N, DB), F32)),
        grid=(H,),
        in_specs=in_specs,
        out_specs=(pl.BlockSpec((TS, None, RW_N, DB), lambda h: (0, h, 0, 0)), st_spec),
        input_output_aliases=aliases,
        compiler_params=_cparams(("parallel",)),
        name="wkv_sample",
    )(*args)


def _rw_post_kernel(oa_ref, ob_ref, os_ref, va_ref, vb_ref, vs_ref, rka_ref, rkb_ref, rks_ref,
                    z_ref, ln_ref, y_ref, o_scr, *, n_seq_blocks):
    i = pl.program_id(0)
    rows = y_ref.shape[0]

    def run(o_tile, v_ref, rk_ref):
        s = jnp.zeros((rows, LANE), F32)
        for j in range(RW_TILES):
            js = slice(j * LANE, (j + 1) * LANE)
            oj = o_tile(j)
            o_scr[:, js] = oj
            s = s + oj
        mean = _head_fold(s) * (1.0 / RW_N)
        s2 = jnp.zeros((rows, LANE), F32)
        for j in range(RW_TILES):
            d = o_scr[:, j * LANE:(j + 1) * LANE] - mean
            s2 = s2 + d * d
        rstd = lax.rsqrt(_head_fold(s2) * (1.0 / RW_N) + RW_GN_EPS)
        rk = rk_ref[...]
        for j in range(RW_TILES):
            js = slice(j * LANE, (j + 1) * LANE)
            o = (o_scr[:, js] - mean) * rstd * ln_ref[0:1, js] + ln_ref[1:2, js]
            o = o + rk * v_ref[:, js]
            y_ref[:, js] = (o * _silu(z_ref[:, js])).astype(BF16)

    @pl.when(i < n_seq_blocks)
    def _():
        run(lambda j: oa_ref[:, j, :], va_ref, rka_ref)

    @pl.when(jnp.logical_and(i >= n_seq_blocks, i < 2 * n_seq_blocks))
    def _():
        run(lambda j: ob_ref[:, j, :], vb_ref, rkb_ref)

    @pl.when(i >= 2 * n_seq_blocks)
    def _():
        run(lambda j: os_ref[:, j * LANE:(j + 1) * LANE], vs_ref, rks_ref)


def rw_post(lay, o3, v3, rk3, proj4, ln):
    te = lay.te
    row = pl.BlockSpec((te, D), lambda i: (i, 0))
    nsb = lay.blocks_per_seq
    npb = lay.n_p_blocks
    a_idx = lambda i: jnp.where(i < nsb, i, 0)
    b_idx = lambda i: jnp.where(jnp.logical_and(i >= nsb, i < 2 * nsb), i - nsb, 0)
    s_idx = lambda i: jnp.maximum(i - npb, 0)
    triple = lambda width: [pl.BlockSpec((te, width), lambda i: (a_idx(i), 0)),
                            pl.BlockSpec((te, width), lambda i: (b_idx(i), 0)),
                            pl.BlockSpec((te, width), lambda i: (s_idx(i), 0))]
    return pl.pallas_call(
        functools.partial(_rw_post_kernel, n_seq_blocks=nsb),
        out_shape=jax.ShapeDtypeStruct((lay.M, D), BF16),
        grid=(lay.nblocks,),
        in_specs=[pl.BlockSpec((te, RW_TILES, LANE), lambda i: (a_idx(i), 0, 0)),
                  pl.BlockSpec((te, RW_TILES, LANE), lambda i: (b_idx(i), 0, 0)),
                  pl.BlockSpec((te, D), lambda i: (s_idx(i), 0))]
                 + triple(D) + triple(LANE)
                 + [pl.BlockSpec((None, te, D), lambda i: (2, i, 0)),
                    pl.BlockSpec((2, D), lambda i: (0, 0))],
        out_specs=row,
        scratch_shapes=[pltpu.VMEM((te, D), F32)],
        compiler_params=_cparams(("parallel",)),
        name="rw_post",
    )(*o3, *v3, *rk3, proj4, ln)


def _head_minor(a):
    lead = a.shape[:-1]
    return a.reshape(lead + (RW_H, RW_N)).swapaxes(-1, -2).reshape(lead + (D,))


def _tiles_to_state(s):
    n = s.shape[0]
    s = s.reshape(n, RW_TILES, RW_FOLD, RW_TILES, RW_FOLD, RW_H)
    return s.transpose(0, 5, 3, 2, 1, 4).reshape(n, RW_H, RW_N, RW_N)


def _pad_cols(w, n):
    return jnp.pad(w, ((0, 0), (0, n - w.shape[1])))


def _pad_rows(w, n):
    return jnp.pad(w, ((0, n - w.shape[0]), (0, 0)))


def rwkv_layer(lay, h, shift_state, state_t, layer, s_states, v_first, w):
    B, T, DB, TS, Mp, M = lay.B, lay.T, lay.DB, lay.TS, lay.Mp, lay.M
    x6 = rw_mix(lay, h, jnp.repeat(shift_state, TS, axis=0), w["mu"])
    w_in = jnp.stack([_head_minor(w["w_in"][c]).astype(BF16) for c in (0, 1, 3, 2)])
    proj4 = matmul(x6, w_in, name="rw_proj")
    vres = v_first is not None
    l1 = [w["v1"]] if vres else []
    l1 += [w["w1"], w["a1"]]
    l2 = [w["v2"]] if vres else []
    l2 += [w["w2"], w["a2"]]
    l1 = jnp.stack([_pad_cols(a, RW_LORA_PAD) for a in l1]).astype(BF16)
    l2 = jnp.stack([_head_minor(_pad_rows(a, RW_LORA_PAD)) for a in l2]).astype(BF16)
    lo1 = matmul(x6, l1, x_off=3 if vres else 4, act_out="tanh_group",
                 tanh_group=1 if vres else 0, out_dtype=BF16, name="rw_lora1")
    lora = matmul(lo1, l2, name="rw_lora2")
    plist = [w["w0"], w["a0"], w["k_k"], w["k_a"], w["r_k"].reshape(-1)]
    if vres:
        plist.append(w["v0"])
    plist += [jnp.zeros((D,), F32)] * (8 - len(plist))
    params = _head_minor(jnp.stack(plist))
    assert B == 2
    o_a, o_b, p_st, (v_a, v_b), (rk_a, rk_b) = wkv_prompt(
        proj4, lora, params, v_first[:2] if vres else None, T)
    stack4_s, v_s, rk_s = rw_prep_sample(lay, proj4, lora, params, v_first[2] if vres else None)
    xs = jnp.concatenate([proj4[0:1, Mp:], stack4_s, v_s[None]], axis=0)
    xs = xs.reshape(6, DB, TS, RW_N, RW_H).transpose(0, 2, 4, 3, 1)
    o_s, s_states = wkv_sample(xs, state_t, layer, s_states)
    o_s = o_s.transpose(3, 0, 2, 1).reshape(lay.Ms, D)
    v3 = (v_a, v_b, v_s)
    v_first_out = v_first if vres else v3

    ln = _head_minor(jnp.stack([w["ln_w"], w["ln_b"]]))
    y = rw_post(lay, (o_a, o_b, o_s), v3, (rk_a, rk_b, rk_s), proj4, ln)
    w_out = w["w_out"].reshape(RW_H, RW_N, D).swapaxes(0, 1).reshape(D, D)
    o_proj = matmul2d(y, w_out.astype(BF16), name="rw_out")
    p_shift = h[T - 1:Mp:T]
    s_shift = h[Mp:].reshape(DB, TS, D)[:, -1]
    return o_proj, _tiles_to_state(p_st), p_shift, s_states, s_shift, v_first_out


def _rot_cols(w):
    half = MLA_ROPE // 2
    return jnp.concatenate([-w[..., half:], w[..., :half]], axis=-1)


def _mla_prep_kernel(pq_ref, pkv_ref, pkr_ref, gq_ref, gkv_ref, cs_ref,
                     cq_ref, ckv_ref, kr_ref, krb_ref):
    cq_ref[...] = _rms(pq_ref[...], gq_ref[...]).astype(BF16)
    ckv_ref[...] = _rms(pkv_ref[...], gkv_ref[...])
    t2 = pkr_ref[...]
    kr = t2 * cs_ref[0] + pltpu.roll(t2, MLA_ROPE, 1) * cs_ref[1]
    kr_ref[...] = kr[:, :MLA_ROPE]
    krb_ref[...] = kr.astype(BF16)


def mla_prep(lay, proj, g_q, g_kv, cs):
    te = lay.te
    M = lay.M
    return pl.pallas_call(
        _mla_prep_kernel,
        out_shape=(jax.ShapeDtypeStruct((M, MLA_QL), BF16),
                   jax.ShapeDtypeStruct((M, MLA_KVL), F32),
                   jax.ShapeDtypeStruct((M, MLA_ROPE), F32),
                   jax.ShapeDtypeStruct((M, LANE), BF16)),
        grid=(lay.nblocks,),
        in_specs=[pl.BlockSpec((te, MLA_QL), lambda i: (i, D // MLA_QL)),
                  pl.BlockSpec((te, MLA_KVL), lambda i: (i, D // MLA_KVL + 1)),
                  pl.BlockSpec((te, LANE), lambda i: (i, (D + MLA_QL + MLA_KVL) // LANE)),
                  pl.BlockSpec((1, MLA_QL), lambda i: (0, 0)),
                  pl.BlockSpec((1, MLA_KVL), lambda i: (0, 0)),
                  pl.BlockSpec((2, te, LANE), lambda i: (0, i, 0))],
        out_specs=(pl.BlockSpec((te, MLA_QL), lambda i: (i, 0)),
                   pl.BlockSpec((te, MLA_KVL), lambda i: (i, 0)),
                   pl.BlockSpec((te, MLA_ROPE), lambda i: (i, 0)),
                   pl.BlockSpec((te, LANE), lambda i: (i, 0))),
        compiler_params=_cparams(("parallel",)),
        name="mla_prep",
    )(proj, proj, proj, g_q[None], g_kv[None], cs)


def _mla_q_kernel(q_ref, cs_ref, qn_ref, qr_ref):
    for h in range(MLA_H):
        qn_ref[:, h * LANE:(h + 1) * LANE] = q_ref[:, 2 * h * LANE:(2 * h + 1) * LANE].astype(BF16)
        t2 = q_ref[:, (2 * h + 1) * LANE:(2 * h + 2) * LANE]
        qr = t2 * cs_ref[0] + pltpu.roll(t2, MLA_ROPE, 1) * cs_ref[1]
        qr_ref[:, h * LANE:(h + 1) * LANE] = qr.astype(BF16)


def mla_q(lay, q, cs):
    te = lay.te
    M = lay.M
    return pl.pallas_call(
        _mla_q_kernel,
        out_shape=(jax.ShapeDtypeStruct((M, MLA_H * LANE), BF16),
                   jax.ShapeDtypeStruct((M, MLA_H * LANE), BF16)),
        grid=(lay.nblocks,),
        in_specs=[pl.BlockSpec((te, MLA_H * 2 * LANE), lambda i: (i, 0)),
                  pl.BlockSpec((2, te, LANE), lambda i: (0, i, 0))],
        out_specs=(pl.BlockSpec((te, MLA_H * LANE), lambda i: (i, 0)),
                   pl.BlockSpec((te, MLA_H * LANE), lambda i: (i, 0))),
        compiler_params=_cparams(("parallel",)),
        name="mla_q",
    )(q, cs)


FLASH_HEADS = 2


def _flash_kernel(qn_ref, qr_ref, kn_ref, kr_ref, v_ref, o_ref, *, tq, tk):
    qi = pl.program_id(2)
    nt = (((1,), (1,)), ((), ()))
    hd = lambda u: slice(u * LANE, (u + 1) * LANE)
    qs = [jnp.concatenate([qn_ref[:, hd(u)], qr_ref[:, hd(u)]], axis=1) for u in range(FLASH_HEADS)]
    c = MLA_SCALE * math.log2(math.e)
    per_q = tq // tk

    def update(start, carry, diag):
        kr = kr_ref[pl.ds(start, tk), :]
        ss = [lax.dot_general(qs[u], jnp.concatenate([kn_ref[pl.ds(start, tk), hd(u)], kr], axis=1),
                              nt, preferred_element_type=F32) * c for u in range(FLASH_HEADS)]
        out = []
        for u in range(FLASH_HEADS):
            m, l, acc = carry[u]
            s = ss[u]
            if diag is not None:
                s = jnp.where(lax.broadcasted_iota(jnp.int32, (tq, tk), 1) + diag * tk
                              <= lax.broadcasted_iota(jnp.int32, (tq, tk), 0), s, -1e30)
            m_new = jnp.maximum(m, jnp.max(s, axis=1, keepdims=True))
            p = jnp.exp2(s - m_new)
            alpha = jnp.exp2(m - m_new)
            l = alpha * l + jnp.sum(p, axis=1, keepdims=True)
            acc = alpha * acc + jnp.dot(p.astype(BF16), v_ref[pl.ds(start, tk), hd(u)],
                                        preferred_element_type=F32)
            out.append((m_new, l, acc))
        return tuple(out)

    def body(ki, carry):
        return update(pl.multiple_of(ki * tk, tk), carry, None)

    init = (jnp.full((tq, 1), -1e30, F32), jnp.zeros((tq, 1), F32), jnp.zeros((tq, MLA_V), F32))
    carry = lax.fori_loop(0, qi * per_q, body, (init,) * FLASH_HEADS)
    for d in range(per_q):
        carry = update(pl.multiple_of(qi * tq + d * tk, tk), carry, d)
    for u in range(FLASH_HEADS):
        _, l, acc = carry[u]
        o_ref[:, hd(u)] = acc / l


def mla_flash(qn, qr, knv, krb, B, T):
    tq = _tile(T, 512)
    tk = tq
    nq = T // tq
    hw = FLASH_HEADS * LANE
    groups = MLA_H // FLASH_HEADS
    return pl.pallas_call(
        functools.partial(_flash_kernel, tq=tq, tk=tk),
        out_shape=jax.ShapeDtypeStruct((B * T, MLA_H * MLA_V), F32),
        grid=(B, groups, nq),
        in_specs=[pl.BlockSpec((tq, hw), lambda b, h, i: (b * nq + i, h)),
                  pl.BlockSpec((tq, hw), lambda b, h, i: (b * nq + i, h)),
                  pl.BlockSpec((T, hw), lambda b, h, i: (b, h)),
                  pl.BlockSpec((T, LANE), lambda b, h, i: (b, 0)),
                  pl.BlockSpec((T, hw), lambda b, h, i: (b, groups + h))],
        out_specs=pl.BlockSpec((tq, hw), lambda b, h, i: (b * nq + i, h)),
        compiler_params=_cparams(("parallel", "parallel", "arbitrary")),
        name="mla_flash",
    )(qn, qr, knv, krb, knv)


def _paged_kernel(pt_ref, ql_ref, qr_ref, cn_ref, kn_ref, *rest, pg, ts):
    ck_refs = rest[:pg]
    kr_refs = rest[pg:2 * pg]
    o_ref, m_ref, l_ref, acc_ref = rest[2 * pg:]
    p = pl.program_id(1)
    nt = (((1,), (1,)), ((), ()))

    @pl.when(p == 0)
    def _():
        m_ref[...] = jnp.full(m_ref.shape, -1e30, F32)
        l_ref[...] = jnp.zeros(l_ref.shape, F32)
        acc_ref[...] = jnp.zeros(acc_ref.shape, F32)

    rows = ts * MLA_H
    ql = ql_ref[...]
    qr = qr_ref[...][:, :MLA_ROPE]
    cks = [r[...].astype(BF16) for r in ck_refs]
    ck = jnp.concatenate(cks, axis=0)
    kr_t = jnp.concatenate([r[...].astype(BF16) for r in kr_refs], axis=1)
    s = jnp.concatenate([lax.dot_general(ql, c, nt, preferred_element_type=F32) for c in cks], axis=1)
    s = (s + jnp.dot(qr, kr_t, preferred_element_type=F32)) * MLA_SCALE
    m = m_ref[...]
    m_new = jnp.maximum(m, jnp.max(s, axis=1, keepdims=True))
    pr = jnp.exp(s - m_new)
    alpha = jnp.exp(m - m_new)
    l = alpha * l_ref[...] + jnp.sum(pr, axis=1, keepdims=True)
    acc = alpha * acc_ref[...] + jnp.dot(pr.astype(BF16), ck, preferred_element_type=F32)
    m = m_new
    m_ref[...] = m
    l_ref[...] = l
    acc_ref[...] = acc

    @pl.when(p == pl.num_programs(1) - 1)
    def _():
        qlf = ql.astype(F32)
        qrf = qr.astype(F32)
        tok = lax.broadcasted_iota(jnp.int32, (ts * MLA_H, 1), 0) // MLA_H
        cols = []
        for j in range(ts):
            cn = cn_ref[j:j + 1, :].astype(BF16).astype(F32)
            kn = kn_ref[j:j + 1, :MLA_ROPE].astype(F32)
            sj = (jnp.sum(qlf * cn, axis=1, keepdims=True)
                  + jnp.sum(qrf * kn, axis=1, keepdims=True)) * MLA_SCALE
            cols.append(jnp.where(tok >= j, sj, -1e30))
        m2 = m
        for sj in cols:
            m2 = jnp.maximum(m2, sj)
        alpha = jnp.exp(m - m2)
        l2 = alpha * l
        acc2 = alpha * acc
        for j, sj in enumerate(cols):
            pj = jnp.exp(sj - m2)
            l2 = l2 + pj
            acc2 = acc2 + pj.astype(BF16).astype(F32) * cn_ref[j:j + 1, :].astype(BF16).astype(F32)
        o_ref[...] = (acc2 / l2).astype(o_ref.dtype)


def mla_paged(page_table, ql, qr, ckv_new, krb_new, cache_ckv, cache_kr, layer, DB, TS):
    n_pages = page_table.shape[1]
    pg = _tile(n_pages, 32) if n_pages >= 8 else n_pages
    rows = TS * MLA_H
    pt = page_table.reshape(-1)

    def page_spec(i, shape):
        return pl.BlockSpec((None, None) + shape,
                            lambda b, p, pt_ref: (layer, pt_ref[b * n_pages + p * pg + i], 0, 0))

    cache_kr_t = jnp.swapaxes(cache_kr, 2, 3)
    per_b = lambda width: pl.BlockSpec((None, rows, width), lambda b, p, pt_ref: (b, 0, 0))
    new_b = lambda width: pl.BlockSpec((None, TS, width), lambda b, p, pt_ref: (b, 0, 0))
    grid_spec = pltpu.PrefetchScalarGridSpec(
        num_scalar_prefetch=1,
        grid=(DB, n_pages // pg),
        in_specs=[per_b(MLA_KVL), per_b(LANE), new_b(MLA_KVL), new_b(LANE)]
                 + [page_spec(i, (PAGE, MLA_KVL)) for i in range(pg)]
                 + [page_spec(i, (MLA_ROPE, PAGE)) for i in range(pg)],
        out_specs=per_b(MLA_KVL),
        scratch_shapes=[pltpu.VMEM((rows, 1), F32), pltpu.VMEM((rows, 1), F32),
                        pltpu.VMEM((rows, MLA_KVL), F32)],
    )
    return pl.pallas_call(
        functools.partial(_paged_kernel, pg=pg, ts=TS),
        out_shape=jax.ShapeDtypeStruct((DB, rows, MLA_KVL), BF16),
        grid_spec=grid_spec,
        compiler_params=_cparams(("parallel", "arbitrary")),
        name="mla_paged",
    )(pt, ql, qr, ckv_new, krb_new, *([cache_ckv] * pg), *([cache_kr_t] * pg))


def _pair_specs(lay):
    te, npb = lay.te, lay.n_p_blocks
    return [pl.BlockSpec((te, D), lambda i: (jnp.minimum(i, npb - 1), 0)),
            pl.BlockSpec((te, D), lambda i: (jnp.maximum(i - npb, 0), 0))]


def _gate_kernel(op_ref, os_ref, z_ref, y_ref, *, n_p_blocks):
    o = _read_rows((op_ref, os_ref), n_p_blocks)
    y_ref[...] = (o * _silu(z_ref[...])).astype(BF16)


def gate_mul(lay, o_pair, zsrc, zcol):
    te = lay.te
    return pl.pallas_call(
        functools.partial(_gate_kernel, n_p_blocks=lay.n_p_blocks),
        out_shape=jax.ShapeDtypeStruct((lay.M, D), BF16),
        grid=(lay.nblocks,),
        in_specs=_pair_specs(lay) + [pl.BlockSpec((te, D), lambda i: (i, zcol))],
        out_specs=pl.BlockSpec((te, D), lambda i: (i, 0)),
        compiler_params=_cparams(("parallel",)),
        name="gate_mul",
    )(*o_pair, zsrc)


def mla_layer(lay, h, cache_ckv, cache_kr, layer, page_table, w):
    B, T, DB, TS, Mp, M = lay.B, lay.T, lay.DB, lay.TS, lay.Mp, lay.M
    w_in = w["w_in"]
    w_kr = w_in[:, 2 * MLA_QL:2 * MLA_QL + MLA_ROPE]
    w1 = jnp.concatenate([w_in[:, 2 * MLA_QL + MLA_ROPE:], w_in[:, :2 * MLA_QL],
                          w_kr, _rot_cols(w_kr)], axis=1)
    proj = matmul2d(h, w1.astype(BF16), tn=640, name="mla_proj")
    pos = jnp.concatenate([jnp.tile(jnp.arange(T, dtype=F32), B),
                           jnp.tile(page_table.shape[1] * PAGE + jnp.arange(TS, dtype=F32), DB)])
    half = MLA_ROPE // 2
    inv_freq = ROPE_THETA ** (-jnp.arange(half, dtype=F32) / half)
    ang = pos[:, None] * inv_freq[None, :]
    zeros = jnp.zeros((M, MLA_ROPE), F32)
    cs = jnp.stack([jnp.concatenate([jnp.cos(ang), jnp.cos(ang), zeros], axis=1),
                    jnp.concatenate([jnp.sin(ang), jnp.sin(ang), zeros], axis=1)])
    cq, ckv, kr, krb = mla_prep(lay, proj, w["q_norm"], w["kv_norm"], cs)
    w_uq = w["w_uq"]
    wq = jnp.concatenate([w_uq, _rot_cols(w_uq[..., MLA_NOPE:])], axis=-1)
    q = matmul2d(cq, wq.reshape(MLA_QL, MLA_H * 2 * LANE).astype(BF16), name="mla_qproj")
    qn, qr = mla_q(lay, q, cs)
    w_kv = jnp.concatenate([w["w_uk"].reshape(MLA_KVL, -1), w["w_uv"].reshape(MLA_KVL, -1)], axis=1)
    knv = matmul2d(ckv[:Mp], w_kv.astype(BF16), out_dtype=BF16, name="mla_kv")
    o_p = mla_flash(qn, qr, knv, krb, B, T)
    w_ukT = w["w_uk"].transpose(1, 2, 0).astype(BF16)
    ql = head_matmul(qn[Mp:], w_ukT, BF16, "mla_qlat")
    o_lat = mla_paged(page_table, ql.reshape(DB, TS * MLA_H, MLA_KVL),
                      qr[Mp:].reshape(DB, TS * MLA_H, LANE),
                      ckv[Mp:].reshape(DB, TS, MLA_KVL), krb[Mp:].reshape(DB, TS, LANE),
                      cache_ckv, cache_kr, layer, DB, TS)
    o_s = head_matmul(o_lat.reshape(lay.Ms, MLA_H * MLA_KVL),
                      w["w_uv"].transpose(1, 0, 2).astype(BF16), F32, "mla_ouv")
    y = gate_mul(lay, (o_p, o_s), proj, 0)
    o_proj = matmul2d(y, w["w_out"].astype(BF16), name="mla_out")
    return (o_proj, ckv[:Mp].reshape(B, T, MLA_KVL), kr[:Mp].reshape(B, T, MLA_ROPE),
            ckv[Mp:].reshape(DB, TS, MLA_KVL), kr[Mp:].reshape(DB, TS, MLA_ROPE))


def _gla_kernel(q_ref, k_ref, v_ref, g_ref, s0_ref, o_ref, sT_ref, st_ref, *, L):
    c = pl.program_id(1)
    C = GLA_SUB
    assert L % C == 0
    heads = range(GLA_H)
    dk = lambda h: slice(h * GLA_DK, (h + 1) * GLA_DK)
    dv = lambda h: slice(h * GLA_DV, (h + 1) * GLA_DV)

    @pl.when(c == 0)
    def _():
        for h in heads:
            st_ref[h] = s0_ref[h].T

    nt = (((1,), (1,)), ((), ()))
    tn = (((0,), (0,)), ((), ()))
    trow = lax.broadcasted_iota(jnp.int32, (L, 1), 0)
    crow = lax.broadcasted_iota(jnp.int32, (C, 1), 0)
    col = lax.broadcasted_iota(jnp.int32, (C, L), 1)
    q = [q_ref[:, dk(h)] * (GLA_DK ** -0.5) for h in heads]
    k = [k_ref[:, dk(h)] for h in heads]
    v = [v_ref[:, dv(h)].astype(BF16) for h in heads]
    b = []
    for h in heads:
        bh = g_ref[:, dk(h)]
        sh = 1
        while sh < L:
            bh = bh + jnp.where(trow >= sh, pltpu.roll(bh, sh, 0), 0.0)
            sh *= 2
        b.append(bh)
    st = [st_ref[h] for h in heads]
    o = [lax.dot_general((q[h] * jnp.exp(b[h])).astype(BF16), st[h].astype(BF16), nt,
                         preferred_element_type=F32) for h in heads]
    att_rows = [[] for _ in heads]
    for i in range(L // C):
        lo = i * C
        if i == 0:
            att_i = [jnp.zeros((C, L), F32) for _ in heads]
        else:
            att_i = []
            for h in heads:
                beta = b[h][lo - 1:lo, :]
                q_in = (q[h][lo:lo + C] * jnp.exp(b[h][lo:lo + C] - beta)).astype(BF16)
                k_out = (k[h] * jnp.exp(jnp.where(trow < lo, beta - b[h], -jnp.inf))).astype(BF16)
                att_i.append(lax.dot_general(q_in, k_out, nt, preferred_element_type=F32))
        for s in range(C):
            for h in heads:
                bi = b[h][lo:lo + C]
                e = jnp.where(crow >= s, bi - bi[s:s + 1, :], -jnp.inf)
                a_s = jnp.sum(q[h][lo:lo + C] * k[h][lo + s:lo + s + 1, :] * jnp.exp(e),
                              axis=1, keepdims=True)
                att_i[h] = jnp.where(col == lo + s, a_s, att_i[h])
        for h in heads:
            att_rows[h].append(att_i[h])
    for h in heads:
        att = att_rows[h][0] if len(att_rows[h]) == 1 else jnp.concatenate(att_rows[h], axis=0)
        o_ref[:, dv(h)] = o[h] + jnp.dot(att.astype(BF16), v[h], preferred_element_type=F32)
    for h in heads:
        b_end = b[h][L - 1:L, :]
        kd = (k[h] * jnp.exp(b_end - b[h])).astype(BF16)
        st[h] = st[h] * jnp.exp(b_end) + lax.dot_general(v[h], kd, tn, preferred_element_type=F32)
        st_ref[h] = st[h]

    @pl.when(c == pl.num_programs(1) - 1)
    def _():
        for h in heads:
            sT_ref[h] = st[h].T


def gla_scan(q, k, v, g, s0, *, nseq, nchunk, L, kcol, vcol):
    rows = lambda n, c: n * nchunk + c
    st_spec = pl.BlockSpec((None, GLA_H, GLA_DK, GLA_DV), lambda n, c: (n, 0, 0, 0))
    return pl.pallas_call(
        functools.partial(_gla_kernel, L=L),
        out_shape=(jax.ShapeDtypeStruct((q.shape[0], GLA_H * GLA_DV), F32),
                   jax.ShapeDtypeStruct((nseq, GLA_H, GLA_DK, GLA_DV), F32)),
        grid=(nseq, nchunk),
        in_specs=[pl.BlockSpec((L, GLA_QK), lambda n, c: (rows(n, c), 0)),
                  pl.BlockSpec((L, GLA_QK), lambda n, c: (rows(n, c), kcol)),
                  pl.BlockSpec((L, GLA_H * GLA_DV), lambda n, c: (rows(n, c), vcol)),
                  pl.BlockSpec((L, GLA_QK), lambda n, c: (rows(n, c), 0)),
                  st_spec],
        out_specs=(pl.BlockSpec((L, GLA_H * GLA_DV), lambda n, c: (rows(n, c), 0)), st_spec),
        scratch_shapes=[pltpu.VMEM((GLA_H, GLA_DV, GLA_DK), F32)],
        compiler_params=_cparams(("parallel", "arbitrary")),
        name="gla_scan",
    )(q, k, v, g, s0)


def _gla_post_kernel(op_ref, os_ref, z_ref, g_ref, y_ref, *, n_p_blocks):
    o = _read_rows((op_ref, os_ref), n_p_blocks)
    for h in range(GLA_H):
        hs = slice(h * GLA_DV, (h + 1) * GLA_DV)
        y_ref[:, hs] = (_rms(o[:, hs], g_ref[...]) * _silu(z_ref[:, hs])).astype(BF16)


def gla_post(lay, o_pair, proj, g_norm):
    te = lay.te
    return pl.pallas_call(
        functools.partial(_gla_post_kernel, n_p_blocks=lay.n_p_blocks),
        out_shape=jax.ShapeDtypeStruct((lay.M, D), BF16),
        grid=(lay.nblocks,),
        in_specs=_pair_specs(lay) + [pl.BlockSpec((te, D), lambda i: (i, 2)),
                                     pl.BlockSpec((1, GLA_DV), lambda i: (0, 0))],
        out_specs=pl.BlockSpec((te, D), lambda i: (i, 0)),
        compiler_params=_cparams(("parallel",)),
        name="gla_post",
    )(*o_pair, proj, g_norm[None])


def gla_layer(lay, h, state, w):
    B, T, DB, TS, Mp, M = lay.B, lay.T, lay.DB, lay.TS, lay.Mp, lay.M
    w_in = _pad_cols(w["w_in"], 2 * GLA_QK + 2 * D + GLA_LORA_PAD)
    proj = matmul2d(h, w_in.astype(BF16), tn=640, name="gla_proj")
    gl = proj[:, 2 * GLA_QK + 2 * D:]
    g = matmul2d(gl, _pad_rows(w["w_g2"], GLA_LORA_PAD).astype(BF16), w["b_g"],
                 act_out="logsig_tau", name="gla_gate")
    L = math.gcd(T, GLA_CHUNK)
    cols = dict(kcol=1, vcol=2 * GLA_QK // (GLA_H * GLA_DV))
    o_p, p_state = gla_scan(proj, proj, proj, g, jnp.zeros((B, GLA_H, GLA_DK, GLA_DV), F32),
                            nseq=B, nchunk=T // L, L=L, **cols)
    LS = GLA_SUB
    pad = lambda a: jnp.pad(a.reshape(DB, TS, -1), ((0, 0), (0, LS - TS), (0, 0))).reshape(DB * LS, -1)
    ps = pad(proj[Mp:, :2 * GLA_QK + D])
    o_s, s_state = gla_scan(ps, ps, ps, pad(g[Mp:]), state, nseq=DB, nchunk=1, L=LS, **cols)
    o_s = o_s.reshape(DB, LS, D)[:, :TS].reshape(lay.Ms, D)
    y = gla_post(lay, (o_p, o_s), proj, w["norm"])
    o_proj = matmul2d(y, w["w_out"].astype(BF16), name="gla_out")
    return o_proj, p_state, s_state


def kernel(x_prompt, x_sample, c_prompt, c_sample, state_rwkv_wkv, state_rwkv_shift, cache_mla_ckv, cache_mla_krope, page_table, state_gla, norm_pre, norm_post, ada_w, ada_b, rw_mu, rw_w_in, rw_w0, rw_w1, rw_w2, rw_a0, rw_a1, rw_a2, rw_v0, rw_v1, rw_v2, rw_k_k, rw_k_a, rw_r_k, rw_ln_w, rw_ln_b, rw_w_out, mla_w_in, mla_q_norm, mla_kv_norm, mla_w_uq, mla_w_uk, mla_w_uv, mla_w_out, gla_w_in, gla_w_g2, gla_b_g, gla_norm, gla_w_out):
    B, T, _ = x_prompt.shape
    DB, TS, _ = x_sample.shape
    depth = norm_pre.shape[0]
    lay = RowLayout(B, T, DB, TS)
    Mp = lay.Mp
    x = (x_prompt.reshape(Mp, D), x_sample.reshape(lay.Ms, D))
    c = jnp.concatenate([c_prompt, c_sample], axis=0)[None]
    c = jnp.broadcast_to(c, (depth,) + c.shape[1:])
    mods = matmul(c, ada_w, ada_b[:, None, :], act_in="silu", name="ada")

    p_wkv, p_shift, p_ckv, p_kr, p_gla = [], [], [], [], []
    s_shift, s_ckv, s_kr, s_gla = [], [], [], []
    v_first = None
    state_t = jnp.transpose(state_rwkv_wkv, (0, 2, 3, 4, 1))
    s_wkv_t = None
    mod_of = [lay.expand_mod(mods[i]) for i in range(depth)]
    h_dtype = lambda i: F32 if i % 3 == 0 else BF16
    (h,) = norm_step(lay, x, pre=(norm_pre[0], mod_of[0], h_dtype(0)))
    for i in range(depth):
        kind, j = i % 3, i // 3
        if kind == 0:
            w = dict(mu=rw_mu[j], w_in=rw_w_in[j], w0=rw_w0[j], w1=rw_w1[j], w2=rw_w2[j],
                     a0=rw_a0[j], a1=rw_a1[j], a2=rw_a2[j], k_k=rw_k_k[j], k_a=rw_k_a[j],
                     r_k=rw_r_k[j], ln_w=rw_ln_w[j], ln_b=rw_ln_b[j], w_out=rw_w_out[j])
            if j > 0:
                w.update(v0=rw_v0[j - 1], v1=rw_v1[j - 1], v2=rw_v2[j - 1])
            o, pst, psh, s_wkv_t, ssh, v_first = rwkv_layer(
                lay, h, state_rwkv_shift[j], state_t, j, s_wkv_t, v_first if j > 0 else None, w)
            p_wkv.append(pst)
            p_shift.append(psh)
            s_shift.append(ssh)
        elif kind == 1:
            w = dict(w_in=mla_w_in[j], q_norm=mla_q_norm[j], kv_norm=mla_kv_norm[j],
                     w_uq=mla_w_uq[j], w_uk=mla_w_uk[j], w_uv=mla_w_uv[j], w_out=mla_w_out[j])
            o, pc, pk, sc, sk = mla_layer(lay, h, cache_mla_ckv, cache_mla_krope, j, page_table, w)
            p_ckv.append(pc)
            p_kr.append(pk)
            s_ckv.append(sc)
            s_kr.append(sk)
        else:
            w = dict(w_in=gla_w_in[j], w_g2=gla_w_g2[j], b_g=gla_b_g[j], norm=gla_norm[j],
                     w_out=gla_w_out[j])
            o, pg, sg = gla_layer(lay, h, state_gla[j], w)
            p_gla.append(pg)
            s_gla.append(sg)
        res = (o, norm_post[i], mod_of[i])
        if i + 1 < depth:
            x, h = norm_step(lay, x, res=res, pre=(norm_pre[i + 1], mod_of[i + 1], h_dtype(i + 1)))
        else:
            yp, ys = norm_step(lay, x, res=res, split_out=True)
    yp = yp.reshape(B, T, D)
    ys = ys.reshape(DB, TS, D)
    return (yp, ys,
            jnp.stack(p_wkv), jnp.stack(p_shift), jnp.stack(p_ckv), jnp.stack(p_kr), jnp.stack(p_gla),
            jnp.transpose(s_wkv_t, (0, 4, 1, 2, 3)), jnp.stack(s_shift), jnp.stack(s_ckv),
            jnp.stack(s_kr), jnp.stack(s_gla))
```

```python
import functools
import math

import jax
import jax.numpy as jnp
import numpy as np
from jax import lax
from jax.experimental import pallas as pl
from jax.experimental.pallas import tpu as pltpu

F32 = jnp.float32
BF16 = jnp.bfloat16

D = 2048
NORM_EPS = 1e-6
RW_N = 64
RW_H = D // RW_N
RW_LORA_PAD = 128
RW_GN_EPS = 64e-5
RW_TILES = D // 128
RW_FOLD = 128 // RW_H
EXP_M05 = math.exp(-0.5)
MLA_H = 16
MLA_NOPE = 128
MLA_ROPE = 64
MLA_V = 128
MLA_QL = 512
MLA_KVL = 512
MLA_SCALE = (MLA_NOPE + MLA_ROPE) ** -0.5
ROPE_THETA = 10000.0
PAGE = 128
GLA_H = 4
GLA_DK = 256
GLA_DV = 512
GLA_QK = GLA_H * GLA_DK
GLA_LORA_PAD = 256
GLA_TAU = 16.0
GLA_CHUNK = 64
GLA_SUB = 16

LANE = 128
ROW_TILE = 128
VMEM_LIMIT = 48 * 1024 * 1024


def _cparams(sem):
    return pltpu.CompilerParams(dimension_semantics=sem, vmem_limit_bytes=VMEM_LIMIT)


def _tile(n, pref):
    for t in (1024, 640, 512, 384, 256, 128, 64, 32, 16, 8):
        if t <= pref and n % t == 0:
            return t
    return n


def _sigmoid(x):
    return 1.0 / (1.0 + jnp.exp(-x))


def _silu(x):
    return x * _sigmoid(x)


def _mm_kernel(*refs, has_bias, act_in, act_out, tanh_group):
    if has_bias:
        x_ref, w_ref, b_ref, o_ref = refs
    else:
        x_ref, w_ref, o_ref = refs
    x = x_ref[...]
    if act_in == "silu":
        x = _silu(x.astype(F32))
    acc = jnp.dot(x.astype(BF16), w_ref[...].astype(BF16), preferred_element_type=F32)
    if has_bias:
        acc = acc + b_ref[...]
    if act_out == "tanh_group":
        acc = jnp.where(pl.program_id(0) == tanh_group, jnp.tanh(acc), acc)
    elif act_out == "logsig_tau":
        acc = (jnp.minimum(acc, 0.0) - jnp.log(1.0 + jnp.exp(-jnp.abs(acc)))) * (1.0 / GLA_TAU)
    o_ref[...] = acc.astype(o_ref.dtype)


MM_X_TILE_BYTES = 9 * 1024 * 1024
MM_MAX_ROWS = 2304


def _row_tile(M, row_bytes):
    for parts in range(1, M // 8 + 1):
        rows = M // parts
        if M % parts == 0 and rows % 8 == 0 and rows <= MM_MAX_ROWS and rows * row_bytes <= MM_X_TILE_BYTES:
            return rows
    return M


def matmul(x, w, bias=None, *, x_off=0, act_in=None, act_out=None, tanh_group=0,
           out_dtype=F32, tn=512, name="mm"):
    G, K, N = w.shape
    M = x.shape[1]
    tm = _row_tile(M, K * x.dtype.itemsize)
    tn = _tile(N, tn)
    in_specs = [
        pl.BlockSpec((None, tm, K), lambda g, i, j: (g + x_off, i, 0)),
        pl.BlockSpec((None, K, tn), lambda g, i, j: (g, 0, j)),
    ]
    args = [x, w]
    if bias is not None:
        in_specs.append(pl.BlockSpec((None, 1, tn), lambda g, i, j: (g, 0, j)))
        args.append(bias)
    return pl.pallas_call(
        functools.partial(_mm_kernel, has_bias=bias is not None, act_in=act_in,
                          act_out=act_out, tanh_group=tanh_group),
        out_shape=jax.ShapeDtypeStruct((G, M, N), out_dtype),
        grid=(G, M // tm, N // tn),
        in_specs=in_specs,
        out_specs=pl.BlockSpec((None, tm, tn), lambda g, i, j: (g, i, j)),
        compiler_params=_cparams(("parallel", "parallel", "arbitrary")),
        name=name,
    )(*args)


def matmul2d(x, w, bias=None, **kw):
    b3 = None if bias is None else bias[None, None, :]
    return matmul(x[None], w[None], b3, **kw)[0]


def _hmm_kernel(x_ref, w_ref, o_ref):
    o_ref[...] = jnp.dot(x_ref[...].astype(BF16), w_ref[...],
                         preferred_element_type=F32).astype(o_ref.dtype)


def head_matmul(x, w, out_dtype, name):
    H, Kh, Nh = w.shape
    M = x.shape[0]
    tm = _tile(M, 512)
    return pl.pallas_call(
        _hmm_kernel,
        out_shape=jax.ShapeDtypeStruct((M, H * Nh), out_dtype),
        grid=(H, M // tm),
        in_specs=[pl.BlockSpec((tm, Kh), lambda h, i: (i, h)),
                  pl.BlockSpec((None, Kh, Nh), lambda h, i: (h, 0, 0))],
        out_specs=pl.BlockSpec((tm, Nh), lambda h, i: (i, h)),
        compiler_params=_cparams(("parallel", "arbitrary")),
        name=name,
    )(x, w)


class RowLayout:
    def __init__(self, B, T, DB, TS):
        self.B, self.T, self.DB, self.TS = B, T, DB, TS
        self.Mp = B * T
        self.Ms = DB * TS
        self.M = self.Mp + self.Ms
        self.te = math.gcd(math.gcd(ROW_TILE, T), self.Ms)
        self.n_p_blocks = self.Mp // self.te
        self.blocks_per_seq = T // self.te
        self.nblocks = self.M // self.te

    def mod_index(self, i):
        return jnp.where(i < self.n_p_blocks, i // self.blocks_per_seq,
                         self.B + i - self.n_p_blocks)

    def expand_mod(self, m):
        mp = jnp.broadcast_to(m[:self.B, None, :], (self.B, self.te, m.shape[-1]))
        ms = jnp.repeat(m[self.B:], self.TS, axis=0).reshape(self.Ms // self.te, self.te, -1)
        return jnp.concatenate([mp, ms], axis=0)


def _rms(x, g):
    ms = jnp.mean(x * x, axis=-1, keepdims=True)
    return x * lax.rsqrt(ms + NORM_EPS) * g


def _read_rows(refs, n_p_blocks):
    if len(refs) == 1:
        return refs[0][...]
    return jnp.where(pl.program_id(0) < n_p_blocks, refs[0][...], refs[1][...])


RW_MIX_ORDER = (0, 1, 3, 2, 4, 5)
SUBLANES = 8


def _norm_step_kernel(*refs, nx, residual, prenorm, mix, split_out, n_p_blocks, blocks_per_seq, ts):
    refs = list(refs)
    x = _read_rows(refs[:nx], n_p_blocks)
    pos = nx
    if residual:
        o_ref, gpost_ref, gate_ref = refs[pos:pos + 3]
        pos += 3
        x = x + gate_ref[...] * _rms(o_ref[...], gpost_ref[...])
    if prenorm:
        gpre_ref, shift_ref, scale_ref = refs[pos:pos + 3]
        pos += 3
    if mix:
        first_ref, mu_ref = refs[pos:pos + 2]
        pos += 2
        tail_scr = refs[-1]
        refs = refs[:-1]
    outs = refs[pos:]
    if residual:
        if split_out:
            @pl.when(pl.program_id(0) < n_p_blocks)
            def _():
                outs[0][...] = x

            @pl.when(pl.program_id(0) >= n_p_blocks)
            def _():
                outs[1][...] = x
            outs = outs[2:]
        else:
            outs[0][...] = x
            outs = outs[1:]
    if prenorm:
        h = _rms(x, gpre_ref[...]) * (1.0 + scale_ref[...]) + shift_ref[...]
        outs[0][...] = h.astype(outs[0].dtype)
    if mix:
        i = pl.program_id(0)

        @pl.when(i == 0)
        def _():
            tail_scr[...] = jnp.zeros(tail_scr.shape, F32)

        rows = h.shape[0]
        ridx = lax.broadcasted_iota(jnp.int32, (rows, 1), 0)
        prev = pltpu.roll(h, 1, 0)
        opens = jnp.logical_and(i < n_p_blocks, i % blocks_per_seq == 0)
        tail = jnp.where(opens, 0.0, tail_scr[SUBLANES - 1:SUBLANES, :])
        prev = jnp.where(ridx == 0, tail, prev)
        prev = jnp.where(jnp.logical_and(i >= n_p_blocks, ridx % ts == 0), first_ref[...], prev)
        tail_scr[...] = h[rows - SUBLANES:, :]
        dx = prev - h
        for o, c in enumerate(RW_MIX_ORDER):
            outs[1][o] = (h + dx * mu_ref[c:c + 1, :]).astype(BF16)


def norm_step(lay, x, *, res=None, pre=None, mix=None, split_out=False):
    te, npb = lay.te, lay.n_p_blocks
    row = pl.BlockSpec((te, D), lambda i: (i, 0))
    vec = pl.BlockSpec((1, D), lambda i: (0, 0))
    modcol = lambda c: pl.BlockSpec((None, te, D), lambda i: (lay.mod_index(i), 0, c))
    p_row = pl.BlockSpec((te, D), lambda i: (jnp.minimum(i, npb - 1), 0))
    s_row = pl.BlockSpec((te, D), lambda i: (jnp.maximum(i - npb, 0), 0))
    xs = list(x) if isinstance(x, tuple) else [x]
    nx = len(xs)
    in_specs = [p_row, s_row] if nx == 2 else [row]
    args = list(xs)
    out_shape, out_specs = [], []
    if res is not None:
        o, g_post, mod = res
        in_specs += [row, vec, modcol(2)]
        args += [o, g_post[None], mod]
        if split_out:
            out_shape += [jax.ShapeDtypeStruct((lay.Mp, D), F32), jax.ShapeDtypeStruct((lay.Ms, D), F32)]
            out_specs += [p_row, s_row]
        else:
            out_shape.append(jax.ShapeDtypeStruct((lay.M, D), F32))
            out_specs.append(row)
    if pre is not None:
        g_pre, mod, dtype = pre
        in_specs += [vec, modcol(0), modcol(1)]
        args += [g_pre[None], mod, mod]
        out_shape.append(jax.ShapeDtypeStruct((lay.M, D), dtype))
        out_specs.append(row)
    scratch = []
    if mix is not None:
        shift_rows, mu = mix
        in_specs += [s_row, pl.BlockSpec((6, D), lambda i: (0, 0))]
        args += [shift_rows, mu]
        out_shape.append(jax.ShapeDtypeStruct((6, lay.M, D), BF16))
        out_specs.append(pl.BlockSpec((6, te, D), lambda i: (0, i, 0)))
        scratch.append(pltpu.VMEM((SUBLANES, D), F32))
    return pl.pallas_call(
        functools.partial(_norm_step_kernel, nx=nx, residual=res is not None,
                          prenorm=pre is not None, mix=mix is not None, split_out=split_out,
                          n_p_blocks=npb, blocks_per_seq=lay.blocks_per_seq, ts=lay.TS),
        out_shape=tuple(out_shape),
        grid=(lay.nblocks,),
        in_specs=in_specs,
        out_specs=tuple(out_specs),
        scratch_shapes=scratch,
        compiler_params=_cparams(("arbitrary",)),
        name="norm_step",
    )(*args)


def _head_fold(s):
    s = s + pltpu.roll(s, RW_H, 1)
    return s + pltpu.roll(s, 2 * RW_H, 1)


def _rw_prep_kernel(*refs, vres):
    if vres:
        p_ref, l_ref, par_ref, vf_ref, out_ref, v_ref, rk_ref = refs
    else:
        p_ref, l_ref, par_ref, out_ref, v_ref, rk_ref = refs
        vf_ref = None
    _rw_prep_rows(p_ref, l_ref, par_ref, vf_ref, out_ref, v_ref, rk_ref)


def _rw_prep_rows(p_ref, l_ref, par_ref, vf_ref, out_ref, v_ref, rk_ref):
    vres = vf_ref is not None
    iw, ia = (1, 2) if vres else (0, 1)
    rows = out_ref.shape[1]
    n2 = jnp.zeros((rows, LANE), F32)
    rk = jnp.zeros((rows, LANE), F32)
    for j in range(RW_TILES):
        js = slice(j * LANE, (j + 1) * LANE)
        par = lambda i: par_ref[i:i + 1, js]
        r = p_ref[0, :, js]
        k = p_ref[1, :, js]
        v = p_ref[3, :, js]
        decay = jnp.exp(-EXP_M05 * _sigmoid(par(0) + l_ref[iw, :, js]))
        a = _sigmoid(par(1) + l_ref[ia, :, js])
        kk = k * par(2)
        n2 = n2 + kk * kk
        kmod = k * (1.0 + (a - 1.0) * par(3))
        rk = rk + r * kmod * par(4)
        if vres:
            v = v + (vf_ref[:, js] - v) * _sigmoid(par(5) + l_ref[0, :, js])
        out_ref[0, :, js] = decay
        out_ref[1, :, js] = kmod
        out_ref[2, :, js] = kk
        out_ref[3, :, js] = a
        v_ref[:, js] = v
    inv = 1.0 / jnp.maximum(jnp.sqrt(_head_fold(n2)), 1e-12)
    rk_ref[...] = _head_fold(rk)
    for j in range(RW_TILES):
        js = slice(j * LANE, (j + 1) * LANE)
        kkn = out_ref[2, :, js] * inv
        out_ref[2, :, js] = kkn
        out_ref[3, :, js] = kkn * out_ref[3, :, js]


def rw_prep_sample(lay, proj, lora, params, v_first_s):
    te, npb = lay.te, lay.n_p_blocks
    vres = v_first_s is not None
    nl = lora.shape[0]
    in_specs = [pl.BlockSpec((4, te, D), lambda i: (0, npb + i, 0)),
                pl.BlockSpec((nl, te, D), lambda i: (0, npb + i, 0)),
                pl.BlockSpec(params.shape, lambda i: (0, 0))]
    args = [proj, lora, params]
    if vres:
        in_specs.append(pl.BlockSpec((te, D), lambda i: (i, 0)))
        args.append(v_first_s)
    return pl.pallas_call(
        functools.partial(_rw_prep_kernel, vres=vres),
        out_shape=(jax.ShapeDtypeStruct((4, lay.Ms, D), F32),
                   jax.ShapeDtypeStruct((lay.Ms, D), F32),
                   jax.ShapeDtypeStruct((lay.Ms, LANE), F32)),
        grid=(lay.Ms // te,),
        in_specs=in_specs,
        out_specs=(pl.BlockSpec((4, te, D), lambda i: (0, i, 0)),
                   pl.BlockSpec((te, D), lambda i: (i, 0)),
                   pl.BlockSpec((te, LANE), lambda i: (i, 0))),
        compiler_params=_cparams(("parallel",)),
        name="rw_prep",
    )(*args)


def _wkv_kernel(*refs, steps, vres):
    n_in = 4 + (2 if vres else 0)
    p_a, p_b, l_a, l_b = refs[:4]
    vf_a, vf_b = (refs[4], refs[5]) if vres else (None, None)
    par_ref, rep_ref = refs[n_in:n_in + 2]
    o_a, o_b, st_ref, vout_a, vout_b, rk_a, rk_b = refs[n_in + 2:n_in + 9]
    vrep_a, vrep_b, s_scr, v3_scr = refs[n_in + 9:]

    @pl.when(pl.program_id(0) == 0)
    def _():
        st_ref[...] = jnp.zeros(st_ref.shape, F32)

    _rw_prep_rows(p_a, l_a, par_ref, vf_a, s_scr.at[0], vout_a, rk_a)
    _rw_prep_rows(p_b, l_b, par_ref, vf_b, s_scr.at[1], vout_b, rk_b)
    for q, vout in enumerate((vout_a, vout_b)):
        for j in range(RW_TILES):
            v3_scr[q, :, j, :] = vout[:, j * LANE:(j + 1) * LANE]

    r_a, r_b = p_a.at[0], p_b.at[0]
    s_a, s_b = s_scr.at[0], s_scr.at[1]
    seqs = ((r_a, s_a, v3_scr.at[0], o_a, vrep_a), (r_b, s_b, v3_scr.at[1], o_b, vrep_b))
    for (_, _, v_ref, _, vrep) in seqs:
        y = v_ref[...].reshape(steps * RW_TILES, LANE)
        hi = y.astype(BF16)
        lo = (y - hi.astype(F32)).astype(BF16)
        for g in range(RW_FOLD):
            yr = (jnp.dot(hi, rep_ref[g], preferred_element_type=F32)
                  + jnp.dot(lo, rep_ref[g], preferred_element_type=F32))
            vrep[:, g * RW_TILES:(g + 1) * RW_TILES, :] = yr.reshape(steps, RW_TILES, LANE)

    lane_group = lax.broadcasted_iota(jnp.int32, (RW_TILES, LANE), 1) // RW_H

    def kk_dot(q, s_ref):
        p = jnp.zeros((RW_N, LANE), F32)
        for j in range(RW_TILES):
            p = p + st_ref[q, j] * s_ref[2, 0:1, j * LANE:(j + 1) * LANE]
        return p

    def step(t, carry):
        t_next = jnp.minimum(t + 1, steps - 1)
        out, raw = [], []
        for q, (r_ref, s_ref, _, o_ref, vrep) in enumerate(seqs):
            sa = -(carry[q] if q == 0 else _head_fold(carry[q]))
            vr = vrep[t]
            w_row = s_ref[0, pl.ds(t, 1), :]
            k_row = s_ref[1, pl.ds(t, 1), :]
            b_row = s_ref[3, pl.ds(t, 1), :]
            r_row = r_ref[pl.ds(t, 1), :]
            kk_row = s_ref[2, pl.ds(t_next, 1), :]
            o = jnp.zeros((RW_N, LANE), F32)
            p = jnp.zeros((RW_N, LANE), F32)
            for j in range(RW_TILES):
                js = slice(j * LANE, (j + 1) * LANE)
                sn = st_ref[q, j] * w_row[:, js] + sa * b_row[:, js] + vr * k_row[:, js]
                st_ref[q, j] = sn
                o = o + sn * r_row[:, js]
                p = p + sn * kk_row[:, js]
            out.append(_head_fold(p) if q == 0 else p)
            emit(q, jnp.maximum(t - 1, 0), carry[2 + q])
            raw.append(o)
        return tuple(out + raw)

    def emit(q, t, o):
        o_ref = seqs[q][3]
        o = _head_fold(o)
        res = o[0:RW_TILES]
        for g in range(1, RW_FOLD):
            res = jnp.where(lane_group == g, o[g * RW_TILES:(g + 1) * RW_TILES], res)
        o_ref[t] = res

    init = tuple(kk_dot(q, s[1]) for q, s in enumerate(seqs))
    zero = jnp.zeros((RW_N, LANE), F32)
    last = lax.fori_loop(0, steps, step, (_head_fold(init[0]), init[1], zero, zero))
    for q in range(2):
        emit(q, steps - 1, last[2 + q])


def wkv_prompt(proj4, lora, params, v_first_p, T):
    tb = _tile(T, 64)
    nc = T // tb
    nl = lora.shape[0]
    vres = v_first_p is not None
    lane = np.arange(LANE)
    rep = jnp.asarray(np.stack([(lane[:, None] // RW_H == g) & (lane[:, None] % RW_H == lane[None, :] % RW_H)
                                for g in range(RW_FOLD)]), BF16)
    pspec = lambda s: pl.BlockSpec((4, tb, D), lambda c: (0, s * nc + c, 0))
    lspec = lambda s: pl.BlockSpec((nl, tb, D), lambda c: (0, s * nc + c, 0))
    rows = lambda width: pl.BlockSpec((tb, width), lambda c: (c, 0))
    ospec = pl.BlockSpec((tb, RW_TILES, LANE), lambda c: (c, 0, 0))
    in_specs = [pspec(0), pspec(1), lspec(0), lspec(1)]
    args = [proj4, proj4, lora, lora]
    if vres:
        in_specs += [rows(D), rows(D)]
        args += list(v_first_p)
    in_specs += [pl.BlockSpec(params.shape, lambda c: (0, 0)),
                 pl.BlockSpec((RW_FOLD, LANE, LANE), lambda c: (0, 0, 0))]
    args += [params, rep]
    o_a, o_b, st, v_a, v_b, rk_a, rk_b = pl.pallas_call(
        functools.partial(_wkv_kernel, steps=tb, vres=vres),
        out_shape=(jax.ShapeDtypeStruct((T, RW_TILES, LANE), F32),
                   jax.ShapeDtypeStruct((T, RW_TILES, LANE), F32),
                   jax.ShapeDtypeStruct((2, RW_TILES, RW_N, LANE), F32),
                   jax.ShapeDtypeStruct((T, D), F32), jax.ShapeDtypeStruct((T, D), F32),
                   jax.ShapeDtypeStruct((T, LANE), F32), jax.ShapeDtypeStruct((T, LANE), F32)),
        grid=(nc,),
        in_specs=in_specs,
        out_specs=(ospec, ospec, pl.BlockSpec((2, RW_TILES, RW_N, LANE), lambda c: (0, 0, 0, 0)),
                   rows(D), rows(D), rows(LANE), rows(LANE)),
        scratch_shapes=[pltpu.VMEM((tb, RW_N, LANE), F32),
                        pltpu.VMEM((tb, RW_N, LANE), F32),
                        pltpu.VMEM((2, 4, tb, D), F32),
                        pltpu.VMEM((2, tb, RW_TILES, LANE), F32)],
        compiler_params=_cparams(("arbitrary",)),
        name="wkv",
    )(*args)
    return o_a, o_b, st, (v_a, v_b), (rk_a, rk_b)


def _wkv_sample_kernel(x_ref, s0_ref, *rest, steps):
    o_ref, st_ref = rest[-2:]

    def row(v, carry):
        s = s0_ref[v]
        for t in range(steps):
            p = jnp.sum(s * x_ref[3, t], axis=0, keepdims=True)
            vt = x_ref[5, t, pl.ds(v, 1), :]
            s = s * x_ref[1, t] - p * x_ref[4, t] + vt * x_ref[2, t]
            o_ref[t, pl.ds(v, 1), :] = jnp.sum(s * x_ref[0, t], axis=0, keepdims=True)
        st_ref[v] = s
        return carry

    lax.fori_loop(0, RW_N, row, 0, unroll=8)


def wkv_sample(x6, state_t, layer, into=None):
    _, TS, H, _, DB = x6.shape
    L = state_t.shape[0]
    st_spec = pl.BlockSpec((None, None, RW_N, RW_N, DB), lambda h: (layer, h, 0, 0, 0))
    in_specs = [pl.BlockSpec((6, TS, None, RW_N, DB), lambda h: (0, 0, h, 0, 0)), st_spec]
    args = [x6, state_t]
    aliases = {}
    if into is not None:
        aliases = {2: 1}
        in_specs.append(pl.BlockSpec(memory_space=pl.ANY))
        args.append(into)
    return pl.pallas_call(
        functools.partial(_wkv_sample_kernel, steps=TS),
        out_shape=(jax.ShapeDtypeStruct((TS, H, RW_N, DB), F32),
                   jax.ShapeDtypeStruct((L, H, RW_N, RW_N, DB), F32)),
        grid=(H,),
        in_specs=in_specs,
        out_specs=(pl.BlockSpec((TS, None, RW_N, DB), lambda h: (0, h, 0, 0)), st_spec),
        input_output_aliases=aliases,
        compiler_params=_cparams(("parallel",)),
        name="wkv_sample",
    )(*args)


def _rw_post_kernel(oa_ref, ob_ref, os_ref, va_ref, vb_ref, vs_ref, rka_ref, rkb_ref, rks_ref,
                    z_ref, ln_ref, y_ref, o_scr, *, n_seq_blocks):
    i = pl.program_id(0)
    rows = y_ref.shape[0]

    def run(o_tile, v_ref, rk_ref):
        s = jnp.zeros((rows, LANE), F32)
        for j in range(RW_TILES):
            js = slice(j * LANE, (j + 1) * LANE)
            oj = o_tile(j)
            o_scr[:, js] = oj
            s = s + oj
        mean = _head_fold(s) * (1.0 / RW_N)
        s2 = jnp.zeros((rows, LANE), F32)
        for j in range(RW_TILES):
            d = o_scr[:, j * LANE:(j + 1) * LANE] - mean
            s2 = s2 + d * d
        rstd = lax.rsqrt(_head_fold(s2) * (1.0 / RW_N) + RW_GN_EPS)
        rk = rk_ref[...]
        for j in range(RW_TILES):
            js = slice(j * LANE, (j + 1) * LANE)
            o = (o_scr[:, js] - mean) * rstd * ln_ref[0:1, js] + ln_ref[1:2, js]
            o = o + rk * v_ref[:, js]
            y_ref[:, js] = (o * _silu(z_ref[:, js])).astype(BF16)

    @pl.when(i < n_seq_blocks)
    def _():
        run(lambda j: oa_ref[:, j, :], va_ref, rka_ref)

    @pl.when(jnp.logical_and(i >= n_seq_blocks, i < 2 * n_seq_blocks))
    def _():
        run(lambda j: ob_ref[:, j, :], vb_ref, rkb_ref)

    @pl.when(i >= 2 * n_seq_blocks)
    def _():
        run(lambda j: os_ref[:, j * LANE:(j + 1) * LANE], vs_ref, rks_ref)


def rw_post(lay, o3, v3, rk3, proj4, ln):
    te = lay.te
    row = pl.BlockSpec((te, D), lambda i: (i, 0))
    nsb = lay.blocks_per_seq
    npb = lay.n_p_blocks
    a_idx = lambda i: jnp.where(i < nsb, i, 0)
    b_idx = lambda i: jnp.where(jnp.logical_and(i >= nsb, i < 2 * nsb), i - nsb, 0)
    s_idx = lambda i: jnp.maximum(i - npb, 0)
    triple = lambda width: [pl.BlockSpec((te, width), lambda i: (a_idx(i), 0)),
                            pl.BlockSpec((te, width), lambda i: (b_idx(i), 0)),
                            pl.BlockSpec((te, width), lambda i: (s_idx(i), 0))]
    return pl.pallas_call(
        functools.partial(_rw_post_kernel, n_seq_blocks=nsb),
        out_shape=jax.ShapeDtypeStruct((lay.M, D), BF16),
        grid=(lay.nblocks,),
        in_specs=[pl.BlockSpec((te, RW_TILES, LANE), lambda i: (a_idx(i), 0, 0)),
                  pl.BlockSpec((te, RW_TILES, LANE), lambda i: (b_idx(i), 0, 0)),
                  pl.BlockSpec((te, D), lambda i: (s_idx(i), 0))]
                 + triple(D) + triple(LANE)
                 + [pl.BlockSpec((None, te, D), lambda i: (2, i, 0)),
                    pl.BlockSpec((2, D), lambda i: (0, 0))],
        out_specs=row,
        scratch_shapes=[pltpu.VMEM((te, D), F32)],
        compiler_params=_cparams(("parallel",)),
        name="rw_post",
    )(*o3, *v3, *rk3, proj4, ln)


def _head_minor(a):
    lead = a.shape[:-1]
    return a.reshape(lead + (RW_H, RW_N)).swapaxes(-1, -2).reshape(lead + (D,))


def _tiles_to_state(s):
    n = s.shape[0]
    s = s.reshape(n, RW_TILES, RW_FOLD, RW_TILES, RW_FOLD, RW_H)
    return s.transpose(0, 5, 3, 2, 1, 4).reshape(n, RW_H, RW_N, RW_N)


def _pad_cols(w, n):
    return jnp.pad(w, ((0, 0), (0, n - w.shape[1])))


def _pad_rows(w, n):
    return jnp.pad(w, ((0, n - w.shape[0]), (0, 0)))


def rwkv_layer(lay, h, x6, state_t, layer, s_states, v_first, w):
    B, T, DB, TS, Mp, M = lay.B, lay.T, lay.DB, lay.TS, lay.Mp, lay.M
    w_in = jnp.stack([_head_minor(w["w_in"][c]).astype(BF16) for c in (0, 1, 3, 2)])
    proj4 = matmul(x6, w_in, name="rw_proj")
    vres = v_first is not None
    l1 = [w["v1"]] if vres else []
    l1 += [w["w1"], w["a1"]]
    l2 = [w["v2"]] if vres else []
    l2 += [w["w2"], w["a2"]]
    l1 = jnp.stack([_pad_cols(a, RW_LORA_PAD) for a in l1]).astype(BF16)
    l2 = jnp.stack([_head_minor(_pad_rows(a, RW_LORA_PAD)) for a in l2]).astype(BF16)
    lo1 = matmul(x6, l1, x_off=3 if vres else 4, act_out="tanh_group",
                 tanh_group=1 if vres else 0, out_dtype=BF16, name="rw_lora1")
    lora = matmul(lo1, l2, name="rw_lora2")
    plist = [w["w0"], w["a0"], w["k_k"], w["k_a"], w["r_k"].reshape(-1)]
    if vres:
        plist.append(w["v0"])
    plist += [jnp.zeros((D,), F32)] * (8 - len(plist))
    params = _head_minor(jnp.stack(plist))
    assert B == 2
    o_a, o_b, p_st, (v_a, v_b), (rk_a, rk_b) = wkv_prompt(
        proj4, lora, params, v_first[:2] if vres else None, T)
    stack4_s, v_s, rk_s = rw_prep_sample(lay, proj4, lora, params, v_first[2] if vres else None)
    xs = jnp.concatenate([proj4[0:1, Mp:], stack4_s, v_s[None]], axis=0)
    xs = xs.reshape(6, DB, TS, RW_N, RW_H).transpose(0, 2, 4, 3, 1)
    o_s, s_states = wkv_sample(xs, state_t, layer, s_states)
    o_s = o_s.transpose(3, 0, 2, 1).reshape(lay.Ms, D)
    v3 = (v_a, v_b, v_s)
    v_first_out = v_first if vres else v3

    ln = _head_minor(jnp.stack([w["ln_w"], w["ln_b"]]))
    y = rw_post(lay, (o_a, o_b, o_s), v3, (rk_a, rk_b, rk_s), proj4, ln)
    w_out = w["w_out"].reshape(RW_H, RW_N, D).swapaxes(0, 1).reshape(D, D)
    o_proj = matmul2d(y, w_out.astype(BF16), name="rw_out")
    p_shift = h[T - 1:Mp:T]
    s_shift = h[Mp:].reshape(DB, TS, D)[:, -1]
    return o_proj, _tiles_to_state(p_st), p_shift, s_states, s_shift, v_first_out


def _rot_cols(w):
    half = MLA_ROPE // 2
    return jnp.concatenate([-w[..., half:], w[..., :half]], axis=-1)


def _mla_prep_kernel(pq_ref, pkv_ref, pkr_ref, gq_ref, gkv_ref, cs_ref,
                     cq_ref, ckv_ref, kr_ref, krb_ref):
    cq_ref[...] = _rms(pq_ref[...], gq_ref[...]).astype(BF16)
    ckv_ref[...] = _rms(pkv_ref[...], gkv_ref[...])
    t2 = pkr_ref[...]
    kr = t2 * cs_ref[0] + pltpu.roll(t2, MLA_ROPE, 1) * cs_ref[1]
    kr_ref[...] = kr[:, :MLA_ROPE]
    krb_ref[...] = kr.astype(BF16)


def mla_prep(lay, proj, g_q, g_kv, cs):
    te = lay.te
    M = lay.M
    return pl.pallas_call(
        _mla_prep_kernel,
        out_shape=(jax.ShapeDtypeStruct((M, MLA_QL), BF16),
                   jax.ShapeDtypeStruct((M, MLA_KVL), F32),
                   jax.ShapeDtypeStruct((M, MLA_ROPE), F32),
                   jax.ShapeDtypeStruct((M, LANE), BF16)),
        grid=(lay.nblocks,),
        in_specs=[pl.BlockSpec((te, MLA_QL), lambda i: (i, D // MLA_QL)),
                  pl.BlockSpec((te, MLA_KVL), lambda i: (i, D // MLA_KVL + 1)),
                  pl.BlockSpec((te, LANE), lambda i: (i, (D + MLA_QL + MLA_KVL) // LANE)),
                  pl.BlockSpec((1, MLA_QL), lambda i: (0, 0)),
                  pl.BlockSpec((1, MLA_KVL), lambda i: (0, 0)),
                  pl.BlockSpec((2, te, LANE), lambda i: (0, i, 0))],
        out_specs=(pl.BlockSpec((te, MLA_QL), lambda i: (i, 0)),
                   pl.BlockSpec((te, MLA_KVL), lambda i: (i, 0)),
                   pl.BlockSpec((te, MLA_ROPE), lambda i: (i, 0)),
                   pl.BlockSpec((te, LANE), lambda i: (i, 0))),
        compiler_params=_cparams(("parallel",)),
        name="mla_prep",
    )(proj, proj, proj, g_q[None], g_kv[None], cs)


def _mla_q_kernel(q_ref, cs_ref, qn_ref, qr_ref):
    for h in range(MLA_H):
        qn_ref[:, h * LANE:(h + 1) * LANE] = q_ref[:, 2 * h * LANE:(2 * h + 1) * LANE].astype(BF16)
        t2 = q_ref[:, (2 * h + 1) * LANE:(2 * h + 2) * LANE]
        qr = t2 * cs_ref[0] + pltpu.roll(t2, MLA_ROPE, 1) * cs_ref[1]
        qr_ref[:, h * LANE:(h + 1) * LANE] = qr.astype(BF16)


def mla_q(lay, q, cs):
    te = lay.te
    M = lay.M
    return pl.pallas_call(
        _mla_q_kernel,
        out_shape=(jax.ShapeDtypeStruct((M, MLA_H * LANE), BF16),
                   jax.ShapeDtypeStruct((M, MLA_H * LANE), BF16)),
        grid=(lay.nblocks,),
        in_specs=[pl.BlockSpec((te, MLA_H * 2 * LANE), lambda i: (i, 0)),
                  pl.BlockSpec((2, te, LANE), lambda i: (0, i, 0))],
        out_specs=(pl.BlockSpec((te, MLA_H * LANE), lambda i: (i, 0)),
                   pl.BlockSpec((te, MLA_H * LANE), lambda i: (i, 0))),
        compiler_params=_cparams(("parallel",)),
        name="mla_q",
    )(q, cs)


FLASH_HEADS = 2


def _flash_kernel(qn_ref, qr_ref, kn_ref, kr_ref, v_ref, o_ref, *, tq, tk):
    qi = pl.program_id(2)
    nt = (((1,), (1,)), ((), ()))
    hd = lambda u: slice(u * LANE, (u + 1) * LANE)
    qs = [jnp.concatenate([qn_ref[:, hd(u)], qr_ref[:, hd(u)]], axis=1) for u in range(FLASH_HEADS)]
    c = MLA_SCALE * math.log2(math.e)
    per_q = tq // tk

    def update(start, carry, diag):
        kr = kr_ref[pl.ds(start, tk), :]
        ss = [lax.dot_general(qs[u], jnp.concatenate([kn_ref[pl.ds(start, tk), hd(u)], kr], axis=1),
                              nt, preferred_element_type=F32) * c for u in range(FLASH_HEADS)]
        out = []
        for u in range(FLASH_HEADS):
            m, l, acc = carry[u]
            s = ss[u]
            if diag is not None:
                s = jnp.where(lax.broadcasted_iota(jnp.int32, (tq, tk), 1) + diag * tk
                              <= lax.broadcasted_iota(jnp.int32, (tq, tk), 0), s, -1e30)
            m_new = jnp.maximum(m, jnp.max(s, axis=1, keepdims=True))
            p = jnp.exp2(s - m_new)
            alpha = jnp.exp2(m - m_new)
            l = alpha * l + jnp.sum(p, axis=1, keepdims=True)
            acc = alpha * acc + jnp.dot(p.astype(BF16), v_ref[pl.ds(start, tk), hd(u)],
                                        preferred_element_type=F32)
            out.append((m_new, l, acc))
        return tuple(out)

    def body(ki, carry):
        return update(pl.multiple_of(ki * tk, tk), carry, None)

    init = (jnp.full((tq, 1), -1e30, F32), jnp.zeros((tq, 1), F32), jnp.zeros((tq, MLA_V), F32))
    carry = lax.fori_loop(0, qi * per_q, body, (init,) * FLASH_HEADS)
    for d in range(per_q):
        carry = update(pl.multiple_of(qi * tq + d * tk, tk), carry, d)
    for u in range(FLASH_HEADS):
        _, l, acc = carry[u]
        o_ref[:, hd(u)] = acc / l


def mla_flash(qn, qr, knv, krb, B, T):
    tq = _tile(T, 512)
    tk = tq
    nq = T // tq
    hw = FLASH_HEADS * LANE
    groups = MLA_H // FLASH_HEADS
    return pl.pallas_call(
        functools.partial(_flash_kernel, tq=tq, tk=tk),
        out_shape=jax.ShapeDtypeStruct((B * T, MLA_H * MLA_V), F32),
        grid=(B, groups, nq),
        in_specs=[pl.BlockSpec((tq, hw), lambda b, h, i: (b * nq + i, h)),
                  pl.BlockSpec((tq, hw), lambda b, h, i: (b * nq + i, h)),
                  pl.BlockSpec((T, hw), lambda b, h, i: (b, h)),
                  pl.BlockSpec((T, LANE), lambda b, h, i: (b, 0)),
                  pl.BlockSpec((T, hw), lambda b, h, i: (b, groups + h))],
        out_specs=pl.BlockSpec((tq, hw), lambda b, h, i: (b * nq + i, h)),
        compiler_params=_cparams(("parallel", "parallel", "arbitrary")),
        name="mla_flash",
    )(qn, qr, knv, krb, knv)


def _paged_kernel(pt_ref, ql_ref, qr_ref, cn_ref, kn_ref, *rest, pg, ts):
    ck_refs = rest[:pg]
    kr_refs = rest[pg:2 * pg]
    o_ref, m_ref, l_ref, acc_ref = rest[2 * pg:]
    p = pl.program_id(1)
    nt = (((1,), (1,)), ((), ()))

    @pl.when(p == 0)
    def _():
        m_ref[...] = jnp.full(m_ref.shape, -1e30, F32)
        l_ref[...] = jnp.zeros(l_ref.shape, F32)
        acc_ref[...] = jnp.zeros(acc_ref.shape, F32)

    rows = ts * MLA_H
    ql = ql_ref[...]
    qr = qr_ref[...][:, :MLA_ROPE]
    cks = [r[...].astype(BF16) for r in ck_refs]
    ck = jnp.concatenate(cks, axis=0)
    kr_t = jnp.concatenate([r[...].astype(BF16) for r in kr_refs], axis=1)
    s = jnp.concatenate([lax.dot_general(ql, c, nt, preferred_element_type=F32) for c in cks], axis=1)
    s = (s + jnp.dot(qr, kr_t, preferred_element_type=F32)) * MLA_SCALE
    m = m_ref[...]
    m_new = jnp.maximum(m, jnp.max(s, axis=1, keepdims=True))
    pr = jnp.exp(s - m_new)
    alpha = jnp.exp(m - m_new)
    l = alpha * l_ref[...] + jnp.sum(pr, axis=1, keepdims=True)
    acc = alpha * acc_ref[...] + jnp.dot(pr.astype(BF16), ck, preferred_element_type=F32)
    m = m_new
    m_ref[...] = m
    l_ref[...] = l
    acc_ref[...] = acc

    @pl.when(p == pl.num_programs(1) - 1)
    def _():
        qlf = ql.astype(F32)
        qrf = qr.astype(F32)
        tok = lax.broadcasted_iota(jnp.int32, (ts * MLA_H, 1), 0) // MLA_H
        cols = []
        for j in range(ts):
            cn = cn_ref[j:j + 1, :].astype(BF16).astype(F32)
            kn = kn_ref[j:j + 1, :MLA_ROPE].astype(F32)
            sj = (jnp.sum(qlf * cn, axis=1, keepdims=True)
                  + jnp.sum(qrf * kn, axis=1, keepdims=True)) * MLA_SCALE
            cols.append(jnp.where(tok >= j, sj, -1e30))
        m2 = m
        for sj in cols:
            m2 = jnp.maximum(m2, sj)
        alpha = jnp.exp(m - m2)
        l2 = alpha * l
        acc2 = alpha * acc
        for j, sj in enumerate(cols):
            pj = jnp.exp(sj - m2)
            l2 = l2 + pj
            acc2 = acc2 + pj.astype(BF16).astype(F32) * cn_ref[j:j + 1, :].astype(BF16).astype(F32)
        o_ref[...] = (acc2 / l2).astype(o_ref.dtype)


def mla_paged(page_table, ql, qr, ckv_new, krb_new, cache_ckv, cache_kr, layer, DB, TS):
    n_pages = page_table.shape[1]
    pg = _tile(n_pages, 32) if n_pages >= 8 else n_pages
    rows = TS * MLA_H
    pt = page_table.reshape(-1)

    def page_spec(i, shape):
        return pl.BlockSpec((None, None) + shape,
                            lambda b, p, pt_ref: (layer, pt_ref[b * n_pages + p * pg + i], 0, 0))

    cache_kr_t = jnp.swapaxes(cache_kr, 2, 3)
    per_b = lambda width: pl.BlockSpec((None, rows, width), lambda b, p, pt_ref: (b, 0, 0))
    new_b = lambda width: pl.BlockSpec((None, TS, width), lambda b, p, pt_ref: (b, 0, 0))
    grid_spec = pltpu.PrefetchScalarGridSpec(
        num_scalar_prefetch=1,
        grid=(DB, n_pages // pg),
        in_specs=[per_b(MLA_KVL), per_b(LANE), new_b(MLA_KVL), new_b(LANE)]
                 + [page_spec(i, (PAGE, MLA_KVL)) for i in range(pg)]
                 + [page_spec(i, (MLA_ROPE, PAGE)) for i in range(pg)],
        out_specs=per_b(MLA_KVL),
        scratch_shapes=[pltpu.VMEM((rows, 1), F32), pltpu.VMEM((rows, 1), F32),
                        pltpu.VMEM((rows, MLA_KVL), F32)],
    )
    return pl.pallas_call(
        functools.partial(_paged_kernel, pg=pg, ts=TS),
        out_shape=jax.ShapeDtypeStruct((DB, rows, MLA_KVL), BF16),
        grid_spec=grid_spec,
        compiler_params=_cparams(("parallel", "arbitrary")),
        name="mla_paged",
    )(pt, ql, qr, ckv_new, krb_new, *([cache_ckv] * pg), *([cache_kr_t] * pg))


def _pair_specs(lay):
    te, npb = lay.te, lay.n_p_blocks
    return [pl.BlockSpec((te, D), lambda i: (jnp.minimum(i, npb - 1), 0)),
            pl.BlockSpec((te, D), lambda i: (jnp.maximum(i - npb, 0), 0))]


def _gate_kernel(op_ref, os_ref, z_ref, y_ref, *, n_p_blocks):
    o = _read_rows((op_ref, os_ref), n_p_blocks)
    y_ref[...] = (o * _silu(z_ref[...])).astype(BF16)


def gate_mul(lay, o_pair, zsrc, zcol):
    te = lay.te
    return pl.pallas_call(
        functools.partial(_gate_kernel, n_p_blocks=lay.n_p_blocks),
        out_shape=jax.ShapeDtypeStruct((lay.M, D), BF16),
        grid=(lay.nblocks,),
        in_specs=_pair_specs(lay) + [pl.BlockSpec((te, D), lambda i: (i, zcol))],
        out_specs=pl.BlockSpec((te, D), lambda i: (i, 0)),
        compiler_params=_cparams(("parallel",)),
        name="gate_mul",
    )(*o_pair, zsrc)


def mla_layer(lay, h, cache_ckv, cache_kr, layer, page_table, w):
    B, T, DB, TS, Mp, M = lay.B, lay.T, lay.DB, lay.TS, lay.Mp, lay.M
    w_in = w["w_in"]
    w_kr = w_in[:, 2 * MLA_QL:2 * MLA_QL + MLA_ROPE]
    w1 = jnp.concatenate([w_in[:, 2 * MLA_QL + MLA_ROPE:], w_in[:, :2 * MLA_QL],
                          w_kr, _rot_cols(w_kr)], axis=1)
    proj = matmul2d(h, w1.astype(BF16), tn=640, name="mla_proj")
    pos = jnp.concatenate([jnp.tile(jnp.arange(T, dtype=F32), B),
                           jnp.tile(page_table.shape[1] * PAGE + jnp.arange(TS, dtype=F32), DB)])
    half = MLA_ROPE // 2
    inv_freq = ROPE_THETA ** (-jnp.arange(half, dtype=F32) / half)
    ang = pos[:, None] * inv_freq[None, :]
    zeros = jnp.zeros((M, MLA_ROPE), F32)
    cs = jnp.stack([jnp.concatenate([jnp.cos(ang), jnp.cos(ang), zeros], axis=1),
                    jnp.concatenate([jnp.sin(ang), jnp.sin(ang), zeros], axis=1)])
    cq, ckv, kr, krb = mla_prep(lay, proj, w["q_norm"], w["kv_norm"], cs)
    w_uq = w["w_uq"]
    wq = jnp.concatenate([w_uq, _rot_cols(w_uq[..., MLA_NOPE:])], axis=-1)
    q = matmul2d(cq, wq.reshape(MLA_QL, MLA_H * 2 * LANE).astype(BF16), name="mla_qproj")
    qn, qr = mla_q(lay, q, cs)
    w_kv = jnp.concatenate([w["w_uk"].reshape(MLA_KVL, -1), w["w_uv"].reshape(MLA_KVL, -1)], axis=1)
    knv = matmul2d(ckv[:Mp], w_kv.astype(BF16), out_dtype=BF16, name="mla_kv")
    o_p = mla_flash(qn, qr, knv, krb, B, T)
    w_ukT = w["w_uk"].transpose(1, 2, 0).astype(BF16)
    ql = head_matmul(qn[Mp:], w_ukT, BF16, "mla_qlat")
    o_lat = mla_paged(page_table, ql.reshape(DB, TS * MLA_H, MLA_KVL),
                      qr[Mp:].reshape(DB, TS * MLA_H, LANE),
                      ckv[Mp:].reshape(DB, TS, MLA_KVL), krb[Mp:].reshape(DB, TS, LANE),
                      cache_ckv, cache_kr, layer, DB, TS)
    o_s = head_matmul(o_lat.reshape(lay.Ms, MLA_H * MLA_KVL),
                      w["w_uv"].transpose(1, 0, 2).astype(BF16), F32, "mla_ouv")
    y = gate_mul(lay, (o_p, o_s), proj, 0)
    o_proj = matmul2d(y, w["w_out"].astype(BF16), name="mla_out")
    return (o_proj, ckv[:Mp].reshape(B, T, MLA_KVL), kr[:Mp].reshape(B, T, MLA_ROPE),
            ckv[Mp:].reshape(DB, TS, MLA_KVL), kr[Mp:].reshape(DB, TS, MLA_ROPE))


def _gla_kernel(q_ref, k_ref, v_ref, g_ref, s0_ref, o_ref, sT_ref, st_ref, *, L):
    c = pl.program_id(1)
    C = GLA_SUB
    assert L % C == 0
    heads = range(GLA_H)
    dk = lambda h: slice(h * GLA_DK, (h + 1) * GLA_DK)
    dv = lambda h: slice(h * GLA_DV, (h + 1) * GLA_DV)

    @pl.when(c == 0)
    def _():
        for h in heads:
            st_ref[h] = s0_ref[h].T

    nt = (((1,), (1,)), ((), ()))
    tn = (((0,), (0,)), ((), ()))
    trow = lax.broadcasted_iota(jnp.int32, (L, 1), 0)
    crow = lax.broadcasted_iota(jnp.int32, (C, 1), 0)
    col = lax.broadcasted_iota(jnp.int32, (C, L), 1)
    q = [q_ref[:, dk(h)] * (GLA_DK ** -0.5) for h in heads]
    k = [k_ref[:, dk(h)] for h in heads]
    v = [v_ref[:, dv(h)].astype(BF16) for h in heads]
    b = []
    for h in heads:
        bh = g_ref[:, dk(h)]
        sh = 1
        while sh < L:
            bh = bh + jnp.where(trow >= sh, pltpu.roll(bh, sh, 0), 0.0)
            sh *= 2
        b.append(bh)
    st = [st_ref[h] for h in heads]
    o = [lax.dot_general((q[h] * jnp.exp(b[h])).astype(BF16), st[h].astype(BF16), nt,
                         preferred_element_type=F32) for h in heads]
    att_rows = [[] for _ in heads]
    for i in range(L // C):
        lo = i * C
        if i == 0:
            att_i = [jnp.zeros((C, L), F32) for _ in heads]
        else:
            att_i = []
            for h in heads:
                beta = b[h][lo - 1:lo, :]
                q_in = (q[h][lo:lo + C] * jnp.exp(b[h][lo:lo + C] - beta)).astype(BF16)
                k_out = (k[h] * jnp.exp(jnp.where(trow < lo, beta - b[h], -jnp.inf))).astype(BF16)
                att_i.append(lax.dot_general(q_in, k_out, nt, preferred_element_type=F32))
        for s in range(C):
            for h in heads:
                bi = b[h][lo:lo + C]
                e = jnp.where(crow >= s, bi - bi[s:s + 1, :], -jnp.inf)
                a_s = jnp.sum(q[h][lo:lo + C] * k[h][lo + s:lo + s + 1, :] * jnp.exp(e),
                              axis=1, keepdims=True)
                att_i[h] = jnp.where(col == lo + s, a_s, att_i[h])
        for h in heads:
            att_rows[h].append(att_i[h])
    for h in heads:
        att = att_rows[h][0] if len(att_rows[h]) == 1 else jnp.concatenate(att_rows[h], axis=0)
        o_ref[:, dv(h)] = o[h] + jnp.dot(att.astype(BF16), v[h], preferred_element_type=F32)
    for h in heads:
        b_end = b[h][L - 1:L, :]
        kd = (k[h] * jnp.exp(b_end - b[h])).astype(BF16)
        st[h] = st[h] * jnp.exp(b_end) + lax.dot_general(v[h], kd, tn, preferred_element_type=F32)
        st_ref[h] = st[h]

    @pl.when(c == pl.num_programs(1) - 1)
    def _():
        for h in heads:
            sT_ref[h] = st[h].T


def gla_scan(q, k, v, g, s0, *, nseq, nchunk, L, kcol, vcol):
    rows = lambda n, c: n * nchunk + c
    st_spec = pl.BlockSpec((None, GLA_H, GLA_DK, GLA_DV), lambda n, c: (n, 0, 0, 0))
    return pl.pallas_call(
        functools.partial(_gla_kernel, L=L),
        out_shape=(jax.ShapeDtypeStruct((q.shape[0], GLA_H * GLA_DV), F32),
                   jax.ShapeDtypeStruct((nseq, GLA_H, GLA_DK, GLA_DV), F32)),
        grid=(nseq, nchunk),
        in_specs=[pl.BlockSpec((L, GLA_QK), lambda n, c: (rows(n, c), 0)),
                  pl.BlockSpec((L, GLA_QK), lambda n, c: (rows(n, c), kcol)),
                  pl.BlockSpec((L, GLA_H * GLA_DV), lambda n, c: (rows(n, c), vcol)),
                  pl.BlockSpec((L, GLA_QK), lambda n, c: (rows(n, c), 0)),
                  st_spec],
        out_specs=(pl.BlockSpec((L, GLA_H * GLA_DV), lambda n, c: (rows(n, c), 0)), st_spec),
        scratch_shapes=[pltpu.VMEM((GLA_H, GLA_DV, GLA_DK), F32)],
        compiler_params=_cparams(("parallel", "arbitrary")),
        name="gla_scan",
    )(q, k, v, g, s0)


def _gla_post_kernel(op_ref, os_ref, z_ref, g_ref, y_ref, *, n_p_blocks):
    o = _read_rows((op_ref, os_ref), n_p_blocks)
    for h in range(GLA_H):
        hs = slice(h * GLA_DV, (h + 1) * GLA_DV)
        y_ref[:, hs] = (_rms(o[:, hs], g_ref[...]) * _silu(z_ref[:, hs])).astype(BF16)


def gla_post(lay, o_pair, proj, g_norm):
    te = lay.te
    return pl.pallas_call(
        functools.partial(_gla_post_kernel, n_p_blocks=lay.n_p_blocks),
        out_shape=jax.ShapeDtypeStruct((lay.M, D), BF16),
        grid=(lay.nblocks,),
        in_specs=_pair_specs(lay) + [pl.BlockSpec((te, D), lambda i: (i, 2)),
                                     pl.BlockSpec((1, GLA_DV), lambda i: (0, 0))],
        out_specs=pl.BlockSpec((te, D), lambda i: (i, 0)),
        compiler_params=_cparams(("parallel",)),
        name="gla_post",
    )(*o_pair, proj, g_norm[None])


def gla_layer(lay, h, state, w):
    B, T, DB, TS, Mp, M = lay.B, lay.T, lay.DB, lay.TS, lay.Mp, lay.M
    w_in = _pad_cols(w["w_in"], 2 * GLA_QK + 2 * D + GLA_LORA_PAD)
    proj = matmul2d(h, w_in.astype(BF16), tn=640, name="gla_proj")
    gl = proj[:, 2 * GLA_QK + 2 * D:]
    g = matmul2d(gl, _pad_rows(w["w_g2"], GLA_LORA_PAD).astype(BF16), w["b_g"],
                 act_out="logsig_tau", name="gla_gate")
    L = math.gcd(T, GLA_CHUNK)
    cols = dict(kcol=1, vcol=2 * GLA_QK // (GLA_H * GLA_DV))
    o_p, p_state = gla_scan(proj, proj, proj, g, jnp.zeros((B, GLA_H, GLA_DK, GLA_DV), F32),
                            nseq=B, nchunk=T // L, L=L, **cols)
    LS = GLA_SUB
    pad = lambda a: jnp.pad(a.reshape(DB, TS, -1), ((0, 0), (0, LS - TS), (0, 0))).reshape(DB * LS, -1)
    ps = pad(proj[Mp:, :2 * GLA_QK + D])
    o_s, s_state = gla_scan(ps, ps, ps, pad(g[Mp:]), state, nseq=DB, nchunk=1, L=LS, **cols)
    o_s = o_s.reshape(DB, LS, D)[:, :TS].reshape(lay.Ms, D)
    y = gla_post(lay, (o_p, o_s), proj, w["norm"])
    o_proj = matmul2d(y, w["w_out"].astype(BF16), name="gla_out")
    return o_proj, p_state, s_state


def kernel(x_prompt, x_sample, c_prompt, c_sample, state_rwkv_wkv, state_rwkv_shift, cache_mla_ckv, cache_mla_krope, page_table, state_gla, norm_pre, norm_post, ada_w, ada_b, rw_mu, rw_w_in, rw_w0, rw_w1, rw_w2, rw_a0, rw_a1, rw_a2, rw_v0, rw_v1, rw_v2, rw_k_k, rw_k_a, rw_r_k, rw_ln_w, rw_ln_b, rw_w_out, mla_w_in, mla_q_norm, mla_kv_norm, mla_w_uq, mla_w_uk, mla_w_uv, mla_w_out, gla_w_in, gla_w_g2, gla_b_g, gla_norm, gla_w_out):
    B, T, _ = x_prompt.shape
    DB, TS, _ = x_sample.shape
    depth = norm_pre.shape[0]
    lay = RowLayout(B, T, DB, TS)
    Mp = lay.Mp
    x = (x_prompt.reshape(Mp, D), x_sample.reshape(lay.Ms, D))
    c = jnp.concatenate([c_prompt, c_sample], axis=0)[None]
    c = jnp.broadcast_to(c, (depth,) + c.shape[1:])
    mods = matmul(c, ada_w, ada_b[:, None, :], act_in="silu", name="ada")

    p_wkv, p_shift, p_ckv, p_kr, p_gla = [], [], [], [], []
    s_shift, s_ckv, s_kr, s_gla = [], [], [], []
    v_first = None
    state_t = jnp.transpose(state_rwkv_wkv, (0, 2, 3, 4, 1))
    s_wkv_t = None
    mod_of = [lay.expand_mod(mods[i]) for i in range(depth)]
    h_dtype = lambda i: F32 if i % 3 == 0 else BF16

    def mix_of(i):
        if i % 3 != 0:
            return None
        return jnp.repeat(state_rwkv_shift[i // 3], TS, axis=0), rw_mu[i // 3]

    h, x6 = norm_step(lay, x, pre=(norm_pre[0], mod_of[0], h_dtype(0)), mix=mix_of(0))
    for i in range(depth):
        kind, j = i % 3, i // 3
        if kind == 0:
            w = dict(mu=rw_mu[j], w_in=rw_w_in[j], w0=rw_w0[j], w1=rw_w1[j], w2=rw_w2[j],
                     a0=rw_a0[j], a1=rw_a1[j], a2=rw_a2[j], k_k=rw_k_k[j], k_a=rw_k_a[j],
                     r_k=rw_r_k[j], ln_w=rw_ln_w[j], ln_b=rw_ln_b[j], w_out=rw_w_out[j])
            if j > 0:
                w.update(v0=rw_v0[j - 1], v1=rw_v1[j - 1], v2=rw_v2[j - 1])
            o, pst, psh, s_wkv_t, ssh, v_first = rwkv_layer(
                lay, h, x6, state_t, j, s_wkv_t, v_first if j > 0 else None, w)
            p_wkv.append(pst)
            p_shift.append(psh)
            s_shift.append(ssh)
        elif kind == 1:
            w = dict(w_in=mla_w_in[j], q_norm=mla_q_norm[j], kv_norm=mla_kv_norm[j],
                     w_uq=mla_w_uq[j], w_uk=mla_w_uk[j], w_uv=mla_w_uv[j], w_out=mla_w_out[j])
            o, pc, pk, sc, sk = mla_layer(lay, h, cache_mla_ckv, cache_mla_krope, j, page_table, w)
            p_ckv.append(pc)
            p_kr.append(pk)
            s_ckv.append(sc)
            s_kr.append(sk)
        else:
            w = dict(w_in=gla_w_in[j], w_g2=gla_w_g2[j], b_g=gla_b_g[j], norm=gla_norm[j],
                     w_out=gla_w_out[j])
            o, pg, sg = gla_layer(lay, h, state_gla[j], w)
            p_gla.append(pg)
            s_gla.append(sg)
        res = (o, norm_post[i], mod_of[i])
        if i + 1 < depth:
            outs = norm_step(lay, x, res=res, pre=(norm_pre[i + 1], mod_of[i + 1], h_dtype(i + 1)),
                             mix=mix_of(i + 1))
            x, h = outs[0], outs[1]
            x6 = outs[2] if len(outs) > 2 else None
        else:
            yp, ys = norm_step(lay, x, res=res, split_out=True)
    yp = yp.reshape(B, T, D)
    ys = ys.reshape(DB, TS, D)
    return (yp, ys,
            jnp.stack(p_wkv), jnp.stack(p_shift), jnp.stack(p_ckv), jnp.stack(p_kr), jnp.stack(p_gla),
            jnp.transpose(s_wkv_t, (0, 4, 1, 2, 3)), jnp.stack(s_shift), jnp.stack(s_ckv),
            jnp.stack(s_kr), jnp.stack(s_gla))
```
